```python
import jax, jax.numpy as jnp
from jax import lax
import numpy as np

D_MODEL = 1024
BATCH = 4
SEQ = 4096
DEPTH = 4

GRID_W = 64
CTX_LEN = 256
HEAD_DIM = 64
A_GROUPS = 4
A_W = A_GROUPS * HEAD_DIM
CHUNK = 128
ATT_HEADS = 8
ATT_KV_HEADS = 2
ATT_Q = ATT_HEADS * HEAD_DIM
ATT_KV = ATT_KV_HEADS * HEAD_DIM
WINDOW = 128
QBLOCK = 128
ROPE_THETA = 10000.0
ROPE_FREQS = HEAD_DIM // 4
NEG_INF = -1e30
RW_HEADS = 4
RW_W = RW_HEADS * HEAD_DIM
W_LORA = 64
A_LORA = 64
G_LORA = 128
N_DIR = 2
GN_EPS = 64e-5
MIX_W = A_W + ATT_Q + RW_W
D_FF = -(-8 * D_MODEL // (3 * 256)) * 256
RMS_EPS = 1e-6
OFF_Q = 2 * A_W
OFF_R = OFF_Q + ATT_Q
OFF_CTX = OFF_R + RW_W + G_LORA
OFF_RWK = OFF_CTX + 2 * ATT_KV
IN_COLS = OFF_RWK + 2 * RW_W + N_DIR * (W_LORA + A_LORA)
MU_HEAD = RW_W + G_LORA
MU_W = MU_HEAD + 2 * RW_W + N_DIR * (W_LORA + A_LORA)

kernel_name = 'hybrid_flow_trunk'


def rms_norm(x, g):
    xf = x.astype(jnp.float32)
    y = xf * lax.rsqrt(jnp.mean(xf * xf, axis=-1, keepdims=True) + RMS_EPS)
    return (y * g.astype(jnp.float32)).astype(x.dtype)


def layer_norm(x, g, b):
    xf = x.astype(jnp.float32)
    mu = jnp.mean(xf, axis=-1, keepdims=True)
    var = jnp.mean(jnp.square(xf - mu), axis=-1, keepdims=True)
    return ((xf - mu) * lax.rsqrt(var + 1e-5) * g + b).astype(x.dtype)


def modulate(x, shift, scale):
    return x * (1.0 + scale) + shift


def swiglu(h, w_gu, w_down):
    gu = h @ w_gu
    return (jax.nn.silu(gu[..., :D_FF]) * gu[..., D_FF:]) @ w_down


def spatial_gating(z, ln_g, ln_b, w_s, b_s):
    z = jax.nn.gelu(z)
    u, v = z[..., :A_W], z[..., A_W:]
    B, T, _ = v.shape
    v = layer_norm(v, ln_g, ln_b).reshape(B, T // CHUNK, CHUNK, A_GROUPS, HEAD_DIM)
    mixed = jnp.einsum('gpq,bnqgd->bnpgd', w_s, v) + b_s.T[None, None, :, :, None]
    return u * mixed.reshape(B, T, A_W)


def rope_tables(T):
    t = jnp.arange(T)
    pos = jnp.stack([t // GRID_W, t % GRID_W], axis=-1).astype(jnp.float32)
    inv = ROPE_THETA ** (-jnp.arange(ROPE_FREQS, dtype=jnp.float32) / ROPE_FREQS)
    ang = pos[:, :, None] * inv
    return jnp.cos(ang), jnp.sin(ang)


def apply_rope(x, cos, sin):
    B, T, H, Dh = x.shape
    xr = x.astype(jnp.float32).reshape(B, T, H, 2, 2, ROPE_FREQS)
    x1, x2 = xr[..., 0, :], xr[..., 1, :]
    c, s = cos[None, :, None], sin[None, :, None]
    out = jnp.stack([x1 * c - x2 * s, x2 * c + x1 * s], axis=-2)
    return out.reshape(B, T, H, Dh).astype(x.dtype)


def window_attention(q, k, v, kc, vc, sink):
    B, T, H, Dh = q.shape
    G = H // ATT_KV_HEADS
    nb = T // QBLOCK
    C = kc.shape[1]
    scale = Dh ** -0.5
    qb = q.reshape(B, nb, QBLOCK, ATT_KV_HEADS, G, Dh)

    def band(t):
        tp = jnp.pad(t, ((0, 0), (QBLOCK, QBLOCK), (0, 0), (0, 0))).reshape(B, nb + 2, QBLOCK, ATT_KV_HEADS, Dh)
        return jnp.concatenate([tp[:, :-2], tp[:, 1:-1], tp[:, 2:]], axis=2)

    kw, vw = band(k), band(v)
    s_loc = jnp.einsum('bnqhgd,bnkhd->bnhgqk', qb, kw).astype(jnp.float32) * scale
    s_ctx = jnp.einsum('bnqhgd,bchd->bnhgqc', qb, kc).astype(jnp.float32) * scale
    qi = jnp.arange(QBLOCK)[:, None]
    kj = jnp.arange(3 * QBLOCK)[None, :]
    in_band = jnp.abs(kj - QBLOCK - qi) <= WINDOW
    kpos = jnp.arange(nb)[:, None] * QBLOCK - QBLOCK + jnp.arange(3 * QBLOCK)[None, :]
    in_seq = (kpos >= 0) & (kpos < T)
    mask = in_band[None] & in_seq[:, None, :]
    s_loc = jnp.where(mask[None, :, None, None], s_loc, NEG_INF)
    sink_l = jnp.broadcast_to(sink.astype(jnp.float32).reshape(ATT_KV_HEADS, G, 1, 1), s_ctx.shape[:-1] + (1,))
    p = jax.nn.softmax(jnp.concatenate([sink_l, s_ctx, s_loc], axis=-1), axis=-1)
    p_ctx = p[..., 1:1 + C].astype(v.dtype)
    p_loc = p[..., 1 + C:].astype(v.dtype)
    o = jnp.einsum('bnhgqc,bchd->bnqhgd', p_ctx, vc) + jnp.einsum('bnhgqk,bnkhd->bnqhgd', p_loc, vw)
    return o.reshape(B, T, H * Dh)


def context_attention(qc, kc, vc, sink):
    B, C, H, Dh = qc.shape
    G = H // ATT_KV_HEADS
    qg = qc.reshape(B, C, ATT_KV_HEADS, G, Dh)
    s = jnp.einsum('bqhgd,bkhd->bhgqk', qg, kc).astype(jnp.float32) * Dh ** -0.5
    sink_l = jnp.broadcast_to(sink.astype(jnp.float32).reshape(ATT_KV_HEADS, G, 1, 1), s.shape[:-1] + (1,))
    p = jax.nn.softmax(jnp.concatenate([sink_l, s], axis=-1), axis=-1)[..., 1:].astype(vc.dtype)
    return jnp.einsum('bhgqk,bkhd->bqhgd', p, vc).reshape(B, C, H * Dh)


def grid_shift(f):
    B, T, C = f.shape
    rows = T // GRID_W
    g = f.reshape(B, rows, GRID_W, C)
    q = C // 4
    left = jnp.pad(g[:, :, :-1, :q], ((0, 0), (0, 0), (1, 0), (0, 0)))
    right = jnp.pad(g[:, :, 1:, q:2 * q], ((0, 0), (0, 0), (0, 1), (0, 0)))
    up = jnp.pad(g[:, :-1, :, 2 * q:3 * q], ((0, 0), (1, 0), (0, 0), (0, 0)))
    down = jnp.pad(g[:, 1:, :, 3 * q:], ((0, 0), (0, 1), (0, 0), (0, 0)))
    return jnp.concatenate([left, right, up, down], axis=-1).reshape(B, T, C)


def seq_shift(f):
    h = f.shape[-1] // 2
    prev = jnp.pad(f[:, :-1, :h], ((0, 0), (1, 0), (0, 0)))
    nxt = jnp.pad(f[:, 1:, h:], ((0, 0), (0, 1), (0, 0)))
    return jnp.concatenate([prev, nxt], axis=-1)


def token_mix(f, shifted, mu):
    return f + (shifted - f) * mu


def heads(t):
    return t.reshape(t.shape[:-1] + (RW_HEADS, HEAD_DIM))


def rwkv_inputs(rwk, lp):
    f = rwk.astype(jnp.float32)
    base = 2 * RW_W
    k = f[..., :RW_W]
    v = f[..., RW_W:base]
    wd = f[..., base:base + N_DIR * W_LORA].reshape(f.shape[:-1] + (N_DIR, W_LORA))
    ad = f[..., base + N_DIR * W_LORA:].reshape(f.shape[:-1] + (N_DIR, A_LORA))
    kk = heads(k * lp['rw_kk'])
    kk = kk / jnp.maximum(jnp.sqrt(jnp.sum(kk * kk, axis=-1, keepdims=True)), 1e-12)
    z = lp['rw_w0'] + jnp.einsum('btdr,drc->btdc', jnp.tanh(wd), lp['rw_w2'])
    w = jnp.exp(-jnp.exp(-jax.nn.softplus(-z) - 0.5))
    a = jax.nn.sigmoid(lp['rw_a0'] + jnp.einsum('btdr,drc->btdc', ad, lp['rw_a2']))
    kd = k[..., None, :] * (1.0 + (a - 1.0) * lp['rw_ka'])
    return heads(v), kk, heads(w), heads(a), heads(kd)


def wkv_scan(w, k, v, kk, a, s0, reverse, r):
    tm = lambda t: jnp.moveaxis(t, 1, 0)

    def update(S, w_t, k_t, v_t, kk_t, a_t):
        sa = jnp.einsum('bhvk,bhk->bhv', S, kk_t)
        return (S * w_t[:, :, None, :] - sa[..., None] * (kk_t * a_t)[:, :, None, :]
                + v_t[..., None] * k_t[:, :, None, :])

    xs = (tm(w), tm(k), tm(v), tm(kk), tm(a))
    if r is None:
        S, _ = lax.scan(lambda S, x: (update(S, *x), None), s0, xs, reverse=reverse)
        return S, None

    def step(S, x):
        S = update(S, *x[1:])
        return S, jnp.einsum('bhvk,bhk->bhv', S, x[0])

    S, y = lax.scan(step, s0, (tm(r),) + xs, reverse=reverse)
    return S, jnp.moveaxis(y, 0, 1)


def rwkv_output(y, r, kd, v, gd, lp):
    B, T = y.shape[:2]
    mu = jnp.mean(y, axis=-1, keepdims=True)
    var = jnp.mean(jnp.square(y - mu), axis=-1, keepdims=True)
    yn = ((y - mu) * lax.rsqrt(var + GN_EPS)).reshape(B, T, RW_W) * lp['rw_lnx_g'] + lp['rw_lnx_b']
    bonus = jnp.sum(r[:, :, None] * kd * lp['rw_rk'], axis=-1, keepdims=True).sum(axis=2) * v
    g = jax.nn.sigmoid(gd) @ lp['rw_g2']
    return (yn + bonus.reshape(B, T, RW_W)) * g


def mix_tokens(h, hc, lp, last):
    B, T, _ = h.shape
    C = hc.shape[1]
    w_in = lp['w_in']
    mu = lp['rw_mu']
    px = h @ w_in
    pc_tail = hc @ w_in[:, OFF_CTX:]
    kc = pc_tail[..., :ATT_KV].reshape(B, C, ATT_KV_HEADS, HEAD_DIM)
    vc = pc_tail[..., ATT_KV:2 * ATT_KV].reshape(B, C, ATT_KV_HEADS, HEAD_DIM)
    rwk_c = pc_tail[..., 2 * ATT_KV:]
    vC, kkC, wC, aC, kdC = rwkv_inputs(token_mix(rwk_c, seq_shift(rwk_c), mu[MU_HEAD:]), lp)
    rC = None
    if not last:
        pc_head = hc @ w_in[:, :OFF_CTX]
        rg_c = pc_head[..., OFF_R:]
        rg_c = token_mix(rg_c, seq_shift(rg_c), mu[:MU_HEAD])
        rC = heads(rg_c[..., :RW_W].astype(jnp.float32))
    s0 = jnp.zeros((B, RW_HEADS, HEAD_DIM, HEAD_DIM), jnp.float32)
    Sc_f, yc_f = wkv_scan(wC[:, :, 0], kdC[:, :, 0], vC, kkC, aC[:, :, 0], s0, False, rC)
    Sc_b, yc_b = wkv_scan(wC[:, :, 1], kdC[:, :, 1], vC, kkC, aC[:, :, 1], s0, True, rC)

    cos, sin = rope_tables(T)
    q = apply_rope(px[..., OFF_Q:OFF_R].reshape(B, T, ATT_HEADS, HEAD_DIM), cos, sin)
    k = apply_rope(px[..., OFF_CTX:OFF_CTX + ATT_KV].reshape(B, T, ATT_KV_HEADS, HEAD_DIM), cos, sin)
    v = px[..., OFF_CTX + ATT_KV:OFF_RWK].reshape(B, T, ATT_KV_HEADS, HEAD_DIM)
    o_att = window_attention(q, k, v, kc, vc, lp['attn_sink'])
    o_sg = spatial_gating(px[..., :OFF_Q], lp['sg_ln_g'], lp['sg_ln_b'], lp['sg_w'], lp['sg_b'])
    rg = px[..., OFF_R:OFF_CTX]
    rg = token_mix(rg, grid_shift(rg), mu[:MU_HEAD])
    rwk = px[..., OFF_RWK:]
    vX, kkX, wX, aX, kdX = rwkv_inputs(token_mix(rwk, grid_shift(rwk), mu[MU_HEAD:]), lp)
    rX = heads(rg[..., :RW_W].astype(jnp.float32))
    _, y_f = wkv_scan(wX[:, :, 0], kdX[:, :, 0], vX, kkX, aX[:, :, 0], Sc_f, False, rX)
    _, y_b = wkv_scan(wX[:, :, 1], kdX[:, :, 1], vX, kkX, aX[:, :, 1], Sc_b, True, rX)
    o_rw = rwkv_output(y_f + y_b, rX, kdX, vX, rg[..., RW_W:], lp).astype(h.dtype)
    y = jnp.concatenate([o_sg, o_att, o_rw], axis=-1) @ lp['w_out']
    if last:
        return y, None

    qc = pc_head[..., OFF_Q:OFF_R].reshape(B, C, ATT_HEADS, HEAD_DIM)
    oc_att = context_attention(qc, kc, vc, lp['attn_sink'])
    oc_sg = spatial_gating(pc_head[..., :OFF_Q], lp['sg_ln_g'], lp['sg_ln_b'], lp['sg_w'], lp['sg_b'])
    oc_rw = rwkv_output(yc_f + yc_b, rC, kdC, vC, rg_c[..., RW_W:], lp).astype(hc.dtype)
    yc = jnp.concatenate([oc_sg, oc_att, oc_rw], axis=-1) @ lp['w_out']
    return y, yc


def trunk_layer(x, xc, c_act, cc_act, lp, last):
    D = D_MODEL
    g = lp['norm_g']
    sh1, sc1, gt1, sh2, sc2, gt2 = [m[:, None, :] for m in jnp.split(c_act @ lp['w_mod'] + lp['b_mod'], 6, axis=-1)]
    n_ctx = 2 if last else 6
    mc = jnp.split(cc_act @ lp['w_mod'][:, :n_ctx * D] + lp['b_mod'][:n_ctx * D], n_ctx)
    h = modulate(rms_norm(x, g[0]), sh1, sc1)
    hc = modulate(rms_norm(xc, g[0]), mc[0], mc[1])
    y, yc = mix_tokens(h, hc, lp, last)
    x = x + gt1 * rms_norm(y, g[1])
    h = modulate(rms_norm(x, g[2]), sh2, sc2)
    x = x + gt2 * rms_norm(swiglu(h, lp['w_gu'], lp['w_down']), g[3])
    if last:
        return x, xc
    xc = xc + mc[2] * rms_norm(yc, g[1])
    hc = modulate(rms_norm(xc, g[2]), mc[3], mc[4])
    xc = xc + mc[5] * rms_norm(swiglu(hc, lp['w_gu'], lp['w_down']), g[3])
    return x, xc


def setup_inputs(seed: int = 0) -> dict:
    key = jax.random.key(seed)
    k = jax.random.split(key, 27)
    D, L = D_MODEL, DEPTH
    f32 = jnp.float32
    nrm = lambda i, shape, s: jax.random.normal(k[i], shape, f32) * s
    return {
        'x': nrm(0, (BATCH, SEQ, D), 1.0),
        'c': nrm(1, (BATCH, D), 1.0),
        'ctx': nrm(2, (BATCH, CTX_LEN, D), 1.0),
        'c_ctx': nrm(3, (D,), 1.0),
        'w_mod': nrm(4, (L, D, 6 * D), 0.5 * D ** -0.5),
        'b_mod': nrm(5, (L, 6 * D), 0.01),
        'norm_g': 1.0 + nrm(6, (L, 4, D), 0.05),
        'w_in': nrm(7, (L, D, IN_COLS), D ** -0.5),
        'w_out': nrm(8, (L, MIX_W, D), MIX_W ** -0.5),
        'sg_ln_g': 1.0 + nrm(9, (L, A_W), 0.05),
        'sg_ln_b': nrm(10, (L, A_W), 0.01),
        'sg_w': nrm(11, (L, A_GROUPS, CHUNK, CHUNK), CHUNK ** -0.5),
        'sg_b': 1.0 + nrm(12, (L, A_GROUPS, CHUNK), 0.05),
        'attn_sink': nrm(13, (L, ATT_HEADS), 0.5),
        'rw_mu': jax.random.uniform(k[14], (L, MU_W), f32),
        'rw_w0': -1.0 + nrm(15, (L, N_DIR, RW_W), 0.5),
        'rw_w2': nrm(16, (L, N_DIR, W_LORA, RW_W), 0.5 * W_LORA ** -0.5),
        'rw_a0': nrm(17, (L, N_DIR, RW_W), 0.5),
        'rw_a2': nrm(18, (L, N_DIR, A_LORA, RW_W), 0.5 * A_LORA ** -0.5),
        'rw_kk': 0.85 + nrm(19, (L, RW_W), 0.05),
        'rw_ka': 1.0 + nrm(20, (L, RW_W), 0.05),
        'rw_rk': nrm(21, (L, RW_HEADS, HEAD_DIM), 0.1),
        'rw_g2': nrm(22, (L, G_LORA, RW_W), G_LORA ** -0.5),
        'rw_lnx_g': 1.0 + nrm(23, (L, RW_W), 0.05),
        'rw_lnx_b': nrm(24, (L, RW_W), 0.01),
        'w_gu': nrm(25, (L, D, 2 * D_FF), D ** -0.5),
        'w_down': nrm(26, (L, D_FF, D), D_FF ** -0.5),
    }


def reference(x, c, ctx, c_ctx, w_mod, b_mod, norm_g, w_in, w_out, sg_ln_g, sg_ln_b, sg_w, sg_b,
              attn_sink, rw_mu, rw_w0, rw_w2, rw_a0, rw_a2, rw_kk, rw_ka, rw_rk, rw_g2,
              rw_lnx_g, rw_lnx_b, w_gu, w_down):
    c_act = jax.nn.silu(c)
    cc_act = jax.nn.silu(c_ctx)
    xc = ctx
    for l in range(DEPTH):
        lp = {
            'w_mod': w_mod[l], 'b_mod': b_mod[l], 'norm_g': norm_g[l],
            'w_in': w_in[l], 'w_out': w_out[l],
            'sg_ln_g': sg_ln_g[l], 'sg_ln_b': sg_ln_b[l], 'sg_w': sg_w[l], 'sg_b': sg_b[l],
            'attn_sink': attn_sink[l], 'rw_mu': rw_mu[l],
            'rw_w0': rw_w0[l], 'rw_w2': rw_w2[l], 'rw_a0': rw_a0[l], 'rw_a2': rw_a2[l],
            'rw_kk': rw_kk[l], 'rw_ka': rw_ka[l], 'rw_rk': rw_rk[l], 'rw_g2': rw_g2[l],
            'rw_lnx_g': rw_lnx_g[l], 'rw_lnx_b': rw_lnx_b[l],
            'w_gu': w_gu[l], 'w_down': w_down[l],
        }
        x, xc = trunk_layer(x, xc, c_act, cc_act, lp, l == DEPTH - 1)
    return x
```

```python
import functools
import math

import jax
import jax.numpy as jnp
from jax import lax
from jax.experimental import pallas as pl
from jax.experimental.pallas import tpu as pltpu

F32 = jnp.float32
BF16 = jnp.bfloat16

D_MODEL = 1024
HEAD_DIM = 64
GRID_W = 64
A_GROUPS = 4
A_W = A_GROUPS * HEAD_DIM
CHUNK = 128
ATT_HEADS = 8
ATT_KV_HEADS = 2
ATT_Q = ATT_HEADS * HEAD_DIM
ATT_KV = ATT_KV_HEADS * HEAD_DIM
WINDOW = 128
QBLOCK = 128
ROPE_THETA = 10000.0
ROPE_FREQS = HEAD_DIM // 4
NEG_INF = -1e30
RW_HEADS = 4
RW_W = RW_HEADS * HEAD_DIM
W_LORA = 64
A_LORA = 64
G_LORA = 128
N_DIR = 2
GN_EPS = 64e-5
MIX_W = A_W + ATT_Q + RW_W
D_FF = -(-8 * D_MODEL // (3 * 256)) * 256
RMS_EPS = 1e-6
LN_EPS = 1e-5
OFF_Q = 2 * A_W
OFF_R = OFF_Q + ATT_Q
OFF_CTX = OFF_R + RW_W + G_LORA
OFF_RWK = OFF_CTX + 2 * ATT_KV
IN_COLS = OFF_RWK + 2 * RW_W + N_DIR * (W_LORA + A_LORA)
MU_HEAD = RW_W + G_LORA
RWK_W = 2 * RW_W + N_DIR * (W_LORA + A_LORA)
IN_GROUPS = ((0, OFF_Q), (OFF_Q, ATT_Q), (OFF_R, MU_HEAD), (OFF_CTX, 2 * ATT_KV), (OFF_RWK, RWK_W))

V7X_LANES = 128
V7X_SUBLANES = 8
V7X_VMEM_LIMIT_BYTES = 56 * 1024 * 1024

SCAN_CHUNK = 64
LOG_DECAY_SCALE = -math.exp(-0.5)


def _params(n_grid):
    return pltpu.CompilerParams(
        dimension_semantics=("arbitrary",) * n_grid, vmem_limit_bytes=V7X_VMEM_LIMIT_BYTES)


def _dot(a, b):
    return jnp.dot(a.astype(BF16), b.astype(BF16), preferred_element_type=F32)


def _dot_nt(a, b):
    return lax.dot_general(a.astype(BF16), b.astype(BF16), (((1,), (1,)), ((), ())),
                           preferred_element_type=F32)


def _dot_tn(a, b):
    return lax.dot_general(a.astype(BF16), b.astype(BF16), (((0,), (0,)), ((), ())),
                           preferred_element_type=F32)


def _split_bf16(x, n):
    parts = []
    for _ in range(n - 1):
        h = x.astype(BF16)
        parts.append(h)
        x = x - h.astype(F32)
    parts.append(x.astype(BF16))
    return parts


def _dot_exact_rhs(x, m_bf16, n=3):
    acc = None
    for p in _split_bf16(x, n):
        t = jnp.dot(p, m_bf16, preferred_element_type=F32)
        acc = t if acc is None else acc + t
    return acc


def _dot_exact_lhs(m_bf16, x, n=3):
    acc = None
    for p in _split_bf16(x, n):
        t = jnp.dot(m_bf16, p, preferred_element_type=F32)
        acc = t if acc is None else acc + t
    return acc


def _head_sum_matrix(width):
    i = lax.broadcasted_iota(jnp.int32, (width, width), 0) // HEAD_DIM
    j = lax.broadcasted_iota(jnp.int32, (width, width), 1) // HEAD_DIM
    return jnp.where(i == j, 1.0, 0.0).astype(BF16)


def _rms_norm(x, g):
    return x * lax.rsqrt(jnp.mean(x * x, axis=-1, keepdims=True) + RMS_EPS) * g


def _tile_lanes(x, width):
    reps = width // x.shape[-1]
    return x if reps == 1 else jnp.concatenate([x] * reps, axis=-1)


def _rope(x, cos, sin_signed):
    w = x.shape[-1]
    lane = lax.broadcasted_iota(jnp.int32, x.shape, 1)
    lo = (lane % (2 * ROPE_FREQS)) < ROPE_FREQS
    partner = jnp.where(lo, pltpu.roll(x, w - ROPE_FREQS, 1), pltpu.roll(x, ROPE_FREQS, 1))
    return x * _tile_lanes(cos, w) + partner * _tile_lanes(sin_signed, w)


def _mod_kernel(c_ref, w_ref, b_ref, o_ref):
    c = c_ref[...]
    act = c * jax.nn.sigmoid(c)
    o_ref[0] = _dot(act, w_ref[0]) + b_ref[0]


def _modulation(c_rows, w_mod, b_mod):
    depth, d, n = w_mod.shape
    rows = c_rows.shape[0]
    tn = n // 4
    return pl.pallas_call(
        _mod_kernel,
        grid=(depth, n // tn),
        in_specs=[
            pl.BlockSpec((rows, d), lambda l, j: (0, 0)),
            pl.BlockSpec((1, d, tn), lambda l, j: (l, 0, j)),
            pl.BlockSpec((1, 1, tn), lambda l, j: (l, 0, j)),
        ],
        out_specs=pl.BlockSpec((1, rows, tn), lambda l, j: (l, 0, j)),
        out_shape=jax.ShapeDtypeStruct((depth, rows, n), F32),
        compiler_params=_params(2),
        name="modulation",
    )(c_rows, w_mod, b_mod.reshape(depth, 1, n))


def _proj_in_kernel(rope, x_ref, g_ref, sh_ref, sc_ref, w_ref, cos_ref, sin_ref, *o_refs):
    h = _rms_norm(x_ref[0], g_ref[...]) * (1.0 + sc_ref[0]) + sh_ref[0]
    hb = h.astype(BF16)
    for (start, width), o_ref in zip(IN_GROUPS, o_refs):
        p = jnp.dot(hb, w_ref[:, start:start + width], preferred_element_type=F32)
        if rope and start == OFF_Q:
            p = _rope(p, cos_ref[...], sin_ref[...])
        if rope and start == OFF_CTX:
            p = jnp.concatenate(
                [_rope(p[:, :ATT_KV], cos_ref[...], sin_ref[...]), p[:, ATT_KV:]], axis=-1)
        o_ref[0] = p


def _proj_in(x, g, sh, sc, w_bf16, cos, sin_signed, rope, tm):
    b, t, d = x.shape
    row = lambda bi, i: (bi, i, 0)
    per_b = lambda bi, i: (bi, 0, 0)
    const = lambda bi, i: (0, 0)
    return pl.pallas_call(
        functools.partial(_proj_in_kernel, rope),
        grid=(b, t // tm),
        in_specs=[
            pl.BlockSpec((1, tm, d), row),
            pl.BlockSpec((1, d), const),
            pl.BlockSpec((1, 1, d), per_b),
            pl.BlockSpec((1, 1, d), per_b),
            pl.BlockSpec((d, IN_COLS), const),
            pl.BlockSpec((tm, V7X_LANES), lambda bi, i: (i, 0)),
            pl.BlockSpec((tm, V7X_LANES), lambda bi, i: (i, 0)),
        ],
        out_specs=[pl.BlockSpec((1, tm, width), row) for _, width in IN_GROUPS],
        out_shape=[jax.ShapeDtypeStruct((b, t, width), F32) for _, width in IN_GROUPS],
        compiler_params=_params(2),
        name="proj_in",
    )(x, g, sh, sc, w_bf16, cos, sin_signed)


def _gmlp_kernel(z_ref, g_ref, b_ref, w_ref, bias_ref, o_ref):
    z = jax.nn.gelu(z_ref[0], approximate=True)
    u, v = z[:, :A_W], z[:, A_W:]
    mu = jnp.mean(v, axis=-1, keepdims=True)
    var = jnp.mean(jnp.square(v - mu), axis=-1, keepdims=True)
    vn = ((v - mu) * lax.rsqrt(var + LN_EPS) * g_ref[...] + b_ref[...]).astype(BF16)
    bias = bias_ref[...]
    for c in range(z.shape[0] // CHUNK):
        rows = slice(c * CHUNK, (c + 1) * CHUNK)
        mixed = jnp.concatenate(
            [jnp.dot(w_ref[gi], vn[rows, gi * HEAD_DIM:(gi + 1) * HEAD_DIM],
                     preferred_element_type=F32) for gi in range(A_GROUPS)], axis=-1)
        o_ref[0, rows, :] = u[rows] * (mixed + bias)


def _gmlp(z, ln_g, ln_b, w_bf16, bias_plane, tm):
    b, t, _ = z.shape
    row = lambda bi, i: (bi, i, 0)
    const2 = lambda bi, i: (0, 0)
    return pl.pallas_call(
        _gmlp_kernel,
        grid=(b, t // tm),
        in_specs=[
            pl.BlockSpec((1, tm, 2 * A_W), row),
            pl.BlockSpec((1, A_W), const2),
            pl.BlockSpec((1, A_W), const2),
            pl.BlockSpec((A_GROUPS, CHUNK, CHUNK), lambda bi, i: (0, 0, 0)),
            pl.BlockSpec((CHUNK, A_W), const2),
        ],
        out_specs=pl.BlockSpec((1, tm, A_W), row),
        out_shape=jax.ShapeDtypeStruct((b, t, A_W), F32),
        compiler_params=_params(2),
        name="gmlp",
    )(z, ln_g, ln_b, w_bf16, bias_plane)


def _attn_kernel(local, nblk, sink_ref, q_ref, *refs):
    if local:
        kvp_ref, kvc_ref, kvn_ref, ctx_ref, o_ref = refs
    else:
        ctx_ref, o_ref = refs
    n = pl.program_id(1)
    group = ATT_HEADS // ATT_KV_HEADS
    scale = HEAD_DIM ** -0.5
    q = q_ref[0]
    ctx = ctx_ref[0]
    rows = group * QBLOCK
    row_head = lax.broadcasted_iota(jnp.int32, (rows, 1), 0) // QBLOCK
    if local:
        kv_loc = jnp.concatenate([kvp_ref[0], kvc_ref[0], kvn_ref[0]], axis=0)
        qi = lax.broadcasted_iota(jnp.int32, (rows, 3 * QBLOCK), 0) % QBLOCK
        kj = lax.broadcasted_iota(jnp.int32, (rows, 3 * QBLOCK), 1)
        blk = kj // QBLOCK
        valid = (jnp.abs(kj - QBLOCK - qi) <= WINDOW)
        valid = valid & ((blk != 0) | (n > 0)) & ((blk != 2) | (n < nblk - 1))
    for g in range(ATT_KV_HEADS):
        qs = jnp.concatenate(
            [q[:, (g * group + i) * HEAD_DIM:(g * group + i + 1) * HEAD_DIM] for i in range(group)],
            axis=0)
        sink = jnp.zeros((rows, 1), F32)
        for i in range(group):
            sink = jnp.where(row_head == i, sink_ref[g * group + i], sink)
        kc = ctx[:, g * HEAD_DIM:(g + 1) * HEAD_DIM]
        vc = ctx[:, ATT_KV + g * HEAD_DIM:ATT_KV + (g + 1) * HEAD_DIM]
        s_ctx = _dot_nt(qs, kc) * scale
        m = jnp.maximum(jnp.max(s_ctx, axis=-1, keepdims=True), sink)
        if local:
            kl = kv_loc[:, g * HEAD_DIM:(g + 1) * HEAD_DIM]
            vl = kv_loc[:, ATT_KV + g * HEAD_DIM:ATT_KV + (g + 1) * HEAD_DIM]
            s_loc = jnp.where(valid, _dot_nt(qs, kl) * scale, NEG_INF)
            m = jnp.maximum(m, jnp.max(s_loc, axis=-1, keepdims=True))
        p_ctx = jnp.exp(s_ctx - m)
        denom = jnp.sum(p_ctx, axis=-1, keepdims=True) + jnp.exp(sink - m)
        acc = _dot(p_ctx, vc)
        if local:
            p_loc = jnp.exp(s_loc - m)
            denom = denom + jnp.sum(p_loc, axis=-1, keepdims=True)
            acc = acc + _dot(p_loc, vl)
        out = acc / denom
        for i in range(group):
            h = g * group + i
            o_ref[0, :, h * HEAD_DIM:(h + 1) * HEAD_DIM] = out[i * QBLOCK:(i + 1) * QBLOCK]


def _attention(q, kv, kv_ctx, sink, local):
    b, t, _ = q.shape
    c = kv_ctx.shape[1]
    nblk = t // QBLOCK
    row = lambda bi, i: (bi, i, 0)
    in_specs = [
        pl.BlockSpec(memory_space=pltpu.SMEM),
        pl.BlockSpec((1, QBLOCK, ATT_Q), row),
    ]
    args = [sink, q]
    if local:
        in_specs += [
            pl.BlockSpec((1, QBLOCK, 2 * ATT_KV), lambda bi, i: (bi, jnp.maximum(i - 1, 0), 0)),
            pl.BlockSpec((1, QBLOCK, 2 * ATT_KV), row),
            pl.BlockSpec((1, QBLOCK, 2 * ATT_KV), lambda bi, i: (bi, jnp.minimum(i + 1, nblk - 1), 0)),
        ]
        args += [kv, kv, kv]
    in_specs.append(pl.BlockSpec((1, c, 2 * ATT_KV), lambda bi, i: (bi, 0, 0)))
    args.append(kv_ctx)
    return pl.pallas_call(
        functools.partial(_attn_kernel, local, nblk),
        grid=(b, nblk),
        in_specs=in_specs,
        out_specs=pl.BlockSpec((1, QBLOCK, ATT_Q), row),
        out_shape=jax.ShapeDtypeStruct((b, t, ATT_Q), F32),
        compiler_params=_params(2),
        name="attention",
    )(*args)


def _shift_rows(x, k):
    return pltpu.roll(x, k % x.shape[0], 0)


def _token_shift(cur, prev_rows, next_rows, grid_mode, first, last):
    tm, width = cur.shape
    ch = lax.broadcasted_iota(jnp.int32, (tm, width), 1)
    pos = lax.broadcasted_iota(jnp.int32, (tm, width), 0)
    if grid_mode:
        qw = width // 4
        col = pos % GRID_W
        left = jnp.where(col == 0, 0.0, _shift_rows(cur, 1))
        right = jnp.where(col == GRID_W - 1, 0.0, _shift_rows(cur, -1))
        up = jnp.concatenate([prev_rows, cur[:tm - GRID_W]], axis=0)
        up = jnp.where((pos < GRID_W) & first, 0.0, up)
        down = jnp.concatenate([cur[GRID_W:], next_rows], axis=0)
        down = jnp.where((pos >= tm - GRID_W) & last, 0.0, down)
        return jnp.where(ch < qw, left, jnp.where(ch < 2 * qw, right, jnp.where(ch < 3 * qw, up, down)))
    prev = jnp.where(pos == 0, 0.0, _shift_rows(cur, 1))
    nxt = jnp.where(pos == tm - 1, 0.0, _shift_rows(cur, -1))
    return jnp.where(ch < width // 2, prev, nxt)


def _rwkv_prep_kernel(grid_mode, nblk, *refs):
    if grid_mode:
        (rg_ref, rgp_ref, rgn_ref, rwk_ref, rwkp_ref, rwkn_ref) = refs[:6]
        rest = refs[6:]
    else:
        rg_ref, rwk_ref = refs[:2]
        rest = refs[2:]
    (mu_rg_ref, mu_rwk_ref, kkp_ref, ka_ref, w0_ref, w2_ref, a0_ref, a2_ref,
     r_ref, gd_ref, v_ref, kk_ref, lw0_ref, lw1_ref, b0_ref, b1_ref, kd0_ref, kd1_ref) = rest
    i = pl.program_id(1)
    first, last = i == 0, i == nblk - 1
    rg, rwk = rg_ref[0], rwk_ref[0]
    if grid_mode:
        rg_s = _token_shift(rg, rgp_ref[0], rgn_ref[0], True, first, last)
        rwk_s = _token_shift(rwk, rwkp_ref[0], rwkn_ref[0], True, first, last)
    else:
        rg_s = _token_shift(rg, None, None, False, first, last)
        rwk_s = _token_shift(rwk, None, None, False, first, last)
    rg = rg + (rg_s - rg) * mu_rg_ref[...]
    rwk = rwk + (rwk_s - rwk) * mu_rwk_ref[...]
    r_ref[0] = rg[:, :RW_W]
    gd_ref[0] = rg[:, RW_W:]
    k = rwk[:, :RW_W]
    v_ref[0] = rwk[:, RW_W:2 * RW_W]
    kk = k * kkp_ref[...]
    norm = jnp.sqrt(_dot_exact_rhs(kk * kk, _head_sum_matrix(RW_W)))
    kk = kk / jnp.maximum(norm, 1e-12)
    kk_ref[0] = kk
    base = 2 * RW_W
    for d, (lw_ref, b_ref, kd_ref) in enumerate(((lw0_ref, b0_ref, kd0_ref), (lw1_ref, b1_ref, kd1_ref))):
        wd = rwk[:, base + d * W_LORA:base + (d + 1) * W_LORA]
        ad = rwk[:, base + N_DIR * W_LORA + d * A_LORA:base + N_DIR * W_LORA + (d + 1) * A_LORA]
        z = w0_ref[d] + _dot(jnp.tanh(wd), w2_ref[d])
        lw_ref[0] = LOG_DECAY_SCALE * jax.nn.sigmoid(z)
        a = jax.nn.sigmoid(a0_ref[d] + _dot(ad, a2_ref[d]))
        b_ref[0] = kk * a
        kd_ref[0] = k * (1.0 + (a - 1.0) * ka_ref[...])


def _rwkv_prep(rg, rwk, mu, kkp, ka, w0, w2_bf16, a0, a2_bf16, grid_mode, tm):
    b, t, _ = rg.shape
    nblk = t // tm
    row = lambda bi, i: (bi, i, 0)
    const2 = lambda bi, i: (0, 0)
    const3 = lambda bi, i: (0, 0, 0)
    in_specs, args = [], []
    for arr, width in ((rg, MU_HEAD), (rwk, RWK_W)):
        in_specs.append(pl.BlockSpec((1, tm, width), row))
        args.append(arr)
        if grid_mode:
            per = tm // GRID_W
            nrow = t // GRID_W
            in_specs.append(pl.BlockSpec(
                (1, GRID_W, width), lambda bi, i: (bi, jnp.maximum(i * per - 1, 0), 0)))
            in_specs.append(pl.BlockSpec(
                (1, GRID_W, width), lambda bi, i: (bi, jnp.minimum((i + 1) * per, nrow - 1), 0)))
            args += [arr, arr]
    in_specs += [
        pl.BlockSpec((1, MU_HEAD), const2),
        pl.BlockSpec((1, RWK_W), const2),
        pl.BlockSpec((1, RW_W), const2),
        pl.BlockSpec((1, RW_W), const2),
        pl.BlockSpec((N_DIR, 1, RW_W), const3),
        pl.BlockSpec((N_DIR, W_LORA, RW_W), const3),
        pl.BlockSpec((N_DIR, 1, RW_W), const3),
        pl.BlockSpec((N_DIR, A_LORA, RW_W), const3),
    ]
    args += [mu[:, :MU_HEAD], mu[:, MU_HEAD:], kkp, ka, w0, w2_bf16, a0, a2_bf16]
    widths = [RW_W, G_LORA] + [RW_W] * 8
    return pl.pallas_call(
        functools.partial(_rwkv_prep_kernel, grid_mode, nblk),
        grid=(b, nblk),
        in_specs=in_specs,
        out_specs=[pl.BlockSpec((1, tm, w), row) for w in widths],
        out_shape=[jax.ShapeDtypeStruct((b, t, w), F32) for w in widths],
        compiler_params=_params(2),
        name="rwkv_prep",
    )(*args)


def _scan_direction(reverse, lw, b, kd, kk, v, r, st_ref, d, y_ref):
    L = SCAN_CHUNK
    ii = lax.broadcasted_iota(jnp.int32, (L, L), 0)
    jj = lax.broadcasted_iota(jnp.int32, (L, L), 1)
    strict = (ii < jj) if reverse else (ii > jj)
    incl = (ii <= jj) if reverse else (ii >= jj)
    eye = ii == jj
    cum = jnp.where(incl, 1.0, 0.0).astype(BF16)
    lw_cum = _dot_exact_lhs(cum, lw)
    lw_tot = lw_cum[0:1] if reverse else lw_cum[L - 1:L]
    e_neg = jnp.exp(-lw_cum)
    e_end = jnp.exp(lw_tot - lw_cum)
    alpha = kk * jnp.exp(lw_cum - lw)
    beta = b * e_neg
    kappa = kd * e_neg
    rho = r * jnp.exp(lw_cum)
    beta_end = b * e_end
    kappa_end = kd * e_end
    w_tot = jnp.exp(lw_tot)
    for h in range(RW_HEADS):
        hs = slice(h * HEAD_DIM, (h + 1) * HEAD_DIM)
        al, be, ka, rh, vm = alpha[:, hs], beta[:, hs], kappa[:, hs], rho[:, hs], v[:, hs]
        p1 = _dot_nt(jnp.concatenate([al, rh], axis=0), jnp.concatenate([be, ka], axis=0))
        nn = jnp.where(strict, p1[:L, :L], 0.0)
        ak = jnp.where(strict, p1[:L, L:], 0.0)
        rb = jnp.where(incl, p1[L:, :L], 0.0)
        rk = jnp.where(incl, p1[L:, L:], 0.0)
        tinv = jnp.where(eye, 1.0, 0.0) - nn
        npow = nn
        for _ in range(int(math.log2(L)) - 1):
            npow = _dot(npow, npow)
            tinv = tinv + _dot(tinv, npow)
        akv = _dot(ak, vm)
        z = _dot(tinv, jnp.concatenate([al, -akv], axis=1))
        big_t = _dot_tn(beta_end[:, hs], z)
        big_r = _dot(rb, z)
        m_mat = jnp.where(eye, w_tot[:, hs], 0.0) - big_t[:, :HEAD_DIM]
        c_mat = big_t[:, HEAD_DIM:] + _dot_tn(kappa_end[:, hs], vm)
        q_mat = rh - big_r[:, :HEAD_DIM]
        y0 = big_r[:, HEAD_DIM:] + _dot(rk, vm)
        st = st_ref[d, h]
        y_ref[0, :, hs] = _dot(q_mat, st) + y0
        st_ref[d, h] = _dot(m_mat, st) + c_mat


def _scan_kernel(nchunk, lw0_ref, b0_ref, kd0_ref, kkf_ref, vf_ref, rf_ref,
                 lw1_ref, b1_ref, kd1_ref, kkb_ref, vb_ref, rb_ref, s0_ref,
                 yf_ref, yb_ref, sT_ref, st_ref):
    c = pl.program_id(1)

    @pl.when(c == 0)
    def _():
        st_ref[...] = s0_ref[0]

    _scan_direction(False, lw0_ref[0], b0_ref[0], kd0_ref[0], kkf_ref[0], vf_ref[0], rf_ref[0],
                    st_ref, 0, yf_ref)
    _scan_direction(True, lw1_ref[0], b1_ref[0], kd1_ref[0], kkb_ref[0], vb_ref[0], rb_ref[0],
                    st_ref, 1, yb_ref)

    @pl.when(c == nchunk - 1)
    def _():
        sT_ref[0] = st_ref[...]


def _scan(lw0, lw1, b0, b1, kd0, kd1, kk, v, r, s0):
    bsz, t, _ = kk.shape
    nchunk = t // SCAN_CHUNK
    fwd = lambda bi, c: (bi, c, 0)
    bwd = lambda bi, c: (bi, nchunk - 1 - c, 0)
    blk = (1, SCAN_CHUNK, RW_W)
    st_blk = (1, N_DIR, RW_HEADS, HEAD_DIM, HEAD_DIM)
    st_map = lambda bi, c: (bi, 0, 0, 0, 0)
    return pl.pallas_call(
        functools.partial(_scan_kernel, nchunk),
        grid=(bsz, nchunk),
        in_specs=[pl.BlockSpec(blk, fwd)] * 6 + [pl.BlockSpec(blk, bwd)] * 6
        + [pl.BlockSpec(st_blk, st_map)],
        out_specs=[pl.BlockSpec(blk, fwd), pl.BlockSpec(blk, bwd), pl.BlockSpec(st_blk, st_map)],
        out_shape=[jax.ShapeDtypeStruct((bsz, t, RW_W), F32),
                   jax.ShapeDtypeStruct((bsz, t, RW_W), F32),
                   jax.ShapeDtypeStruct(s0.shape, F32)],
        scratch_shapes=[pltpu.VMEM((N_DIR, RW_HEADS, HEAD_DIM, HEAD_DIM), F32)],
        compiler_params=_params(2),
        name="rwkv_scan",
    )(lw0, b0, kd0, kk, v, r, lw1, b1, kd1, kk, v, r, s0)


FFN_SPLIT = 2


def _out_kernel(x_ref, sg_ref, att_ref, yf_ref, yb_ref, r_ref, kd0_ref, kd1_ref, v_ref, gd_ref,
                gt1_ref, sh2_ref, sc2_ref, gt2_ref, ng_ref, lng_ref, lnb_ref, rk_ref, g2_ref,
                wo_ref, wgu_ref, wd_ref, o_ref):
    hsum = _head_sum_matrix(RW_W)
    y = yf_ref[0] + yb_ref[0]
    mu = _dot_exact_rhs(y, hsum) * (1.0 / HEAD_DIM)
    yc = y - mu
    var = _dot_exact_rhs(yc * yc, hsum) * (1.0 / HEAD_DIM)
    yn = yc * lax.rsqrt(var + GN_EPS) * lng_ref[...] + lnb_ref[...]
    bonus = _dot_exact_rhs(r_ref[0] * (kd0_ref[0] + kd1_ref[0]) * rk_ref[...], hsum) * v_ref[0]
    gate = _dot(jax.nn.sigmoid(gd_ref[0]), g2_ref[...])
    o_rw = (yn + bonus) * gate
    y_mix = (_dot(sg_ref[0], wo_ref[:A_W]) + _dot(att_ref[0], wo_ref[A_W:A_W + ATT_Q])
             + _dot(o_rw, wo_ref[A_W + ATT_Q:]))
    x1 = x_ref[0] + gt1_ref[0] * _rms_norm(y_mix, ng_ref[1:2])
    h2 = (_rms_norm(x1, ng_ref[2:3]) * (1.0 + sc2_ref[0]) + sh2_ref[0]).astype(BF16)
    fc = D_FF // FFN_SPLIT
    acc = None
    for j in range(FFN_SPLIT):
        gg = jnp.dot(h2, wgu_ref[:, j * fc:(j + 1) * fc], preferred_element_type=F32)
        uu = jnp.dot(h2, wgu_ref[:, D_FF + j * fc:D_FF + (j + 1) * fc], preferred_element_type=F32)
        part = _dot(gg * jax.nn.sigmoid(gg) * uu, wd_ref[j * fc:(j + 1) * fc])
        acc = part if acc is None else acc + part
    o_ref[0] = x1 + gt2_ref[0] * _rms_norm(acc, ng_ref[3:4])


def _mix_out_ffn(x, o_sg, o_att, yf, yb, r, kd0, kd1, v, gd, gt1, sh2, sc2, gt2, norm_g,
                 lnx_g, lnx_b, rk, g2_bf16, wo_bf16, wgu_bf16, wd_bf16, tm):
    b, t, d = x.shape
    row = lambda bi, i: (bi, i, 0)
    per_b = lambda bi, i: (bi, 0, 0)
    const2 = lambda bi, i: (0, 0)
    single = pl.Buffered(1)
    in_specs = [
        pl.BlockSpec((1, tm, d), row),
        pl.BlockSpec((1, tm, A_W), row),
        pl.BlockSpec((1, tm, ATT_Q), row),
    ] + [pl.BlockSpec((1, tm, RW_W), row)] * 6 + [
        pl.BlockSpec((1, tm, G_LORA), row),
    ] + [pl.BlockSpec((1, 1, d), per_b)] * 4 + [
        pl.BlockSpec((4, d), const2),
        pl.BlockSpec((1, RW_W), const2),
        pl.BlockSpec((1, RW_W), const2),
        pl.BlockSpec((1, RW_W), const2),
        pl.BlockSpec((G_LORA, RW_W), const2),
        pl.BlockSpec((MIX_W, d), const2, pipeline_mode=single),
        pl.BlockSpec((d, 2 * D_FF), const2, pipeline_mode=single),
        pl.BlockSpec((D_FF, d), const2, pipeline_mode=single),
    ]
    return pl.pallas_call(
        _out_kernel,
        grid=(b, t // tm),
        in_specs=in_specs,
        out_specs=pl.BlockSpec((1, tm, d), row),
        out_shape=jax.ShapeDtypeStruct((b, t, d), F32),
        compiler_params=_params(2),
        name="mix_out_ffn",
    )(x, o_sg, o_att, yf, yb, r, kd0, kd1, v, gd, gt1, sh2, sc2, gt2, norm_g,
      lnx_g, lnx_b, rk, g2_bf16, wo_bf16, wgu_bf16, wd_bf16)


def _rope_tables(t):
    pos = jnp.arange(t)
    inv = ROPE_THETA ** (-jnp.arange(ROPE_FREQS, dtype=F32) / ROPE_FREQS)
    ang_r = (pos // GRID_W).astype(F32)[:, None] * inv
    ang_c = (pos % GRID_W).astype(F32)[:, None] * inv
    cos = jnp.concatenate([jnp.cos(ang_r)] * 2 + [jnp.cos(ang_c)] * 2, axis=-1)
    sin = jnp.concatenate([-jnp.sin(ang_r), jnp.sin(ang_r), -jnp.sin(ang_c), jnp.sin(ang_c)], axis=-1)
    return jnp.concatenate([cos, cos], axis=-1), jnp.concatenate([sin, sin], axis=-1)


def _row_tile(t, target):
    tm = min(t, target)
    while t % tm:
        tm //= 2
    return tm


def kernel(x, c, ctx, c_ctx, w_mod, b_mod, norm_g, w_in, w_out, sg_ln_g, sg_ln_b, sg_w, sg_b,
           attn_sink, rw_mu, rw_w0, rw_w2, rw_a0, rw_a2, rw_kk, rw_ka, rw_rk, rw_g2,
           rw_lnx_g, rw_lnx_b, w_gu, w_down):
    bsz, t, d = x.shape
    clen = ctx.shape[1]
    depth = w_mod.shape[0]

    rows = -(-(bsz + 1) // V7X_SUBLANES) * V7X_SUBLANES
    c_rows = jnp.zeros((rows, d), F32).at[:bsz].set(c).at[bsz].set(c_ctx)
    mods = _modulation(c_rows, w_mod, b_mod)

    cos, sin = _rope_tables(t)
    cos_c, sin_c = cos[:clen], sin[:clen]
    s_zero = jnp.zeros((bsz, N_DIR, RW_HEADS, HEAD_DIM, HEAD_DIM), F32)

    tm_proj = _row_tile(t, 512)
    tm_gmlp = _row_tile(t, 512)
    tm_prep = _row_tile(t, 512)
    tm_out = _row_tile(t, 256)
    tc_proj = _row_tile(clen, 512)
    tc_out = _row_tile(clen, 256)

    xc = ctx
    for l in range(depth):
        last = l == depth - 1
        lat = [mods[l, :bsz, i * d:(i + 1) * d][:, None, :] for i in range(6)]
        cm = [jnp.broadcast_to(mods[l, bsz, i * d:(i + 1) * d][None, None, :], (bsz, 1, d))
              for i in range(6)]
        g = norm_g[l]
        w_in_b = w_in[l].astype(BF16)
        w2_b, a2_b = rw_w2[l].astype(BF16), rw_a2[l].astype(BF16)
        prep_params = (rw_mu[l][None], rw_kk[l][None], rw_ka[l][None], rw_w0[l][:, None, :], w2_b,
                       rw_a0[l][:, None, :], a2_b)

        c_a, c_q, c_rg, c_kv, c_rwk = _proj_in(xc, g[0:1], cm[0], cm[1], w_in_b, cos_c, sin_c, False, tc_proj)
        (c_r, c_gd, c_v, c_kk, c_lw0, c_lw1, c_b0, c_b1, c_kd0, c_kd1) = _rwkv_prep(
            c_rg, c_rwk, *prep_params, False, clen)
        c_yf, c_yb, s_ctx = _scan(c_lw0, c_lw1, c_b0, c_b1, c_kd0, c_kd1, c_kk, c_v, c_r, s_zero)

        p_a, p_q, p_rg, p_kv, p_rwk = _proj_in(x, g[0:1], lat[0], lat[1], w_in_b, cos, sin, True, tm_proj)
        (r, gd, v, kk, lw0, lw1, b0, b1, kd0, kd1) = _rwkv_prep(p_rg, p_rwk, *prep_params, True, tm_prep)
        yf, yb, _ = _scan(lw0, lw1, b0, b1, kd0, kd1, kk, v, r, s_ctx)

        sg_w_b = sg_w[l].astype(BF16)
        bias_plane = jnp.repeat(sg_b[l].T, HEAD_DIM, axis=1)
        o_sg = _gmlp(p_a, sg_ln_g[l][None], sg_ln_b[l][None], sg_w_b, bias_plane, tm_gmlp)
        o_att = _attention(p_q, p_kv, c_kv, attn_sink[l], True)

        out_params = (g, rw_lnx_g[l][None], rw_lnx_b[l][None], rw_rk[l].reshape(1, RW_W),
                      rw_g2[l].astype(BF16), w_out[l].astype(BF16), w_gu[l].astype(BF16),
                      w_down[l].astype(BF16))
        x_new = _mix_out_ffn(x, o_sg, o_att, yf, yb, r, kd0, kd1, v, gd,
                             lat[2], lat[3], lat[4], lat[5], *out_params, tm_out)
        if not last:
            oc_sg = _gmlp(c_a, sg_ln_g[l][None], sg_ln_b[l][None], sg_w_b, bias_plane, _row_tile(clen, 512))
            oc_att = _attention(c_q, c_kv, c_kv, attn_sink[l], False)
            xc = _mix_out_ffn(xc, oc_sg, oc_att, c_yf, c_yb, c_r, c_kd0, c_kd1, c_v, c_gd,
                              cm[2], cm[3], cm[4], cm[5], *out_params, tc_out)
        x = x_new
    return x
```

```python
import functools
import math

import jax
import jax.numpy as jnp
from jax import lax
from jax.experimental import pallas as pl
from jax.experimental.pallas import tpu as pltpu

F32 = jnp.float32
BF16 = jnp.bfloat16

D_MODEL = 1024
HEAD_DIM = 64
GRID_W = 64
A_GROUPS = 4
A_W = A_GROUPS * HEAD_DIM
CHUNK = 128
ATT_HEADS = 8
ATT_KV_HEADS = 2
ATT_Q = ATT_HEADS * HEAD_DIM
ATT_KV = ATT_KV_HEADS * HEAD_DIM
WINDOW = 128
QBLOCK = 128
ROPE_THETA = 10000.0
ROPE_FREQS = HEAD_DIM // 4
NEG_INF = -1e30
RW_HEADS = 4
RW_W = RW_HEADS * HEAD_DIM
W_LORA = 64
A_LORA = 64
G_LORA = 128
N_DIR = 2
GN_EPS = 64e-5
MIX_W = A_W + ATT_Q + RW_W
D_FF = -(-8 * D_MODEL // (3 * 256)) * 256
RMS_EPS = 1e-6
LN_EPS = 1e-5
OFF_Q = 2 * A_W
OFF_R = OFF_Q + ATT_Q
OFF_CTX = OFF_R + RW_W + G_LORA
OFF_RWK = OFF_CTX + 2 * ATT_KV
IN_COLS = OFF_RWK + 2 * RW_W + N_DIR * (W_LORA + A_LORA)
MU_HEAD = RW_W + G_LORA
RWK_W = 2 * RW_W + N_DIR * (W_LORA + A_LORA)
IN_GROUPS = ((0, OFF_Q), (OFF_Q, ATT_Q), (OFF_R, MU_HEAD), (OFF_CTX, 2 * ATT_KV), (OFF_RWK, RWK_W))

V7X_LANES = 128
V7X_SUBLANES = 8
V7X_VMEM_LIMIT_BYTES = 56 * 1024 * 1024

SCAN_CHUNK = 64
SCAN_BATCH = 4
LOG_DECAY_SCALE = -math.exp(-0.5)


def _params(n_grid):
    return pltpu.CompilerParams(
        dimension_semantics=("arbitrary",) * n_grid, vmem_limit_bytes=V7X_VMEM_LIMIT_BYTES)


def _dot(a, b):
    return jnp.dot(a.astype(BF16), b.astype(BF16), preferred_element_type=F32)


def _dot_nt(a, b):
    return lax.dot_general(a.astype(BF16), b.astype(BF16), (((1,), (1,)), ((), ())),
                           preferred_element_type=F32)


def _dot_tn(a, b):
    return lax.dot_general(a.astype(BF16), b.astype(BF16), (((0,), (0,)), ((), ())),
                           preferred_element_type=F32)


def _split_bf16(x, n):
    parts = []
    for _ in range(n - 1):
        h = x.astype(BF16)
        parts.append(h)
        x = x - h.astype(F32)
    parts.append(x.astype(BF16))
    return parts


def _dot_exact_rhs(x, m_bf16, n=3):
    acc = None
    for p in _split_bf16(x, n):
        t = jnp.dot(p, m_bf16, preferred_element_type=F32)
        acc = t if acc is None else acc + t
    return acc


def _dot_exact_lhs(m_bf16, x, n=3):
    acc = None
    for p in _split_bf16(x, n):
        t = jnp.dot(m_bf16, p, preferred_element_type=F32)
        acc = t if acc is None else acc + t
    return acc


def _head_sum_matrix(width):
    i = lax.broadcasted_iota(jnp.int32, (width, width), 0) // HEAD_DIM
    j = lax.broadcasted_iota(jnp.int32, (width, width), 1) // HEAD_DIM
    return jnp.where(i == j, 1.0, 0.0).astype(BF16)


def _rms_norm(x, g):
    return x * lax.rsqrt(jnp.mean(x * x, axis=-1, keepdims=True) + RMS_EPS) * g


def _tile_lanes(x, width):
    reps = width // x.shape[-1]
    return x if reps == 1 else jnp.concatenate([x] * reps, axis=-1)


def _rope(x, cos, sin_signed):
    w = x.shape[-1]
    lane = lax.broadcasted_iota(jnp.int32, x.shape, 1)
    lo = (lane % (2 * ROPE_FREQS)) < ROPE_FREQS
    partner = jnp.where(lo, pltpu.roll(x, w - ROPE_FREQS, 1), pltpu.roll(x, ROPE_FREQS, 1))
    return x * _tile_lanes(cos, w) + partner * _tile_lanes(sin_signed, w)


def _mod_kernel(c_ref, w_ref, b_ref, o_ref):
    c = c_ref[...]
    act = c * jax.nn.sigmoid(c)
    o_ref[0] = _dot(act, w_ref[0]) + b_ref[0]


def _modulation(c_rows, w_mod, b_mod):
    depth, d, n = w_mod.shape
    rows = c_rows.shape[0]
    tn = n // 4
    return pl.pallas_call(
        _mod_kernel,
        grid=(depth, n // tn),
        in_specs=[
            pl.BlockSpec((rows, d), lambda l, j: (0, 0)),
            pl.BlockSpec((1, d, tn), lambda l, j: (l, 0, j)),
            pl.BlockSpec((1, 1, tn), lambda l, j: (l, 0, j)),
        ],
        out_specs=pl.BlockSpec((1, rows, tn), lambda l, j: (l, 0, j)),
        out_shape=jax.ShapeDtypeStruct((depth, rows, n), F32),
        compiler_params=_params(2),
        name="modulation",
    )(c_rows, w_mod, b_mod.reshape(depth, 1, n))


def _proj_in_kernel(rope, x_ref, g_ref, sh_ref, sc_ref, w_ref, cos_ref, sin_ref, *o_refs):
    h = _rms_norm(x_ref[0], g_ref[...]) * (1.0 + sc_ref[0]) + sh_ref[0]
    hb = h.astype(BF16)
    for (start, width), o_ref in zip(IN_GROUPS, o_refs):
        p = jnp.dot(hb, w_ref[:, start:start + width], preferred_element_type=F32)
        if rope and start == OFF_Q:
            p = _rope(p, cos_ref[...], sin_ref[...])
        if rope and start == OFF_CTX:
            p = jnp.concatenate(
                [_rope(p[:, :ATT_KV], cos_ref[...], sin_ref[...]), p[:, ATT_KV:]], axis=-1)
        o_ref[0] = p


def _proj_in(x, g, sh, sc, w_bf16, cos, sin_signed, rope, tm):
    b, t, d = x.shape
    row = lambda bi, i: (bi, i, 0)
    per_b = lambda bi, i: (bi, 0, 0)
    const = lambda bi, i: (0, 0)
    return pl.pallas_call(
        functools.partial(_proj_in_kernel, rope),
        grid=(b, t // tm),
        in_specs=[
            pl.BlockSpec((1, tm, d), row),
            pl.BlockSpec((1, d), const),
            pl.BlockSpec((1, 1, d), per_b),
            pl.BlockSpec((1, 1, d), per_b),
            pl.BlockSpec((d, IN_COLS), const),
            pl.BlockSpec((tm, V7X_LANES), lambda bi, i: (i, 0)),
            pl.BlockSpec((tm, V7X_LANES), lambda bi, i: (i, 0)),
        ],
        out_specs=[pl.BlockSpec((1, tm, width), row) for _, width in IN_GROUPS],
        out_shape=[jax.ShapeDtypeStruct((b, t, width), F32) for _, width in IN_GROUPS],
        compiler_params=_params(2),
        name="proj_in",
    )(x, g, sh, sc, w_bf16, cos, sin_signed)


def _gmlp_kernel(z_ref, g_ref, b_ref, w_ref, bias_ref, o_ref):
    z = jax.nn.gelu(z_ref[0], approximate=True)
    u, v = z[:, :A_W], z[:, A_W:]
    mu = jnp.mean(v, axis=-1, keepdims=True)
    var = jnp.mean(jnp.square(v - mu), axis=-1, keepdims=True)
    vn = ((v - mu) * lax.rsqrt(var + LN_EPS) * g_ref[...] + b_ref[...]).astype(BF16)
    bias = bias_ref[...]
    for c in range(z.shape[0] // CHUNK):
        rows = slice(c * CHUNK, (c + 1) * CHUNK)
        mixed = jnp.concatenate(
            [jnp.dot(w_ref[gi], vn[rows, gi * HEAD_DIM:(gi + 1) * HEAD_DIM],
                     preferred_element_type=F32) for gi in range(A_GROUPS)], axis=-1)
        o_ref[0, rows, :] = u[rows] * (mixed + bias)


def _gmlp(z, ln_g, ln_b, w_bf16, bias_plane, tm):
    b, t, _ = z.shape
    row = lambda bi, i: (bi, i, 0)
    const2 = lambda bi, i: (0, 0)
    return pl.pallas_call(
        _gmlp_kernel,
        grid=(b, t // tm),
        in_specs=[
            pl.BlockSpec((1, tm, 2 * A_W), row),
            pl.BlockSpec((1, A_W), const2),
            pl.BlockSpec((1, A_W), const2),
            pl.BlockSpec((A_GROUPS, CHUNK, CHUNK), lambda bi, i: (0, 0, 0)),
            pl.BlockSpec((CHUNK, A_W), const2),
        ],
        out_specs=pl.BlockSpec((1, tm, A_W), row),
        out_shape=jax.ShapeDtypeStruct((b, t, A_W), F32),
        compiler_params=_params(2),
        name="gmlp",
    )(z, ln_g, ln_b, w_bf16, bias_plane)


def _attn_kernel(local, nblk, sink_ref, q_ref, *refs):
    if local:
        kvp_ref, kvc_ref, kvn_ref, ctx_ref, o_ref = refs
    else:
        ctx_ref, o_ref = refs
    n = pl.program_id(1)
    group = ATT_HEADS // ATT_KV_HEADS
    scale = HEAD_DIM ** -0.5
    q = q_ref[0]
    ctx = ctx_ref[0]
    rows = group * QBLOCK
    row_head = lax.broadcasted_iota(jnp.int32, (rows, 1), 0) // QBLOCK
    if local:
        kv_loc = jnp.concatenate([kvp_ref[0], kvc_ref[0], kvn_ref[0]], axis=0)
        qi = lax.broadcasted_iota(jnp.int32, (rows, 3 * QBLOCK), 0) % QBLOCK
        kj = lax.broadcasted_iota(jnp.int32, (rows, 3 * QBLOCK), 1)
        blk = kj // QBLOCK
        valid = (jnp.abs(kj - QBLOCK - qi) <= WINDOW)
        valid = valid & ((blk != 0) | (n > 0)) & ((blk != 2) | (n < nblk - 1))
    for g in range(ATT_KV_HEADS):
        qs = jnp.concatenate(
            [q[:, (g * group + i) * HEAD_DIM:(g * group + i + 1) * HEAD_DIM] for i in range(group)],
            axis=0)
        sink = jnp.zeros((rows, 1), F32)
        for i in range(group):
            sink = jnp.where(row_head == i, sink_ref[g * group + i], sink)
        kc = ctx[:, g * HEAD_DIM:(g + 1) * HEAD_DIM]
        vc = ctx[:, ATT_KV + g * HEAD_DIM:ATT_KV + (g + 1) * HEAD_DIM]
        s_ctx = _dot_nt(qs, kc) * scale
        m = jnp.maximum(jnp.max(s_ctx, axis=-1, keepdims=True), sink)
        if local:
            kl = kv_loc[:, g * HEAD_DIM:(g + 1) * HEAD_DIM]
            vl = kv_loc[:, ATT_KV + g * HEAD_DIM:ATT_KV + (g + 1) * HEAD_DIM]
            s_loc = jnp.where(valid, _dot_nt(qs, kl) * scale, NEG_INF)
            m = jnp.maximum(m, jnp.max(s_loc, axis=-1, keepdims=True))
        p_ctx = jnp.exp(s_ctx - m)
        denom = jnp.sum(p_ctx, axis=-1, keepdims=True) + jnp.exp(sink - m)
        acc = _dot(p_ctx, vc)
        if local:
            p_loc = jnp.exp(s_loc - m)
            denom = denom + jnp.sum(p_loc, axis=-1, keepdims=True)
            acc = acc + _dot(p_loc, vl)
        out = acc / denom
        for i in range(group):
            h = g * group + i
            o_ref[0, :, h * HEAD_DIM:(h + 1) * HEAD_DIM] = out[i * QBLOCK:(i + 1) * QBLOCK]


def _attention(q, kv, kv_ctx, sink, local):
    b, t, _ = q.shape
    c = kv_ctx.shape[1]
    nblk = t // QBLOCK
    row = lambda bi, i: (bi, i, 0)
    in_specs = [
        pl.BlockSpec(memory_space=pltpu.SMEM),
        pl.BlockSpec((1, QBLOCK, ATT_Q), row),
    ]
    args = [sink, q]
    if local:
        in_specs += [
            pl.BlockSpec((1, QBLOCK, 2 * ATT_KV), lambda bi, i: (bi, jnp.maximum(i - 1, 0), 0)),
            pl.BlockSpec((1, QBLOCK, 2 * ATT_KV), row),
            pl.BlockSpec((1, QBLOCK, 2 * ATT_KV), lambda bi, i: (bi, jnp.minimum(i + 1, nblk - 1), 0)),
        ]
        args += [kv, kv, kv]
    in_specs.append(pl.BlockSpec((1, c, 2 * ATT_KV), lambda bi, i: (bi, 0, 0)))
    args.append(kv_ctx)
    return pl.pallas_call(
        functools.partial(_attn_kernel, local, nblk),
        grid=(b, nblk),
        in_specs=in_specs,
        out_specs=pl.BlockSpec((1, QBLOCK, ATT_Q), row),
        out_shape=jax.ShapeDtypeStruct((b, t, ATT_Q), F32),
        compiler_params=_params(2),
        name="attention",
    )(*args)


def _shift_rows(x, k):
    return pltpu.roll(x, k % x.shape[0], 0)


def _token_shift(cur, prev_rows, next_rows, grid_mode, first, last):
    tm, width = cur.shape
    ch = lax.broadcasted_iota(jnp.int32, (tm, width), 1)
    pos = lax.broadcasted_iota(jnp.int32, (tm, width), 0)
    if grid_mode:
        qw = width // 4
        col = pos % GRID_W
        left = jnp.where(col == 0, 0.0, _shift_rows(cur, 1))
        right = jnp.where(col == GRID_W - 1, 0.0, _shift_rows(cur, -1))
        up = jnp.concatenate([prev_rows, cur[:tm - GRID_W]], axis=0)
        up = jnp.where((pos < GRID_W) & first, 0.0, up)
        down = jnp.concatenate([cur[GRID_W:], next_rows], axis=0)
        down = jnp.where((pos >= tm - GRID_W) & last, 0.0, down)
        return jnp.where(ch < qw, left, jnp.where(ch < 2 * qw, right, jnp.where(ch < 3 * qw, up, down)))
    prev = jnp.where(pos == 0, 0.0, _shift_rows(cur, 1))
    nxt = jnp.where(pos == tm - 1, 0.0, _shift_rows(cur, -1))
    return jnp.where(ch < width // 2, prev, nxt)


def _rwkv_prep_kernel(grid_mode, nblk, *refs):
    if grid_mode:
        (rg_ref, rgp_ref, rgn_ref, rwk_ref, rwkp_ref, rwkn_ref) = refs[:6]
        rest = refs[6:]
    else:
        rg_ref, rwk_ref = refs[:2]
        rest = refs[2:]
    (mu_rg_ref, mu_rwk_ref, kkp_ref, ka_ref, w0_ref, w2_ref, a0_ref, a2_ref,
     r_ref, gd_ref, v_ref, kk_ref, lw0_ref, lw1_ref, b0_ref, b1_ref, kd0_ref, kd1_ref) = rest
    i = pl.program_id(1)
    first, last = i == 0, i == nblk - 1
    rg, rwk = rg_ref[0], rwk_ref[0]
    if grid_mode:
        rg_s = _token_shift(rg, rgp_ref[0], rgn_ref[0], True, first, last)
        rwk_s = _token_shift(rwk, rwkp_ref[0], rwkn_ref[0], True, first, last)
    else:
        rg_s = _token_shift(rg, None, None, False, first, last)
        rwk_s = _token_shift(rwk, None, None, False, first, last)
    rg = rg + (rg_s - rg) * mu_rg_ref[...]
    rwk = rwk + (rwk_s - rwk) * mu_rwk_ref[...]
    r_ref[0] = rg[:, :RW_W]
    gd_ref[0] = rg[:, RW_W:]
    k = rwk[:, :RW_W]
    v_ref[0] = rwk[:, RW_W:2 * RW_W]
    kk = k * kkp_ref[...]
    norm = jnp.sqrt(_dot_exact_rhs(kk * kk, _head_sum_matrix(RW_W)))
    kk = kk / jnp.maximum(norm, 1e-12)
    kk_ref[0] = kk
    base = 2 * RW_W
    for d, (lw_ref, b_ref, kd_ref) in enumerate(((lw0_ref, b0_ref, kd0_ref), (lw1_ref, b1_ref, kd1_ref))):
        wd = rwk[:, base + d * W_LORA:base + (d + 1) * W_LORA]
        ad = rwk[:, base + N_DIR * W_LORA + d * A_LORA:base + N_DIR * W_LORA + (d + 1) * A_LORA]
        z = w0_ref[d] + _dot(jnp.tanh(wd), w2_ref[d])
        lw_ref[0] = LOG_DECAY_SCALE * jax.nn.sigmoid(z)
        a = jax.nn.sigmoid(a0_ref[d] + _dot(ad, a2_ref[d]))
        b_ref[0] = kk * a
        kd_ref[0] = k * (1.0 + (a - 1.0) * ka_ref[...])


def _rwkv_prep(rg, rwk, mu, kkp, ka, w0, w2_bf16, a0, a2_bf16, grid_mode, tm):
    b, t, _ = rg.shape
    nblk = t // tm
    row = lambda bi, i: (bi, i, 0)
    const2 = lambda bi, i: (0, 0)
    const3 = lambda bi, i: (0, 0, 0)
    in_specs, args = [], []
    for arr, width in ((rg, MU_HEAD), (rwk, RWK_W)):
        in_specs.append(pl.BlockSpec((1, tm, width), row))
        args.append(arr)
        if grid_mode:
            per = tm // GRID_W
            nrow = t // GRID_W
            in_specs.append(pl.BlockSpec(
                (1, GRID_W, width), lambda bi, i: (bi, jnp.maximum(i * per - 1, 0), 0)))
            in_specs.append(pl.BlockSpec(
                (1, GRID_W, width), lambda bi, i: (bi, jnp.minimum((i + 1) * per, nrow - 1), 0)))
            args += [arr, arr]
    in_specs += [
        pl.BlockSpec((1, MU_HEAD), const2),
        pl.BlockSpec((1, RWK_W), const2),
        pl.BlockSpec((1, RW_W), const2),
        pl.BlockSpec((1, RW_W), const2),
        pl.BlockSpec((N_DIR, 1, RW_W), const3),
        pl.BlockSpec((N_DIR, W_LORA, RW_W), const3),
        pl.BlockSpec((N_DIR, 1, RW_W), const3),
        pl.BlockSpec((N_DIR, A_LORA, RW_W), const3),
    ]
    args += [mu[:, :MU_HEAD], mu[:, MU_HEAD:], kkp, ka, w0, w2_bf16, a0, a2_bf16]
    widths = [RW_W, G_LORA] + [RW_W] * 8
    return pl.pallas_call(
        functools.partial(_rwkv_prep_kernel, grid_mode, nblk),
        grid=(b, nblk),
        in_specs=in_specs,
        out_specs=[pl.BlockSpec((1, tm, w), row) for w in widths],
        out_shape=[jax.ShapeDtypeStruct((b, t, w), F32) for w in widths],
        compiler_params=_params(2),
        name="rwkv_prep",
    )(*args)


def _scan_direction(reverse, lw, b, kd, kk, v, r, st_ref, y_ref):
    L = SCAN_CHUNK
    ii = lax.broadcasted_iota(jnp.int32, (L, L), 0)
    jj = lax.broadcasted_iota(jnp.int32, (L, L), 1)
    cum = jnp.where((ii <= jj) if reverse else (ii >= jj), 1.0, 0.0).astype(BF16)
    lw_cum = _dot_exact_lhs(cum, lw)
    yield
    lw_tot = lw_cum[0:1] if reverse else lw_cum[L - 1:L]
    e_neg = jnp.exp(-lw_cum)
    e_end = jnp.exp(lw_tot - lw_cum)

    row = lax.broadcasted_iota(jnp.int32, (RW_W, RW_W), 0)
    col = lax.broadcasted_iota(jnp.int32, (RW_W, RW_W), 1)
    blk = (row // HEAD_DIM) == (col // HEAD_DIM)
    strict = blk & ((row < col) if reverse else (row > col))
    incl = blk & ((row <= col) if reverse else (row >= col))
    eye = row == col

    def diag_rows(x):
        return jnp.where(blk, jnp.concatenate([x] * RW_HEADS, axis=0), 0.0)

    def diag_cols(x):
        return jnp.where(blk, jnp.concatenate([x.T] * RW_HEADS, axis=1), 0.0)

    alpha = diag_rows(kk * jnp.exp(lw_cum - lw)).astype(BF16)
    beta = diag_rows(b * e_neg).astype(BF16)
    kappa = diag_rows(kd * e_neg).astype(BF16)
    rho = diag_rows(r * jnp.exp(lw_cum))
    vals = diag_rows(v).astype(BF16)
    beta_end_t = diag_cols(b * e_end).astype(BF16)
    kappa_end_t = diag_cols(kd * e_end).astype(BF16)

    p1 = _dot_nt(jnp.concatenate([alpha, rho.astype(BF16)], axis=0),
                 jnp.concatenate([beta, kappa], axis=0))
    yield
    nn = jnp.where(strict, p1[:RW_W, :RW_W], 0.0)
    ak = jnp.where(strict, p1[:RW_W, RW_W:], 0.0)
    rb = jnp.where(incl, p1[RW_W:, :RW_W], 0.0).astype(BF16)
    rk = jnp.where(incl, p1[RW_W:, RW_W:], 0.0)
    tinv = jnp.where(eye, 1.0, 0.0) - nn
    npow = _dot(nn, nn)
    akv = _dot(ak, vals)
    yield
    for _ in range(int(math.log2(L)) - 2):
        both = _dot(jnp.concatenate([npow, tinv], axis=0), npow)
        npow, tinv = both[:RW_W], tinv + both[RW_W:]
        yield
    tinv = tinv + _dot(tinv, npow)
    yield
    z = _dot(tinv, jnp.concatenate([alpha, (-akv).astype(BF16)], axis=1)).astype(BF16)
    yield
    big_t = jnp.dot(beta_end_t, z, preferred_element_type=F32)
    big_r = jnp.dot(rb, z, preferred_element_type=F32)
    m_mat = jnp.where(eye, jnp.exp(lw_tot), 0.0) - big_t[:, :RW_W]
    c_mat = big_t[:, RW_W:] + jnp.dot(kappa_end_t, vals, preferred_element_type=F32)
    q_mat = rho - big_r[:, :RW_W]
    y0 = big_r[:, RW_W:] + _dot(rk, vals)
    yield
    st = st_ref[...].astype(BF16)
    y_blocks = _dot(q_mat, st) + y0
    y = y_blocks[:L]
    for h in range(1, RW_HEADS):
        y = y + y_blocks[h * L:(h + 1) * L]
    y_ref[...] = y
    st_ref[...] = _dot(m_mat, st) + c_mat


def _scan_kernel(nchunk, nbatch, lw0_ref, b0_ref, kd0_ref, kkf_ref, vf_ref, rf_ref,
                 lw1_ref, b1_ref, kd1_ref, kkb_ref, vb_ref, rb_ref, s0_ref,
                 yf_ref, yb_ref, sT_ref, st_ref):
    c = pl.program_id(1)

    @pl.when(c == 0)
    def _():
        st_ref[...] = s0_ref[...]

    chains = []
    for bi in range(nbatch):
        chains.append(_scan_direction(False, lw0_ref[bi], b0_ref[bi], kd0_ref[bi], kkf_ref[bi], vf_ref[bi],
                                      rf_ref[bi], st_ref.at[bi, 0], yf_ref.at[bi]))
        chains.append(_scan_direction(True, lw1_ref[bi], b1_ref[bi], kd1_ref[bi], kkb_ref[bi], vb_ref[bi],
                                      rb_ref[bi], st_ref.at[bi, 1], yb_ref.at[bi]))
    while chains:
        chains = [g for g in chains if next(g, True) is None]

    @pl.when(c == nchunk - 1)
    def _():
        sT_ref[...] = st_ref[...]


def _scan(lw0, lw1, b0, b1, kd0, kd1, kk, v, r, s0, nbatch):
    assert SCAN_CHUNK == HEAD_DIM
    bsz, t, _ = kk.shape
    nchunk = t // SCAN_CHUNK
    fwd = lambda bi, c: (bi, c, 0)
    bwd = lambda bi, c: (bi, nchunk - 1 - c, 0)
    blk = (nbatch, SCAN_CHUNK, RW_W)
    st_blk = (nbatch, N_DIR, RW_W, RW_W)
    st_map = lambda bi, c: (bi, 0, 0, 0)
    return pl.pallas_call(
        functools.partial(_scan_kernel, nchunk, nbatch),
        grid=(bsz // nbatch, nchunk),
        in_specs=[pl.BlockSpec(blk, fwd)] * 6 + [pl.BlockSpec(blk, bwd)] * 6
        + [pl.BlockSpec(st_blk, st_map)],
        out_specs=[pl.BlockSpec(blk, fwd), pl.BlockSpec(blk, bwd), pl.BlockSpec(st_blk, st_map)],
        out_shape=[jax.ShapeDtypeStruct((bsz, t, RW_W), F32),
                   jax.ShapeDtypeStruct((bsz, t, RW_W), F32),
                   jax.ShapeDtypeStruct(s0.shape, F32)],
        scratch_shapes=[pltpu.VMEM(st_blk, F32)],
        compiler_params=_params(2),
        name="rwkv_scan",
    )(lw0, b0, kd0, kk, v, r, lw1, b1, kd1, kk, v, r, s0)


FFN_SPLIT = 2


def _out_kernel(x_ref, sg_ref, att_ref, yf_ref, yb_ref, r_ref, kd0_ref, kd1_ref, v_ref, gd_ref,
                gt1_ref, sh2_ref, sc2_ref, gt2_ref, ng_ref, lng_ref, lnb_ref, rk_ref, g2_ref,
                wo_ref, wgu_ref, wd_ref, o_ref):
    hsum = _head_sum_matrix(RW_W)
    y = yf_ref[0] + yb_ref[0]
    mu = _dot_exact_rhs(y, hsum) * (1.0 / HEAD_DIM)
    yc = y - mu
    var = _dot_exact_rhs(yc * yc, hsum) * (1.0 / HEAD_DIM)
    yn = yc * lax.rsqrt(var + GN_EPS) * lng_ref[...] + lnb_ref[...]
    bonus = _dot_exact_rhs(r_ref[0] * (kd0_ref[0] + kd1_ref[0]) * rk_ref[...], hsum) * v_ref[0]
    gate = _dot(jax.nn.sigmoid(gd_ref[0]), g2_ref[...])
    o_rw = (yn + bonus) * gate
    y_mix = (_dot(sg_ref[0], wo_ref[:A_W]) + _dot(att_ref[0], wo_ref[A_W:A_W + ATT_Q])
             + _dot(o_rw, wo_ref[A_W + ATT_Q:]))
    x1 = x_ref[0] + gt1_ref[0] * _rms_norm(y_mix, ng_ref[1:2])
    h2 = (_rms_norm(x1, ng_ref[2:3]) * (1.0 + sc2_ref[0]) + sh2_ref[0]).astype(BF16)
    fc = D_FF // FFN_SPLIT
    acc = None
    for j in range(FFN_SPLIT):
        gg = jnp.dot(h2, wgu_ref[:, j * fc:(j + 1) * fc], preferred_element_type=F32)
        uu = jnp.dot(h2, wgu_ref[:, D_FF + j * fc:D_FF + (j + 1) * fc], preferred_element_type=F32)
        part = _dot(gg * jax.nn.sigmoid(gg) * uu, wd_ref[j * fc:(j + 1) * fc])
        acc = part if acc is None else acc + part
    o_ref[0] = x1 + gt2_ref[0] * _rms_norm(acc, ng_ref[3:4])


def _mix_out_ffn(x, o_sg, o_att, yf, yb, r, kd0, kd1, v, gd, gt1, sh2, sc2, gt2, norm_g,
                 lnx_g, lnx_b, rk, g2_bf16, wo_bf16, wgu_bf16, wd_bf16, tm):
    b, t, d = x.shape
    row = lambda bi, i: (bi, i, 0)
    per_b = lambda bi, i: (bi, 0, 0)
    const2 = lambda bi, i: (0, 0)
    single = pl.Buffered(1)
    in_specs = [
        pl.BlockSpec((1, tm, d), row),
        pl.BlockSpec((1, tm, A_W), row),
        pl.BlockSpec((1, tm, ATT_Q), row),
    ] + [pl.BlockSpec((1, tm, RW_W), row)] * 6 + [
        pl.BlockSpec((1, tm, G_LORA), row),
    ] + [pl.BlockSpec((1, 1, d), per_b)] * 4 + [
        pl.BlockSpec((4, d), const2),
        pl.BlockSpec((1, RW_W), const2),
        pl.BlockSpec((1, RW_W), const2),
        pl.BlockSpec((1, RW_W), const2),
        pl.BlockSpec((G_LORA, RW_W), const2),
        pl.BlockSpec((MIX_W, d), const2, pipeline_mode=single),
        pl.BlockSpec((d, 2 * D_FF), const2, pipeline_mode=single),
        pl.BlockSpec((D_FF, d), const2, pipeline_mode=single),
    ]
    return pl.pallas_call(
        _out_kernel,
        grid=(b, t // tm),
        in_specs=in_specs,
        out_specs=pl.BlockSpec((1, tm, d), row),
        out_shape=jax.ShapeDtypeStruct((b, t, d), F32),
        compiler_params=_params(2),
        name="mix_out_ffn",
    )(x, o_sg, o_att, yf, yb, r, kd0, kd1, v, gd, gt1, sh2, sc2, gt2, norm_g,
      lnx_g, lnx_b, rk, g2_bf16, wo_bf16, wgu_bf16, wd_bf16)


def _rope_tables(t):
    pos = jnp.arange(t)
    inv = ROPE_THETA ** (-jnp.arange(ROPE_FREQS, dtype=F32) / ROPE_FREQS)
    ang_r = (pos // GRID_W).astype(F32)[:, None] * inv
    ang_c = (pos % GRID_W).astype(F32)[:, None] * inv
    cos = jnp.concatenate([jnp.cos(ang_r)] * 2 + [jnp.cos(ang_c)] * 2, axis=-1)
    sin = jnp.concatenate([-jnp.sin(ang_r), jnp.sin(ang_r), -jnp.sin(ang_c), jnp.sin(ang_c)], axis=-1)
    return jnp.concatenate([cos, cos], axis=-1), jnp.concatenate([sin, sin], axis=-1)


def _row_tile(t, target):
    tm = min(t, target)
    while t % tm:
        tm //= 2
    return tm


def kernel(x, c, ctx, c_ctx, w_mod, b_mod, norm_g, w_in, w_out, sg_ln_g, sg_ln_b, sg_w, sg_b,
           attn_sink, rw_mu, rw_w0, rw_w2, rw_a0, rw_a2, rw_kk, rw_ka, rw_rk, rw_g2,
           rw_lnx_g, rw_lnx_b, w_gu, w_down):
    bsz, t, d = x.shape
    clen = ctx.shape[1]
    depth = w_mod.shape[0]

    rows = -(-(bsz + 1) // V7X_SUBLANES) * V7X_SUBLANES
    c_rows = jnp.zeros((rows, d), F32).at[:bsz].set(c).at[bsz].set(c_ctx)
    mods = _modulation(c_rows, w_mod, b_mod)

    cos, sin = _rope_tables(t)
    cos_c, sin_c = cos[:clen], sin[:clen]
    s_zero = jnp.zeros((bsz, N_DIR, RW_W, RW_W), F32)

    tm_proj = _row_tile(t, 512)
    tm_gmlp = _row_tile(t, 512)
    tm_prep = _row_tile(t, 512)
    tm_out = _row_tile(t, 256)
    tc_proj = _row_tile(clen, 512)
    tc_out = _row_tile(clen, 256)
    nb_scan = SCAN_BATCH if bsz % SCAN_BATCH == 0 else 1

    xc = ctx
    for l in range(depth):
        last = l == depth - 1
        lat = [mods[l, :bsz, i * d:(i + 1) * d][:, None, :] for i in range(6)]
        cm = [jnp.broadcast_to(mods[l, bsz, i * d:(i + 1) * d][None, None, :], (bsz, 1, d))
              for i in range(6)]
        g = norm_g[l]
        w_in_b = w_in[l].astype(BF16)
        w2_b, a2_b = rw_w2[l].astype(BF16), rw_a2[l].astype(BF16)
        prep_params = (rw_mu[l][None], rw_kk[l][None], rw_ka[l][None], rw_w0[l][:, None, :], w2_b,
                       rw_a0[l][:, None, :], a2_b)

        c_a, c_q, c_rg, c_kv, c_rwk = _proj_in(xc, g[0:1], cm[0], cm[1], w_in_b, cos_c, sin_c, False, tc_proj)
        (c_r, c_gd, c_v, c_kk, c_lw0, c_lw1, c_b0, c_b1, c_kd0, c_kd1) = _rwkv_prep(
            c_rg, c_rwk, *prep_params, False, clen)
        c_yf, c_yb, s_ctx = _scan(c_lw0, c_lw1, c_b0, c_b1, c_kd0, c_kd1, c_kk, c_v, c_r, s_zero, nb_scan)

        p_a, p_q, p_rg, p_kv, p_rwk = _proj_in(x, g[0:1], lat[0], lat[1], w_in_b, cos, sin, True, tm_proj)
        (r, gd, v, kk, lw0, lw1, b0, b1, kd0, kd1) = _rwkv_prep(p_rg, p_rwk, *prep_params, True, tm_prep)
        yf, yb, _ = _scan(lw0, lw1, b0, b1, kd0, kd1, kk, v, r, s_ctx, nb_scan)

        sg_w_b = sg_w[l].astype(BF16)
        bias_plane = jnp.repeat(sg_b[l].T, HEAD_DIM, axis=1)
        o_sg = _gmlp(p_a, sg_ln_g[l][None], sg_ln_b[l][None], sg_w_b, bias_plane, tm_gmlp)
        o_att = _attention(p_q, p_kv, c_kv, attn_sink[l], True)

        out_params = (g, rw_lnx_g[l][None], rw_lnx_b[l][None], rw_rk[l].reshape(1, RW_W),
                      rw_g2[l].astype(BF16), w_out[l].astype(BF16), w_gu[l].astype(BF16),
                      w_down[l].astype(BF16))
        x_new = _mix_out_ffn(x, o_sg, o_att, yf, yb, r, kd0, kd1, v, gd,
                             lat[2], lat[3], lat[4], lat[5], *out_params, tm_out)
        if not last:
            oc_sg = _gmlp(c_a, sg_ln_g[l][None], sg_ln_b[l][None], sg_w_b, bias_plane, _row_tile(clen, 512))
            oc_att = _attention(c_q, c_kv, c_kv, attn_sink[l], False)
            xc = _mix_out_ffn(xc, oc_sg, oc_att, c_yf, c_yb, c_r, c_kd0, c_kd1, c_v, c_gd,
                              cm[2], cm[3], cm[4], cm[5], *out_params, tc_out)
        x = x_new
    return x
```

```python
import functools
import math

import jax
import jax.numpy as jnp
from jax import lax
from jax.experimental import pallas as pl
from jax.experimental.pallas import tpu as pltpu

F32 = jnp.float32
BF16 = jnp.bfloat16

D_MODEL = 1024
HEAD_DIM = 64
GRID_W = 64
A_GROUPS = 4
A_W = A_GROUPS * HEAD_DIM
CHUNK = 128
ATT_HEADS = 8
ATT_KV_HEADS = 2
ATT_Q = ATT_HEADS * HEAD_DIM
ATT_KV = ATT_KV_HEADS * HEAD_DIM
WINDOW = 128
QBLOCK = 128
ROPE_THETA = 10000.0
ROPE_FREQS = HEAD_DIM // 4
NEG_INF = -1e30
ATT_SCALE = HEAD_DIM ** -0.5
RW_HEADS = 4
RW_W = RW_HEADS * HEAD_DIM
W_LORA = 64
A_LORA = 64
G_LORA = 128
N_DIR = 2
GN_EPS = 64e-5
MIX_W = A_W + ATT_Q + RW_W
D_FF = -(-8 * D_MODEL // (3 * 256)) * 256
RMS_EPS = 1e-6
LN_EPS = 1e-5
OFF_Q = 2 * A_W
OFF_R = OFF_Q + ATT_Q
OFF_CTX = OFF_R + RW_W + G_LORA
OFF_RWK = OFF_CTX + 2 * ATT_KV
IN_COLS = OFF_RWK + 2 * RW_W + N_DIR * (W_LORA + A_LORA)
MU_HEAD = RW_W + G_LORA
RWK_W = 2 * RW_W + N_DIR * (W_LORA + A_LORA)
IN_GROUPS = ((0, OFF_Q), (OFF_Q, ATT_Q), (OFF_R, MU_HEAD), (OFF_CTX, 2 * ATT_KV), (OFF_RWK, RWK_W))

V7X_LANES = 128
V7X_SUBLANES = 8
V7X_VMEM_LIMIT_BYTES = 56 * 1024 * 1024

SCAN_CHUNK = 64
SCAN_BATCH = 4
LOG_DECAY_SCALE = -math.exp(-0.5)


def _params(n_grid):
    return pltpu.CompilerParams(
        dimension_semantics=("arbitrary",) * n_grid, vmem_limit_bytes=V7X_VMEM_LIMIT_BYTES)


def _dot(a, b):
    return jnp.dot(a.astype(BF16), b.astype(BF16), preferred_element_type=F32)


def _dot_nt(a, b):
    return lax.dot_general(a.astype(BF16), b.astype(BF16), (((1,), (1,)), ((), ())),
                           preferred_element_type=F32)


def _dot_tn(a, b):
    return lax.dot_general(a.astype(BF16), b.astype(BF16), (((0,), (0,)), ((), ())),
                           preferred_element_type=F32)


def _split_bf16(x, n):
    parts = []
    for _ in range(n - 1):
        h = x.astype(BF16)
        parts.append(h)
        x = x - h.astype(F32)
    parts.append(x.astype(BF16))
    return parts


def _dot_exact_rhs(x, m_bf16, n=3):
    acc = None
    for p in _split_bf16(x, n):
        t = jnp.dot(p, m_bf16, preferred_element_type=F32)
        acc = t if acc is None else acc + t
    return acc


def _dot_exact_lhs(m_bf16, x, n=3):
    acc = None
    for p in _split_bf16(x, n):
        t = jnp.dot(m_bf16, p, preferred_element_type=F32)
        acc = t if acc is None else acc + t
    return acc


def _head_sum_matrix(width):
    i = lax.broadcasted_iota(jnp.int32, (width, width), 0) // HEAD_DIM
    j = lax.broadcasted_iota(jnp.int32, (width, width), 1) // HEAD_DIM
    return jnp.where(i == j, 1.0, 0.0).astype(BF16)


def _rms_norm(x, g):
    return x * lax.rsqrt(jnp.mean(x * x, axis=-1, keepdims=True) + RMS_EPS) * g


def _tile_lanes(x, width):
    reps = width // x.shape[-1]
    return x if reps == 1 else jnp.concatenate([x] * reps, axis=-1)


def _rope(x, cos, sin_signed):
    w = x.shape[-1]
    lane = lax.broadcasted_iota(jnp.int32, x.shape, 1)
    lo = (lane % (2 * ROPE_FREQS)) < ROPE_FREQS
    partner = jnp.where(lo, pltpu.roll(x, w - ROPE_FREQS, 1), pltpu.roll(x, ROPE_FREQS, 1))
    return x * _tile_lanes(cos, w) + partner * _tile_lanes(sin_signed, w)


def _mod_kernel(c_ref, w_ref, b_ref, o_ref):
    c = c_ref[...]
    act = c * jax.nn.sigmoid(c)
    o_ref[0] = _dot(act, w_ref[0]) + b_ref[0]


def _modulation(c_rows, w_mod, b_mod):
    depth, d, n = w_mod.shape
    rows = c_rows.shape[0]
    tn = n // 4
    return pl.pallas_call(
        _mod_kernel,
        grid=(depth, n // tn),
        in_specs=[
            pl.BlockSpec((rows, d), lambda l, j: (0, 0)),
            pl.BlockSpec((1, d, tn), lambda l, j: (l, 0, j)),
            pl.BlockSpec((1, 1, tn), lambda l, j: (l, 0, j)),
        ],
        out_specs=pl.BlockSpec((1, rows, tn), lambda l, j: (l, 0, j)),
        out_shape=jax.ShapeDtypeStruct((depth, rows, n), F32),
        compiler_params=_params(2),
        name="modulation",
    )(c_rows, w_mod, b_mod.reshape(depth, 1, n))


def _proj_in_kernel(rope, x_ref, g_ref, sh_ref, sc_ref, w_ref, cos_ref, sin_ref, *o_refs):
    h = _rms_norm(x_ref[0], g_ref[...]) * (1.0 + sc_ref[0]) + sh_ref[0]
    hb = h.astype(BF16)
    for (start, width), o_ref in zip(IN_GROUPS, o_refs):
        p = jnp.dot(hb, w_ref[:, start:start + width], preferred_element_type=F32)
        if rope and start == OFF_Q:
            p = _rope(p, cos_ref[...], sin_ref[...])
        if start == OFF_Q:
            p = p * ATT_SCALE
        if rope and start == OFF_CTX:
            p = jnp.concatenate(
                [_rope(p[:, :ATT_KV], cos_ref[...], sin_ref[...]), p[:, ATT_KV:]], axis=-1)
        o_ref[0] = p


def _proj_in(x, g, sh, sc, w_bf16, cos, sin_signed, rope, tm):
    b, t, d = x.shape
    row = lambda bi, i: (bi, i, 0)
    per_b = lambda bi, i: (bi, 0, 0)
    const = lambda bi, i: (0, 0)
    return pl.pallas_call(
        functools.partial(_proj_in_kernel, rope),
        grid=(b, t // tm),
        in_specs=[
            pl.BlockSpec((1, tm, d), row),
            pl.BlockSpec((1, d), const),
            pl.BlockSpec((1, 1, d), per_b),
            pl.BlockSpec((1, 1, d), per_b),
            pl.BlockSpec((d, IN_COLS), const),
            pl.BlockSpec((tm, V7X_LANES), lambda bi, i: (i, 0)),
            pl.BlockSpec((tm, V7X_LANES), lambda bi, i: (i, 0)),
        ],
        out_specs=[pl.BlockSpec((1, tm, width), row) for _, width in IN_GROUPS],
        out_shape=[jax.ShapeDtypeStruct((b, t, width), F32) for _, width in IN_GROUPS],
        compiler_params=_params(2),
        name="proj_in",
    )(x, g, sh, sc, w_bf16, cos, sin_signed)


def _gmlp_kernel(z_ref, g_ref, b_ref, w_ref, bias_ref, o_ref):
    z = jax.nn.gelu(z_ref[0], approximate=True)
    u, v = z[:, :A_W], z[:, A_W:]
    mu = jnp.mean(v, axis=-1, keepdims=True)
    var = jnp.mean(jnp.square(v - mu), axis=-1, keepdims=True)
    vn = ((v - mu) * lax.rsqrt(var + LN_EPS) * g_ref[...] + b_ref[...]).astype(BF16)
    bias = bias_ref[...]
    for c in range(z.shape[0] // CHUNK):
        rows = slice(c * CHUNK, (c + 1) * CHUNK)
        mixed = jnp.concatenate(
            [jnp.dot(w_ref[gi], vn[rows, gi * HEAD_DIM:(gi + 1) * HEAD_DIM],
                     preferred_element_type=F32) for gi in range(A_GROUPS)], axis=-1)
        o_ref[0, rows, :] = u[rows] * (mixed + bias)


def _gmlp(z, ln_g, ln_b, w_bf16, bias_plane, tm):
    b, t, _ = z.shape
    row = lambda bi, i: (bi, i, 0)
    const2 = lambda bi, i: (0, 0)
    return pl.pallas_call(
        _gmlp_kernel,
        grid=(b, t // tm),
        in_specs=[
            pl.BlockSpec((1, tm, 2 * A_W), row),
            pl.BlockSpec((1, A_W), const2),
            pl.BlockSpec((1, A_W), const2),
            pl.BlockSpec((A_GROUPS, CHUNK, CHUNK), lambda bi, i: (0, 0, 0)),
            pl.BlockSpec((CHUNK, A_W), const2),
        ],
        out_specs=pl.BlockSpec((1, tm, A_W), row),
        out_shape=jax.ShapeDtypeStruct((b, t, A_W), F32),
        compiler_params=_params(2),
        name="gmlp",
    )(z, ln_g, ln_b, w_bf16, bias_plane)


def _attn_kernel(local, nblk, sink_ref, q_ref, *refs):
    if local:
        kvp_ref, kvc_ref, kvn_ref, ctx_ref, o_ref = refs
    else:
        ctx_ref, o_ref = refs
    n = pl.program_id(1)
    group = ATT_HEADS // ATT_KV_HEADS
    q = q_ref[0]
    ctx = ctx_ref[0]
    rows = group * QBLOCK
    row_head = lax.broadcasted_iota(jnp.int32, (rows, 1), 0) // QBLOCK
    if local:
        kv_loc = jnp.concatenate([kvp_ref[0], kvc_ref[0], kvn_ref[0]], axis=0)
        qi = lax.broadcasted_iota(jnp.int32, (rows, 3 * QBLOCK), 0) % QBLOCK
        kj = lax.broadcasted_iota(jnp.int32, (rows, 3 * QBLOCK), 1)
        blk = kj // QBLOCK
        valid = (jnp.abs(kj - QBLOCK - qi) <= WINDOW)
        valid = valid & ((blk != 0) | (n > 0)) & ((blk != 2) | (n < nblk - 1))
    s_ctx, s_loc, sinks = [], [], []
    for g in range(ATT_KV_HEADS):
        qs = jnp.concatenate(
            [q[:, (g * group + i) * HEAD_DIM:(g * group + i + 1) * HEAD_DIM] for i in range(group)],
            axis=0)
        sink = jnp.zeros((rows, 1), F32)
        for i in range(group):
            sink = jnp.where(row_head == i, sink_ref[g * group + i], sink)
        sinks.append(sink)
        s_ctx.append(_dot_nt(qs, ctx[:, g * HEAD_DIM:(g + 1) * HEAD_DIM]))
        if local:
            s_loc.append(jnp.where(valid, _dot_nt(qs, kv_loc[:, g * HEAD_DIM:(g + 1) * HEAD_DIM]), NEG_INF))
    p_ctx, p_loc, denoms = [], [], []
    for g in range(ATT_KV_HEADS):
        m = jnp.maximum(jnp.max(s_ctx[g], axis=-1, keepdims=True), sinks[g])
        if local:
            m = jnp.maximum(m, jnp.max(s_loc[g], axis=-1, keepdims=True))
        p = jnp.exp(s_ctx[g] - m)
        denom = jnp.sum(p, axis=-1, keepdims=True) + jnp.exp(sinks[g] - m)
        p_ctx.append(p.astype(BF16))
        if local:
            p = jnp.exp(s_loc[g] - m)
            denom = denom + jnp.sum(p, axis=-1, keepdims=True)
            p_loc.append(p.astype(BF16))
        denoms.append(denom)
    for g in range(ATT_KV_HEADS):
        acc = _dot(p_ctx[g], ctx[:, ATT_KV + g * HEAD_DIM:ATT_KV + (g + 1) * HEAD_DIM])
        if local:
            acc = acc + _dot(p_loc[g], kv_loc[:, ATT_KV + g * HEAD_DIM:ATT_KV + (g + 1) * HEAD_DIM])
        out = acc / denoms[g]
        for i in range(group):
            h = g * group + i
            o_ref[0, :, h * HEAD_DIM:(h + 1) * HEAD_DIM] = out[i * QBLOCK:(i + 1) * QBLOCK]


def _attention(q, kv, kv_ctx, sink, local):
    b, t, _ = q.shape
    c = kv_ctx.shape[1]
    nblk = t // QBLOCK
    row = lambda bi, i: (bi, i, 0)
    in_specs = [
        pl.BlockSpec(memory_space=pltpu.SMEM),
        pl.BlockSpec((1, QBLOCK, ATT_Q), row),
    ]
    args = [sink, q]
    if local:
        in_specs += [
            pl.BlockSpec((1, QBLOCK, 2 * ATT_KV), lambda bi, i: (bi, jnp.maximum(i - 1, 0), 0)),
            pl.BlockSpec((1, QBLOCK, 2 * ATT_KV), row),
            pl.BlockSpec((1, QBLOCK, 2 * ATT_KV), lambda bi, i: (bi, jnp.minimum(i + 1, nblk - 1), 0)),
        ]
        args += [kv, kv, kv]
    in_specs.append(pl.BlockSpec((1, c, 2 * ATT_KV), lambda bi, i: (bi, 0, 0)))
    args.append(kv_ctx)
    return pl.pallas_call(
        functools.partial(_attn_kernel, local, nblk),
        grid=(b, nblk),
        in_specs=in_specs,
        out_specs=pl.BlockSpec((1, QBLOCK, ATT_Q), row),
        out_shape=jax.ShapeDtypeStruct((b, t, ATT_Q), F32),
        compiler_params=_params(2),
        name="attention",
    )(*args)


def _shift_rows(x, k):
    return pltpu.roll(x, k % x.shape[0], 0)


def _token_shift(cur, prev_rows, next_rows, grid_mode, first, last):
    tm, width = cur.shape
    ch = lax.broadcasted_iota(jnp.int32, (tm, width), 1)
    pos = lax.broadcasted_iota(jnp.int32, (tm, width), 0)
    if grid_mode:
        qw = width // 4
        col = pos % GRID_W
        left = jnp.where(col == 0, 0.0, _shift_rows(cur, 1))
        right = jnp.where(col == GRID_W - 1, 0.0, _shift_rows(cur, -1))
        up = jnp.concatenate([prev_rows, cur[:tm - GRID_W]], axis=0)
        up = jnp.where((pos < GRID_W) & first, 0.0, up)
        down = jnp.concatenate([cur[GRID_W:], next_rows], axis=0)
        down = jnp.where((pos >= tm - GRID_W) & last, 0.0, down)
        return jnp.where(ch < qw, left, jnp.where(ch < 2 * qw, right, jnp.where(ch < 3 * qw, up, down)))
    prev = jnp.where(pos == 0, 0.0, _shift_rows(cur, 1))
    nxt = jnp.where(pos == tm - 1, 0.0, _shift_rows(cur, -1))
    return jnp.where(ch < width // 2, prev, nxt)


def _rwkv_prep_kernel(grid_mode, nblk, *refs):
    if grid_mode:
        (rg_ref, rgp_ref, rgn_ref, rwk_ref, rwkp_ref, rwkn_ref) = refs[:6]
        rest = refs[6:]
    else:
        rg_ref, rwk_ref = refs[:2]
        rest = refs[2:]
    (mu_rg_ref, mu_rwk_ref, kkp_ref, ka_ref, w0_ref, w2_ref, a0_ref, a2_ref,
     r_ref, gd_ref, v_ref, kk_ref, lw0_ref, lw1_ref, b0_ref, b1_ref, kd0_ref, kd1_ref) = rest
    i = pl.program_id(1)
    first, last = i == 0, i == nblk - 1
    rg, rwk = rg_ref[0], rwk_ref[0]
    if grid_mode:
        rg_s = _token_shift(rg, rgp_ref[0], rgn_ref[0], True, first, last)
        rwk_s = _token_shift(rwk, rwkp_ref[0], rwkn_ref[0], True, first, last)
    else:
        rg_s = _token_shift(rg, None, None, False, first, last)
        rwk_s = _token_shift(rwk, None, None, False, first, last)
    rg = rg + (rg_s - rg) * mu_rg_ref[...]
    rwk = rwk + (rwk_s - rwk) * mu_rwk_ref[...]
    r_ref[0] = rg[:, :RW_W]
    gd_ref[0] = rg[:, RW_W:]
    k = rwk[:, :RW_W]
    v_ref[0] = rwk[:, RW_W:2 * RW_W]
    kk = k * kkp_ref[...]
    norm = jnp.sqrt(_dot_exact_rhs(kk * kk, _head_sum_matrix(RW_W)))
    kk = kk / jnp.maximum(norm, 1e-12)
    kk_ref[0] = kk
    base = 2 * RW_W
    for d, (lw_ref, b_ref, kd_ref) in enumerate(((lw0_ref, b0_ref, kd0_ref), (lw1_ref, b1_ref, kd1_ref))):
        wd = rwk[:, base + d * W_LORA:base + (d + 1) * W_LORA]
        ad = rwk[:, base + N_DIR * W_LORA + d * A_LORA:base + N_DIR * W_LORA + (d + 1) * A_LORA]
        z = w0_ref[d] + _dot(jnp.tanh(wd), w2_ref[d])
        lw_ref[0] = LOG_DECAY_SCALE * jax.nn.sigmoid(z)
        a = jax.nn.sigmoid(a0_ref[d] + _dot(ad, a2_ref[d]))
        b_ref[0] = kk * a
        kd_ref[0] = k * (1.0 + (a - 1.0) * ka_ref[...])


def _rwkv_prep(rg, rwk, mu, kkp, ka, w0, w2_bf16, a0, a2_bf16, grid_mode, tm):
    b, t, _ = rg.shape
    nblk = t // tm
    row = lambda bi, i: (bi, i, 0)
    const2 = lambda bi, i: (0, 0)
    const3 = lambda bi, i: (0, 0, 0)
    in_specs, args = [], []
    for arr, width in ((rg, MU_HEAD), (rwk, RWK_W)):
        in_specs.append(pl.BlockSpec((1, tm, width), row))
        args.append(arr)
        if grid_mode:
            per = tm // GRID_W
            nrow = t // GRID_W
            in_specs.append(pl.BlockSpec(
                (1, GRID_W, width), lambda bi, i: (bi, jnp.maximum(i * per - 1, 0), 0)))
            in_specs.append(pl.BlockSpec(
                (1, GRID_W, width), lambda bi, i: (bi, jnp.minimum((i + 1) * per, nrow - 1), 0)))
            args += [arr, arr]
    in_specs += [
        pl.BlockSpec((1, MU_HEAD), const2),
        pl.BlockSpec((1, RWK_W), const2),
        pl.BlockSpec((1, RW_W), const2),
        pl.BlockSpec((1, RW_W), const2),
        pl.BlockSpec((N_DIR, 1, RW_W), const3),
        pl.BlockSpec((N_DIR, W_LORA, RW_W), const3),
        pl.BlockSpec((N_DIR, 1, RW_W), const3),
        pl.BlockSpec((N_DIR, A_LORA, RW_W), const3),
    ]
    args += [mu[:, :MU_HEAD], mu[:, MU_HEAD:], kkp, ka, w0, w2_bf16, a0, a2_bf16]
    widths = [RW_W, G_LORA] + [RW_W] * 8
    return pl.pallas_call(
        functools.partial(_rwkv_prep_kernel, grid_mode, nblk),
        grid=(b, nblk),
        in_specs=in_specs,
        out_specs=[pl.BlockSpec((1, tm, w), row) for w in widths],
        out_shape=[jax.ShapeDtypeStruct((b, t, w), F32) for w in widths],
        compiler_params=_params(2),
        name="rwkv_prep",
    )(*args)


def _scan_direction(reverse, lw, b, kd, kk, v, r, st_ref, y_ref):
    L = SCAN_CHUNK
    ii = lax.broadcasted_iota(jnp.int32, (L, L), 0)
    jj = lax.broadcasted_iota(jnp.int32, (L, L), 1)
    cum = jnp.where((ii <= jj) if reverse else (ii >= jj), 1.0, 0.0).astype(BF16)
    lw_cum = _dot_exact_lhs(cum, lw)
    yield
    lw_tot = lw_cum[0:1] if reverse else lw_cum[L - 1:L]
    e_neg = jnp.exp(-lw_cum)
    e_end = jnp.exp(lw_tot - lw_cum)

    ti = lax.broadcasted_iota(jnp.int32, (L, RW_W), 0)
    lane = lax.broadcasted_iota(jnp.int32, (L, RW_W), 1)
    si = lane % HEAD_DIM
    strict = (ti < si) if reverse else (ti > si)
    incl = (ti <= si) if reverse else (ti >= si)
    eye = ti == si
    row = lax.broadcasted_iota(jnp.int32, (RW_W, RW_W), 0)
    col = lax.broadcasted_iota(jnp.int32, (RW_W, RW_W), 1)
    blk = (row // HEAD_DIM) == (col // HEAD_DIM)

    def block_diag(x):
        return jnp.where(blk, jnp.concatenate([x] * RW_HEADS, axis=0), 0.0).astype(BF16)

    def diag_blocks(full):
        acc = None
        for h in range(RW_HEADS):
            part = jnp.where(lane // HEAD_DIM == h, full[h * L:(h + 1) * L], 0.0)
            acc = part if acc is None else acc + part
        return acc

    alpha = kk * jnp.exp(lw_cum - lw)
    rho = r * jnp.exp(lw_cum)
    beta_end = b * e_end
    kappa_end = kd * e_end
    v_bd = block_diag(v)
    p1 = _dot_nt(jnp.concatenate([alpha, rho], axis=0),
                 jnp.concatenate([block_diag(b * e_neg), block_diag(kd * e_neg)], axis=0))
    yield
    nn = jnp.where(strict, p1[:L, :RW_W], 0.0)
    ak = jnp.where(strict, p1[:L, RW_W:], 0.0)
    rb = jnp.where(incl, p1[L:, :RW_W], 0.0)
    rk = jnp.where(incl, p1[L:, RW_W:], 0.0)
    tinv = jnp.where(eye, 1.0, 0.0) - nn
    npow = _dot(nn, block_diag(nn))
    kv = _dot(jnp.concatenate([ak, rk], axis=0), v_bd)
    akv, rkv = kv[:L], kv[L:]
    yield
    for _ in range(int(math.log2(L)) - 2):
        both = _dot(jnp.concatenate([npow, tinv], axis=0), block_diag(npow))
        npow, tinv = both[:L], tinv + both[L:]
        yield
    tinv = tinv + _dot(tinv, block_diag(npow))
    yield
    z = _dot(tinv, jnp.concatenate([block_diag(alpha), block_diag(-akv)], axis=1))
    yield
    t_alpha, x0 = z[:, :RW_W], z[:, RW_W:]
    rbz = _dot(rb, jnp.concatenate([block_diag(t_alpha), block_diag(x0)], axis=1))
    q_mat = rho - rbz[:, :RW_W]
    y0 = rbz[:, RW_W:] + rkv
    m_mat = jnp.where(eye, jnp.exp(lw_tot), 0.0) - diag_blocks(_dot_tn(beta_end, t_alpha))
    c_mat = diag_blocks(_dot_tn(jnp.concatenate([beta_end, kappa_end], axis=0),
                                jnp.concatenate([x0, v], axis=0)))
    yield
    both = _dot(jnp.concatenate([q_mat, m_mat], axis=0), block_diag(st_ref[...]))
    y_ref[...] = both[:L] + y0
    st_ref[...] = both[L:] + c_mat


def _scan_kernel(nchunk, nbatch, lw0_ref, b0_ref, kd0_ref, kkf_ref, vf_ref, rf_ref,
                 lw1_ref, b1_ref, kd1_ref, kkb_ref, vb_ref, rb_ref, s0_ref,
                 yf_ref, yb_ref, sT_ref, st_ref):
    c = pl.program_id(1)

    @pl.when(c == 0)
    def _():
        st_ref[...] = s0_ref[...]

    chains = []
    for bi in range(nbatch):
        chains.append(_scan_direction(False, lw0_ref[bi], b0_ref[bi], kd0_ref[bi], kkf_ref[bi], vf_ref[bi],
                                      rf_ref[bi], st_ref.at[bi, 0], yf_ref.at[bi]))
        chains.append(_scan_direction(True, lw1_ref[bi], b1_ref[bi], kd1_ref[bi], kkb_ref[bi], vb_ref[bi],
                                      rb_ref[bi], st_ref.at[bi, 1], yb_ref.at[bi]))
    _round_robin(chains)

    @pl.when(c == nchunk - 1)
    def _():
        sT_ref[...] = st_ref[...]


def _scan(lw0, lw1, b0, b1, kd0, kd1, kk, v, r, s0, nbatch):
    assert SCAN_CHUNK == HEAD_DIM
    bsz, t, _ = kk.shape
    nchunk = t // SCAN_CHUNK
    fwd = lambda bi, c: (bi, c, 0)
    bwd = lambda bi, c: (bi, nchunk - 1 - c, 0)
    blk = (nbatch, SCAN_CHUNK, RW_W)
    st_blk = (nbatch, N_DIR, HEAD_DIM, RW_W)
    st_map = lambda bi, c: (bi, 0, 0, 0)
    return pl.pallas_call(
        functools.partial(_scan_kernel, nchunk, nbatch),
        grid=(bsz // nbatch, nchunk),
        in_specs=[pl.BlockSpec(blk, fwd)] * 6 + [pl.BlockSpec(blk, bwd)] * 6
        + [pl.BlockSpec(st_blk, st_map)],
        out_specs=[pl.BlockSpec(blk, fwd), pl.BlockSpec(blk, bwd), pl.BlockSpec(st_blk, st_map)],
        out_shape=[jax.ShapeDtypeStruct((bsz, t, RW_W), F32),
                   jax.ShapeDtypeStruct((bsz, t, RW_W), F32),
                   jax.ShapeDtypeStruct(s0.shape, F32)],
        scratch_shapes=[pltpu.VMEM(st_blk, F32)],
        compiler_params=_params(2),
        name="rwkv_scan",
    )(lw0, b0, kd0, kk, v, r, lw1, b1, kd1, kk, v, r, s0)


FFN_SPLIT = 2
OUT_SUBTILES = 2


def _round_robin(chains):
    while chains:
        chains = [g for g in chains if next(g, True) is None]


def _out_rows(rows, x_ref, sg_ref, att_ref, yf_ref, yb_ref, r_ref, kd0_ref, kd1_ref, v_ref, gd_ref,
              gt1_ref, sh2_ref, sc2_ref, gt2_ref, ng_ref, lng_ref, lnb_ref, rk_ref, g2_ref,
              wo_ref, wgu_ref, wd_ref, o_ref):
    hsum = _head_sum_matrix(RW_W)
    y = yf_ref[0, rows] + yb_ref[0, rows]
    mu = _dot_exact_rhs(y, hsum) * (1.0 / HEAD_DIM)
    bonus = _dot_exact_rhs(r_ref[0, rows] * (kd0_ref[0, rows] + kd1_ref[0, rows]) * rk_ref[...], hsum) * v_ref[0, rows]
    gate = _dot(jax.nn.sigmoid(gd_ref[0, rows]), g2_ref[...])
    y_mix = _dot(sg_ref[0, rows], wo_ref[:A_W]) + _dot(att_ref[0, rows], wo_ref[A_W:A_W + ATT_Q])
    yield
    yc = y - mu
    var = _dot_exact_rhs(yc * yc, hsum) * (1.0 / HEAD_DIM)
    yield
    yn = yc * lax.rsqrt(var + GN_EPS) * lng_ref[...] + lnb_ref[...]
    o_rw = (yn + bonus) * gate
    y_mix = y_mix + _dot(o_rw, wo_ref[A_W + ATT_Q:])
    yield
    x1 = x_ref[0, rows] + gt1_ref[0] * _rms_norm(y_mix, ng_ref[1:2])
    h2 = (_rms_norm(x1, ng_ref[2:3]) * (1.0 + sc2_ref[0]) + sh2_ref[0]).astype(BF16)
    fc = D_FF // FFN_SPLIT
    acc = None
    for j in range(FFN_SPLIT):
        gg = jnp.dot(h2, wgu_ref[:, j * fc:(j + 1) * fc], preferred_element_type=F32)
        uu = jnp.dot(h2, wgu_ref[:, D_FF + j * fc:D_FF + (j + 1) * fc], preferred_element_type=F32)
        yield
        part = _dot(gg * jax.nn.sigmoid(gg) * uu, wd_ref[j * fc:(j + 1) * fc])
        acc = part if acc is None else acc + part
        yield
    o_ref[0, rows] = x1 + gt2_ref[0] * _rms_norm(acc, ng_ref[3:4])


def _out_kernel(nsub, *refs):
    tm = refs[0].shape[1]
    sub = tm // nsub
    _round_robin([_out_rows(pl.ds(i * sub, sub), *refs) for i in range(nsub)])


def _mix_out_ffn(x, o_sg, o_att, yf, yb, r, kd0, kd1, v, gd, gt1, sh2, sc2, gt2, norm_g,
                 lnx_g, lnx_b, rk, g2_bf16, wo_bf16, wgu_bf16, wd_bf16, tm):
    b, t, d = x.shape
    row = lambda bi, i: (bi, i, 0)
    per_b = lambda bi, i: (bi, 0, 0)
    const2 = lambda bi, i: (0, 0)
    single = pl.Buffered(1)
    in_specs = [
        pl.BlockSpec((1, tm, d), row),
        pl.BlockSpec((1, tm, A_W), row),
        pl.BlockSpec((1, tm, ATT_Q), row),
    ] + [pl.BlockSpec((1, tm, RW_W), row)] * 6 + [
        pl.BlockSpec((1, tm, G_LORA), row),
    ] + [pl.BlockSpec((1, 1, d), per_b)] * 4 + [
        pl.BlockSpec((4, d), const2),
        pl.BlockSpec((1, RW_W), const2),
        pl.BlockSpec((1, RW_W), const2),
        pl.BlockSpec((1, RW_W), const2),
        pl.BlockSpec((G_LORA, RW_W), const2),
        pl.BlockSpec((MIX_W, d), const2, pipeline_mode=single),
        pl.BlockSpec((d, 2 * D_FF), const2, pipeline_mode=single),
        pl.BlockSpec((D_FF, d), const2, pipeline_mode=single),
    ]
    return pl.pallas_call(
        functools.partial(_out_kernel, OUT_SUBTILES if tm % (OUT_SUBTILES * V7X_SUBLANES) == 0 else 1),
        grid=(b, t // tm),
        in_specs=in_specs,
        out_specs=pl.BlockSpec((1, tm, d), row),
        out_shape=jax.ShapeDtypeStruct((b, t, d), F32),
        compiler_params=_params(2),
        name="mix_out_ffn",
    )(x, o_sg, o_att, yf, yb, r, kd0, kd1, v, gd, gt1, sh2, sc2, gt2, norm_g,
      lnx_g, lnx_b, rk, g2_bf16, wo_bf16, wgu_bf16, wd_bf16)


def _rope_tables(t):
    pos = jnp.arange(t)
    inv = ROPE_THETA ** (-jnp.arange(ROPE_FREQS, dtype=F32) / ROPE_FREQS)
    ang_r = (pos // GRID_W).astype(F32)[:, None] * inv
    ang_c = (pos % GRID_W).astype(F32)[:, None] * inv
    cos = jnp.concatenate([jnp.cos(ang_r)] * 2 + [jnp.cos(ang_c)] * 2, axis=-1)
    sin = jnp.concatenate([-jnp.sin(ang_r), jnp.sin(ang_r), -jnp.sin(ang_c), jnp.sin(ang_c)], axis=-1)
    return jnp.concatenate([cos, cos], axis=-1), jnp.concatenate([sin, sin], axis=-1)


def _row_tile(t, target):
    tm = min(t, target)
    while t % tm:
        tm //= 2
    return tm


def kernel(x, c, ctx, c_ctx, w_mod, b_mod, norm_g, w_in, w_out, sg_ln_g, sg_ln_b, sg_w, sg_b,
           attn_sink, rw_mu, rw_w0, rw_w2, rw_a0, rw_a2, rw_kk, rw_ka, rw_rk, rw_g2,
           rw_lnx_g, rw_lnx_b, w_gu, w_down):
    bsz, t, d = x.shape
    clen = ctx.shape[1]
    depth = w_mod.shape[0]

    rows = -(-(bsz + 1) // V7X_SUBLANES) * V7X_SUBLANES
    c_rows = jnp.zeros((rows, d), F32).at[:bsz].set(c).at[bsz].set(c_ctx)
    mods = _modulation(c_rows, w_mod, b_mod)

    cos, sin = _rope_tables(t)
    cos_c, sin_c = cos[:clen], sin[:clen]
    s_zero = jnp.zeros((bsz, N_DIR, HEAD_DIM, RW_W), F32)

    tm_proj = _row_tile(t, 512)
    tm_gmlp = _row_tile(t, 512)
    tm_prep = _row_tile(t, 512)
    tm_out = _row_tile(t, 512)
    tc_proj = _row_tile(clen, 512)
    tc_out = _row_tile(clen, 256)
    nb_scan = SCAN_BATCH if bsz % SCAN_BATCH == 0 else 1

    xc = ctx
    for l in range(depth):
        last = l == depth - 1
        lat = [mods[l, :bsz, i * d:(i + 1) * d][:, None, :] for i in range(6)]
        cm = [jnp.broadcast_to(mods[l, bsz, i * d:(i + 1) * d][None, None, :], (bsz, 1, d))
              for i in range(6)]
        g = norm_g[l]
        w_in_b = w_in[l].astype(BF16)
        w2_b, a2_b = rw_w2[l].astype(BF16), rw_a2[l].astype(BF16)
        prep_params = (rw_mu[l][None], rw_kk[l][None], rw_ka[l][None], rw_w0[l][:, None, :], w2_b,
                       rw_a0[l][:, None, :], a2_b)

        c_a, c_q, c_rg, c_kv, c_rwk = _proj_in(xc, g[0:1], cm[0], cm[1], w_in_b, cos_c, sin_c, False, tc_proj)
        (c_r, c_gd, c_v, c_kk, c_lw0, c_lw1, c_b0, c_b1, c_kd0, c_kd1) = _rwkv_prep(
            c_rg, c_rwk, *prep_params, False, clen)
        c_yf, c_yb, s_ctx = _scan(c_lw0, c_lw1, c_b0, c_b1, c_kd0, c_kd1, c_kk, c_v, c_r, s_zero, nb_scan)

        p_a, p_q, p_rg, p_kv, p_rwk = _proj_in(x, g[0:1], lat[0], lat[1], w_in_b, cos, sin, True, tm_proj)
        (r, gd, v, kk, lw0, lw1, b0, b1, kd0, kd1) = _rwkv_prep(p_rg, p_rwk, *prep_params, True, tm_prep)
        yf, yb, _ = _scan(lw0, lw1, b0, b1, kd0, kd1, kk, v, r, s_ctx, nb_scan)

        sg_w_b = sg_w[l].astype(BF16)
        bias_plane = jnp.repeat(sg_b[l].T, HEAD_DIM, axis=1)
        o_sg = _gmlp(p_a, sg_ln_g[l][None], sg_ln_b[l][None], sg_w_b, bias_plane, tm_gmlp)
        o_att = _attention(p_q, p_kv, c_kv, attn_sink[l], True)

        out_params = (g, rw_lnx_g[l][None], rw_lnx_b[l][None], rw_rk[l].reshape(1, RW_W),
                      rw_g2[l].astype(BF16), w_out[l].astype(BF16), w_gu[l].astype(BF16),
                      w_down[l].astype(BF16))
        x_new = _mix_out_ffn(x, o_sg, o_att, yf, yb, r, kd0, kd1, v, gd,
                             lat[2], lat[3], lat[4], lat[5], *out_params, tm_out)
        if not last:
            oc_sg = _gmlp(c_a, sg_ln_g[l][None], sg_ln_b[l][None], sg_w_b, bias_plane, _row_tile(clen, 512))
            oc_att = _attention(c_q, c_kv, c_kv, attn_sink[l], False)
            xc = _mix_out_ffn(xc, oc_sg, oc_att, c_yf, c_yb, c_r, c_kd0, c_kd1, c_v, c_gd,
                              cm[2], cm[3], cm[4], cm[5], *out_params, tc_out)
        x = x_new
    return x
```

```python
import functools
import math

import jax
import jax.numpy as jnp
from jax import lax
from jax.experimental import pallas as pl
from jax.experimental.pallas import tpu as pltpu

F32 = jnp.float32
BF16 = jnp.bfloat16

D_MODEL = 1024
HEAD_DIM = 64
GRID_W = 64
A_GROUPS = 4
A_W = A_GROUPS * HEAD_DIM
CHUNK = 128
ATT_HEADS = 8
ATT_KV_HEADS = 2
ATT_Q = ATT_HEADS * HEAD_DIM
ATT_KV = ATT_KV_HEADS * HEAD_DIM
WINDOW = 128
QBLOCK = 128
ROPE_THETA = 10000.0
ROPE_FREQS = HEAD_DIM // 4
NEG_INF = -1e30
ATT_SCALE = HEAD_DIM ** -0.5
RW_HEADS = 4
RW_W = RW_HEADS * HEAD_DIM
W_LORA = 64
A_LORA = 64
G_LORA = 128
N_DIR = 2
GN_EPS = 64e-5
MIX_W = A_W + ATT_Q + RW_W
D_FF = -(-8 * D_MODEL // (3 * 256)) * 256
RMS_EPS = 1e-6
LN_EPS = 1e-5
OFF_Q = 2 * A_W
OFF_R = OFF_Q + ATT_Q
OFF_CTX = OFF_R + RW_W + G_LORA
OFF_RWK = OFF_CTX + 2 * ATT_KV
IN_COLS = OFF_RWK + 2 * RW_W + N_DIR * (W_LORA + A_LORA)
MU_HEAD = RW_W + G_LORA
RWK_W = 2 * RW_W + N_DIR * (W_LORA + A_LORA)
IN_GROUPS = ((0, OFF_Q), (OFF_Q, ATT_Q), (OFF_R, MU_HEAD), (OFF_CTX, 2 * ATT_KV), (OFF_RWK, RWK_W))

V7X_LANES = 128
V7X_SUBLANES = 8
V7X_VMEM_LIMIT_BYTES = 56 * 1024 * 1024

SCAN_CHUNK = 64
SCAN_BATCH = 4
LOG_DECAY_SCALE = -math.exp(-0.5)


def _params(n_grid):
    return pltpu.CompilerParams(
        dimension_semantics=("arbitrary",) * n_grid, vmem_limit_bytes=V7X_VMEM_LIMIT_BYTES)


def _dot(a, b):
    return jnp.dot(a.astype(BF16), b.astype(BF16), preferred_element_type=F32)


def _dot_nt(a, b):
    return lax.dot_general(a.astype(BF16), b.astype(BF16), (((1,), (1,)), ((), ())),
                           preferred_element_type=F32)


def _split_bf16(x, n):
    parts = []
    for _ in range(n - 1):
        h = x.astype(BF16)
        parts.append(h)
        x = x - h.astype(F32)
    parts.append(x.astype(BF16))
    return parts


def _dot_exact_rhs(x, m_bf16, n=3):
    acc = None
    for p in _split_bf16(x, n):
        t = jnp.dot(p, m_bf16, preferred_element_type=F32)
        acc = t if acc is None else acc + t
    return acc


def _dot_exact_lhs(m_bf16, x, n=3):
    acc = None
    for p in _split_bf16(x, n):
        t = jnp.dot(m_bf16, p, preferred_element_type=F32)
        acc = t if acc is None else acc + t
    return acc


def _head_sum_matrix(width):
    i = lax.broadcasted_iota(jnp.int32, (width, width), 0) // HEAD_DIM
    j = lax.broadcasted_iota(jnp.int32, (width, width), 1) // HEAD_DIM
    return jnp.where(i == j, 1.0, 0.0).astype(BF16)


def _rms_norm(x, g):
    return x * lax.rsqrt(jnp.mean(x * x, axis=-1, keepdims=True) + RMS_EPS) * g


def _tile_lanes(x, width):
    reps = width // x.shape[-1]
    return x if reps == 1 else jnp.concatenate([x] * reps, axis=-1)


def _rope(x, cos, sin_signed):
    w = x.shape[-1]
    lane = lax.broadcasted_iota(jnp.int32, x.shape, 1)
    lo = (lane % (2 * ROPE_FREQS)) < ROPE_FREQS
    partner = jnp.where(lo, pltpu.roll(x, w - ROPE_FREQS, 1), pltpu.roll(x, ROPE_FREQS, 1))
    return x * _tile_lanes(cos, w) + partner * _tile_lanes(sin_signed, w)


def _mod_kernel(c_ref, w_ref, b_ref, o_ref):
    c = c_ref[...]
    act = c * jax.nn.sigmoid(c)
    o_ref[0] = _dot(act, w_ref[0]) + b_ref[0]


def _modulation(c_rows, w_mod, b_mod):
    depth, d, n = w_mod.shape
    rows = c_rows.shape[0]
    tn = n // 4
    return pl.pallas_call(
        _mod_kernel,
        grid=(depth, n // tn),
        in_specs=[
            pl.BlockSpec((rows, d), lambda l, j: (0, 0)),
            pl.BlockSpec((1, d, tn), lambda l, j: (l, 0, j)),
            pl.BlockSpec((1, 1, tn), lambda l, j: (l, 0, j)),
        ],
        out_specs=pl.BlockSpec((1, rows, tn), lambda l, j: (l, 0, j)),
        out_shape=jax.ShapeDtypeStruct((depth, rows, n), F32),
        compiler_params=_params(2),
        name="modulation",
    )(c_rows, w_mod, b_mod.reshape(depth, 1, n))


def _proj_in_kernel(rope, x_ref, g_ref, sh_ref, sc_ref, w_ref, cos_ref, sin_ref, *o_refs):
    h = _rms_norm(x_ref[0], g_ref[...]) * (1.0 + sc_ref[0]) + sh_ref[0]
    hb = h.astype(BF16)
    for (start, width), o_ref in zip(IN_GROUPS, o_refs):
        p = jnp.dot(hb, w_ref[:, start:start + width], preferred_element_type=F32)
        if rope and start == OFF_Q:
            p = _rope(p, cos_ref[...], sin_ref[...])
        if start == OFF_Q:
            p = p * ATT_SCALE
        if rope and start == OFF_CTX:
            p = jnp.concatenate(
                [_rope(p[:, :ATT_KV], cos_ref[...], sin_ref[...]), p[:, ATT_KV:]], axis=-1)
        o_ref[0] = p


def _proj_in(x, g, sh, sc, w_bf16, cos, sin_signed, rope, tm):
    b, t, d = x.shape
    row = lambda bi, i: (bi, i, 0)
    per_b = lambda bi, i: (bi, 0, 0)
    const = lambda bi, i: (0, 0)
    return pl.pallas_call(
        functools.partial(_proj_in_kernel, rope),
        grid=(b, t // tm),
        in_specs=[
            pl.BlockSpec((1, tm, d), row),
            pl.BlockSpec((1, d), const),
            pl.BlockSpec((1, 1, d), per_b),
            pl.BlockSpec((1, 1, d), per_b),
            pl.BlockSpec((d, IN_COLS), const),
            pl.BlockSpec((tm, V7X_LANES), lambda bi, i: (i, 0)),
            pl.BlockSpec((tm, V7X_LANES), lambda bi, i: (i, 0)),
        ],
        out_specs=[pl.BlockSpec((1, tm, width), row) for _, width in IN_GROUPS],
        out_shape=[jax.ShapeDtypeStruct((b, t, width), F32) for _, width in IN_GROUPS],
        compiler_params=_params(2),
        name="proj_in",
    )(x, g, sh, sc, w_bf16, cos, sin_signed)


def _gmlp_kernel(z_ref, g_ref, b_ref, w_ref, bias_ref, o_ref):
    z = jax.nn.gelu(z_ref[0], approximate=True)
    u, v = z[:, :A_W], z[:, A_W:]
    mu = jnp.mean(v, axis=-1, keepdims=True)
    var = jnp.mean(jnp.square(v - mu), axis=-1, keepdims=True)
    vn = ((v - mu) * lax.rsqrt(var + LN_EPS) * g_ref[...] + b_ref[...]).astype(BF16)
    bias = bias_ref[...]
    for c in range(z.shape[0] // CHUNK):
        rows = slice(c * CHUNK, (c + 1) * CHUNK)
        mixed = jnp.concatenate(
            [jnp.dot(w_ref[gi], vn[rows, gi * HEAD_DIM:(gi + 1) * HEAD_DIM],
                     preferred_element_type=F32) for gi in range(A_GROUPS)], axis=-1)
        o_ref[0, rows, :] = u[rows] * (mixed + bias)


def _gmlp(z, ln_g, ln_b, w_bf16, bias_plane, tm):
    b, t, _ = z.shape
    row = lambda bi, i: (bi, i, 0)
    const2 = lambda bi, i: (0, 0)
    return pl.pallas_call(
        _gmlp_kernel,
        grid=(b, t // tm),
        in_specs=[
            pl.BlockSpec((1, tm, 2 * A_W), row),
            pl.BlockSpec((1, A_W), const2),
            pl.BlockSpec((1, A_W), const2),
            pl.BlockSpec((A_GROUPS, CHUNK, CHUNK), lambda bi, i: (0, 0, 0)),
            pl.BlockSpec((CHUNK, A_W), const2),
        ],
        out_specs=pl.BlockSpec((1, tm, A_W), row),
        out_shape=jax.ShapeDtypeStruct((b, t, A_W), F32),
        compiler_params=_params(2),
        name="gmlp",
    )(z, ln_g, ln_b, w_bf16, bias_plane)


def _attn_kernel(local, nblk, sink_ref, q_ref, *refs):
    if local:
        kvp_ref, kvc_ref, kvn_ref, ctx_ref, o_ref = refs
    else:
        ctx_ref, o_ref = refs
    n = pl.program_id(1)
    group = ATT_HEADS // ATT_KV_HEADS
    q = q_ref[0]
    ctx = ctx_ref[0]
    rows = group * QBLOCK
    row_head = lax.broadcasted_iota(jnp.int32, (rows, 1), 0) // QBLOCK
    if local:
        kv_loc = jnp.concatenate([kvp_ref[0], kvc_ref[0], kvn_ref[0]], axis=0)
        qi = lax.broadcasted_iota(jnp.int32, (rows, 3 * QBLOCK), 0) % QBLOCK
        kj = lax.broadcasted_iota(jnp.int32, (rows, 3 * QBLOCK), 1)
        blk = kj // QBLOCK
        valid = (jnp.abs(kj - QBLOCK - qi) <= WINDOW)
        valid = valid & ((blk != 0) | (n > 0)) & ((blk != 2) | (n < nblk - 1))
    s_ctx, s_loc, sinks = [], [], []
    for g in range(ATT_KV_HEADS):
        qs = jnp.concatenate(
            [q[:, (g * group + i) * HEAD_DIM:(g * group + i + 1) * HEAD_DIM] for i in range(group)],
            axis=0)
        sink = jnp.zeros((rows, 1), F32)
        for i in range(group):
            sink = jnp.where(row_head == i, sink_ref[g * group + i], sink)
        sinks.append(sink)
        s_ctx.append(_dot_nt(qs, ctx[:, g * HEAD_DIM:(g + 1) * HEAD_DIM]))
        if local:
            s_loc.append(jnp.where(valid, _dot_nt(qs, kv_loc[:, g * HEAD_DIM:(g + 1) * HEAD_DIM]), NEG_INF))
    p_ctx, p_loc, denoms = [], [], []
    for g in range(ATT_KV_HEADS):
        m = jnp.maximum(jnp.max(s_ctx[g], axis=-1, keepdims=True), sinks[g])
        if local:
            m = jnp.maximum(m, jnp.max(s_loc[g], axis=-1, keepdims=True))
        p = jnp.exp(s_ctx[g] - m)
        denom = jnp.sum(p, axis=-1, keepdims=True) + jnp.exp(sinks[g] - m)
        p_ctx.append(p.astype(BF16))
        if local:
            p = jnp.exp(s_loc[g] - m)
            denom = denom + jnp.sum(p, axis=-1, keepdims=True)
            p_loc.append(p.astype(BF16))
        denoms.append(denom)
    for g in range(ATT_KV_HEADS):
        acc = _dot(p_ctx[g], ctx[:, ATT_KV + g * HEAD_DIM:ATT_KV + (g + 1) * HEAD_DIM])
        if local:
            acc = acc + _dot(p_loc[g], kv_loc[:, ATT_KV + g * HEAD_DIM:ATT_KV + (g + 1) * HEAD_DIM])
        out = acc / denoms[g]
        for i in range(group):
            h = g * group + i
            o_ref[0, :, h * HEAD_DIM:(h + 1) * HEAD_DIM] = out[i * QBLOCK:(i + 1) * QBLOCK]


def _attention(q, kv, kv_ctx, sink, local):
    b, t, _ = q.shape
    c = kv_ctx.shape[1]
    nblk = t // QBLOCK
    row = lambda bi, i: (bi, i, 0)
    in_specs = [
        pl.BlockSpec(memory_space=pltpu.SMEM),
        pl.BlockSpec((1, QBLOCK, ATT_Q), row),
    ]
    args = [sink, q]
    if local:
        in_specs += [
            pl.BlockSpec((1, QBLOCK, 2 * ATT_KV), lambda bi, i: (bi, jnp.maximum(i - 1, 0), 0)),
            pl.BlockSpec((1, QBLOCK, 2 * ATT_KV), row),
            pl.BlockSpec((1, QBLOCK, 2 * ATT_KV), lambda bi, i: (bi, jnp.minimum(i + 1, nblk - 1), 0)),
        ]
        args += [kv, kv, kv]
    in_specs.append(pl.BlockSpec((1, c, 2 * ATT_KV), lambda bi, i: (bi, 0, 0)))
    args.append(kv_ctx)
    return pl.pallas_call(
        functools.partial(_attn_kernel, local, nblk),
        grid=(b, nblk),
        in_specs=in_specs,
        out_specs=pl.BlockSpec((1, QBLOCK, ATT_Q), row),
        out_shape=jax.ShapeDtypeStruct((b, t, ATT_Q), F32),
        compiler_params=_params(2),
        name="attention",
    )(*args)


def _shift_rows(x, k):
    return pltpu.roll(x, k % x.shape[0], 0)


def _token_shift(cur, prev_rows, next_rows, grid_mode, first, last):
    tm, width = cur.shape
    ch = lax.broadcasted_iota(jnp.int32, (tm, width), 1)
    pos = lax.broadcasted_iota(jnp.int32, (tm, width), 0)
    if grid_mode:
        qw = width // 4
        col = pos % GRID_W
        left = jnp.where(col == 0, 0.0, _shift_rows(cur, 1))
        right = jnp.where(col == GRID_W - 1, 0.0, _shift_rows(cur, -1))
        up = jnp.concatenate([prev_rows, cur[:tm - GRID_W]], axis=0)
        up = jnp.where((pos < GRID_W) & first, 0.0, up)
        down = jnp.concatenate([cur[GRID_W:], next_rows], axis=0)
        down = jnp.where((pos >= tm - GRID_W) & last, 0.0, down)
        return jnp.where(ch < qw, left, jnp.where(ch < 2 * qw, right, jnp.where(ch < 3 * qw, up, down)))
    prev = jnp.where(pos == 0, 0.0, _shift_rows(cur, 1))
    nxt = jnp.where(pos == tm - 1, 0.0, _shift_rows(cur, -1))
    return jnp.where(ch < width // 2, prev, nxt)


def _rwkv_prep_kernel(grid_mode, nblk, *refs):
    if grid_mode:
        (rg_ref, rgp_ref, rgn_ref, rwk_ref, rwkp_ref, rwkn_ref) = refs[:6]
        rest = refs[6:]
    else:
        rg_ref, rwk_ref = refs[:2]
        rest = refs[2:]
    (mu_rg_ref, mu_rwk_ref, kkp_ref, ka_ref, w0_ref, w2_ref, a0_ref, a2_ref,
     r_ref, gd_ref, v_ref, kk_ref, lw0_ref, lw1_ref, b0_ref, b1_ref, kd0_ref, kd1_ref) = rest
    i = pl.program_id(1)
    first, last = i == 0, i == nblk - 1
    rg, rwk = rg_ref[0], rwk_ref[0]
    if grid_mode:
        rg_s = _token_shift(rg, rgp_ref[0], rgn_ref[0], True, first, last)
        rwk_s = _token_shift(rwk, rwkp_ref[0], rwkn_ref[0], True, first, last)
    else:
        rg_s = _token_shift(rg, None, None, False, first, last)
        rwk_s = _token_shift(rwk, None, None, False, first, last)
    rg = rg + (rg_s - rg) * mu_rg_ref[...]
    rwk = rwk + (rwk_s - rwk) * mu_rwk_ref[...]
    r_ref[0] = rg[:, :RW_W]
    gd_ref[0] = rg[:, RW_W:]
    k = rwk[:, :RW_W]
    v_ref[0] = rwk[:, RW_W:2 * RW_W]
    kk = k * kkp_ref[...]
    norm = jnp.sqrt(_dot_exact_rhs(kk * kk, _head_sum_matrix(RW_W)))
    kk = kk / jnp.maximum(norm, 1e-12)
    kk_ref[0] = kk
    base = 2 * RW_W
    for d, (lw_ref, b_ref, kd_ref) in enumerate(((lw0_ref, b0_ref, kd0_ref), (lw1_ref, b1_ref, kd1_ref))):
        wd = rwk[:, base + d * W_LORA:base + (d + 1) * W_LORA]
        ad = rwk[:, base + N_DIR * W_LORA + d * A_LORA:base + N_DIR * W_LORA + (d + 1) * A_LORA]
        z = w0_ref[d] + _dot(jnp.tanh(wd), w2_ref[d])
        lw = LOG_DECAY_SCALE * jax.nn.sigmoid(z)
        ci = lax.broadcasted_iota(jnp.int32, (SCAN_CHUNK, SCAN_CHUNK), 0)
        cj = lax.broadcasted_iota(jnp.int32, (SCAN_CHUNK, SCAN_CHUNK), 1)
        cum = jnp.where((ci <= cj) if d else (ci >= cj), 1.0, 0.0).astype(BF16)
        for c in range(lw.shape[0] // SCAN_CHUNK):
            rows = slice(c * SCAN_CHUNK, (c + 1) * SCAN_CHUNK)
            lw_ref[0, rows] = _dot_exact_lhs(cum, lw[rows])
        a = jax.nn.sigmoid(a0_ref[d] + _dot(ad, a2_ref[d]))
        b_ref[0] = kk * a
        kd_ref[0] = k * (1.0 + (a - 1.0) * ka_ref[...])


def _rwkv_prep(rg, rwk, mu, kkp, ka, w0, w2_bf16, a0, a2_bf16, grid_mode, tm):
    b, t, _ = rg.shape
    nblk = t // tm
    row = lambda bi, i: (bi, i, 0)
    const2 = lambda bi, i: (0, 0)
    const3 = lambda bi, i: (0, 0, 0)
    in_specs, args = [], []
    for arr, width in ((rg, MU_HEAD), (rwk, RWK_W)):
        in_specs.append(pl.BlockSpec((1, tm, width), row))
        args.append(arr)
        if grid_mode:
            per = tm // GRID_W
            nrow = t // GRID_W
            in_specs.append(pl.BlockSpec(
                (1, GRID_W, width), lambda bi, i: (bi, jnp.maximum(i * per - 1, 0), 0)))
            in_specs.append(pl.BlockSpec(
                (1, GRID_W, width), lambda bi, i: (bi, jnp.minimum((i + 1) * per, nrow - 1), 0)))
            args += [arr, arr]
    in_specs += [
        pl.BlockSpec((1, MU_HEAD), const2),
        pl.BlockSpec((1, RWK_W), const2),
        pl.BlockSpec((1, RW_W), const2),
        pl.BlockSpec((1, RW_W), const2),
        pl.BlockSpec((N_DIR, 1, RW_W), const3),
        pl.BlockSpec((N_DIR, W_LORA, RW_W), const3),
        pl.BlockSpec((N_DIR, 1, RW_W), const3),
        pl.BlockSpec((N_DIR, A_LORA, RW_W), const3),
    ]
    args += [mu[:, :MU_HEAD], mu[:, MU_HEAD:], kkp, ka, w0, w2_bf16, a0, a2_bf16]
    widths = [RW_W, G_LORA] + [RW_W] * 8
    return pl.pallas_call(
        functools.partial(_rwkv_prep_kernel, grid_mode, nblk),
        grid=(b, nblk),
        in_specs=in_specs,
        out_specs=[pl.BlockSpec((1, tm, w), row) for w in widths],
        out_shape=[jax.ShapeDtypeStruct((b, t, w), F32) for w in widths],
        compiler_params=_params(2),
        name="rwkv_prep",
    )(*args)


def _scan_direction(reverse, lw_cum, b, kd, kk, v, r, st_ref, y_ref):
    L = SCAN_CHUNK
    ti = lax.broadcasted_iota(jnp.int32, (L, RW_W), 0)
    if reverse:
        lw_before = jnp.where(ti == L - 1, 0.0, pltpu.roll(lw_cum, L - 1, 0))
    else:
        lw_before = jnp.where(ti == 0, 0.0, pltpu.roll(lw_cum, 1, 0))
    lw_tot = lw_cum[0:1] if reverse else lw_cum[L - 1:L]
    e_neg = jnp.exp(-lw_cum)
    e_end = jnp.exp(lw_tot - lw_cum)

    lane = lax.broadcasted_iota(jnp.int32, (L, RW_W), 1)
    si = lane % HEAD_DIM
    strict = (ti < si) if reverse else (ti > si)
    incl = (ti <= si) if reverse else (ti >= si)
    eye = ti == si
    row = lax.broadcasted_iota(jnp.int32, (RW_W, RW_W), 0)
    col = lax.broadcasted_iota(jnp.int32, (RW_W, RW_W), 1)
    blk = (row // HEAD_DIM) == (col // HEAD_DIM)

    def block_diag(x):
        return jnp.where(blk, jnp.concatenate([x] * RW_HEADS, axis=0), 0.0).astype(BF16)

    def head_transpose(x):
        xt = x.T
        return jnp.concatenate([xt[h * HEAD_DIM:(h + 1) * HEAD_DIM] for h in range(RW_HEADS)], axis=1)

    alpha = kk * jnp.exp(lw_before)
    rho = r * jnp.exp(lw_cum)
    beta_end_t = head_transpose(b * e_end)
    kappa_end_t = head_transpose(kd * e_end)
    v_bd = block_diag(v)
    p1 = _dot_nt(jnp.concatenate([alpha, rho], axis=0),
                 jnp.concatenate([block_diag(b * e_neg), block_diag(kd * e_neg)], axis=0))
    yield
    nn = jnp.where(strict, p1[:L, :RW_W], 0.0)
    ak = jnp.where(strict, p1[:L, RW_W:], 0.0)
    rb = jnp.where(incl, p1[L:, :RW_W], 0.0)
    rk = jnp.where(incl, p1[L:, RW_W:], 0.0)
    tinv = jnp.where(eye, 1.0, 0.0) - nn
    npow = _dot(nn, block_diag(nn))
    kv = _dot(jnp.concatenate([ak, rk, kappa_end_t], axis=0), v_bd)
    akv, rkv, kev = kv[:L], kv[L:2 * L], kv[2 * L:]
    yield
    for _ in range(int(math.log2(L)) - 2):
        both = _dot(jnp.concatenate([npow, tinv], axis=0), block_diag(npow))
        npow, tinv = both[:L], tinv + both[L:]
        yield
    tinv = tinv + _dot(tinv, block_diag(npow))
    yield
    z = _dot(tinv, jnp.concatenate([block_diag(alpha), block_diag(-akv)], axis=1))
    yield
    t_alpha, x0 = z[:, :RW_W], z[:, RW_W:]
    bz = _dot(jnp.concatenate([rb, beta_end_t], axis=0),
              jnp.concatenate([block_diag(t_alpha), block_diag(x0)], axis=1))
    q_mat = rho - bz[:L, :RW_W]
    y0 = bz[:L, RW_W:] + rkv
    m_mat = jnp.where(eye, jnp.exp(lw_tot), 0.0) - bz[L:, :RW_W]
    c_mat = bz[L:, RW_W:] + kev
    yield
    both = _dot(jnp.concatenate([q_mat, m_mat], axis=0), block_diag(st_ref[...]))
    y_ref[...] = both[:L] + y0
    st_ref[...] = both[L:] + c_mat


def _scan_kernel(nchunk, nbatch, lw0_ref, b0_ref, kd0_ref, kkf_ref, vf_ref, rf_ref,
                 lw1_ref, b1_ref, kd1_ref, kkb_ref, vb_ref, rb_ref, s0_ref,
                 yf_ref, yb_ref, sT_ref, st_ref):
    c = pl.program_id(1)

    @pl.when(c == 0)
    def _():
        st_ref[...] = s0_ref[...]

    chains = []
    for bi in range(nbatch):
        chains.append(_scan_direction(False, lw0_ref[bi], b0_ref[bi], kd0_ref[bi], kkf_ref[bi], vf_ref[bi],
                                      rf_ref[bi], st_ref.at[bi, 0], yf_ref.at[bi]))
        chains.append(_scan_direction(True, lw1_ref[bi], b1_ref[bi], kd1_ref[bi], kkb_ref[bi], vb_ref[bi],
                                      rb_ref[bi], st_ref.at[bi, 1], yb_ref.at[bi]))
    _round_robin(chains)

    @pl.when(c == nchunk - 1)
    def _():
        sT_ref[...] = st_ref[...]


def _scan(lw0, lw1, b0, b1, kd0, kd1, kk, v, r, s0, nbatch):
    assert SCAN_CHUNK == HEAD_DIM
    bsz, t, _ = kk.shape
    nchunk = t // SCAN_CHUNK
    fwd = lambda bi, c: (bi, c, 0)
    bwd = lambda bi, c: (bi, nchunk - 1 - c, 0)
    blk = (nbatch, SCAN_CHUNK, RW_W)
    st_blk = (nbatch, N_DIR, HEAD_DIM, RW_W)
    st_map = lambda bi, c: (bi, 0, 0, 0)
    return pl.pallas_call(
        functools.partial(_scan_kernel, nchunk, nbatch),
        grid=(bsz // nbatch, nchunk),
        in_specs=[pl.BlockSpec(blk, fwd)] * 6 + [pl.BlockSpec(blk, bwd)] * 6
        + [pl.BlockSpec(st_blk, st_map)],
        out_specs=[pl.BlockSpec(blk, fwd), pl.BlockSpec(blk, bwd), pl.BlockSpec(st_blk, st_map)],
        out_shape=[jax.ShapeDtypeStruct((bsz, t, RW_W), F32),
                   jax.ShapeDtypeStruct((bsz, t, RW_W), F32),
                   jax.ShapeDtypeStruct(s0.shape, F32)],
        scratch_shapes=[pltpu.VMEM(st_blk, F32)],
        compiler_params=_params(2),
        name="rwkv_scan",
    )(lw0, b0, kd0, kk, v, r, lw1, b1, kd1, kk, v, r, s0)


FFN_SPLIT = 2
OUT_SUBTILES = 2


def _round_robin(chains):
    while chains:
        chains = [g for g in chains if next(g, True) is None]


def _out_rows(rows, x_ref, sg_ref, att_ref, yf_ref, yb_ref, r_ref, kd0_ref, kd1_ref, v_ref, gd_ref,
              gt1_ref, sh2_ref, sc2_ref, gt2_ref, ng_ref, lng_ref, lnb_ref, rk_ref, g2_ref,
              wo_ref, wgu_ref, wd_ref, o_ref):
    hsum = _head_sum_matrix(RW_W)
    y = yf_ref[0, rows] + yb_ref[0, rows]
    mu = _dot_exact_rhs(y, hsum) * (1.0 / HEAD_DIM)
    bonus = _dot_exact_rhs(r_ref[0, rows] * (kd0_ref[0, rows] + kd1_ref[0, rows]) * rk_ref[...], hsum) * v_ref[0, rows]
    gate = _dot(jax.nn.sigmoid(gd_ref[0, rows]), g2_ref[...])
    y_mix = _dot(sg_ref[0, rows], wo_ref[:A_W]) + _dot(att_ref[0, rows], wo_ref[A_W:A_W + ATT_Q])
    yield
    yc = y - mu
    var = _dot_exact_rhs(yc * yc, hsum) * (1.0 / HEAD_DIM)
    yield
    yn = yc * lax.rsqrt(var + GN_EPS) * lng_ref[...] + lnb_ref[...]
    o_rw = (yn + bonus) * gate
    y_mix = y_mix + _dot(o_rw, wo_ref[A_W + ATT_Q:])
    yield
    x1 = x_ref[0, rows] + gt1_ref[0] * _rms_norm(y_mix, ng_ref[1:2])
    h2 = (_rms_norm(x1, ng_ref[2:3]) * (1.0 + sc2_ref[0]) + sh2_ref[0]).astype(BF16)
    fc = D_FF // FFN_SPLIT
    acc = None
    for j in range(FFN_SPLIT):
        gg = jnp.dot(h2, wgu_ref[:, j * fc:(j + 1) * fc], preferred_element_type=F32)
        uu = jnp.dot(h2, wgu_ref[:, D_FF + j * fc:D_FF + (j + 1) * fc], preferred_element_type=F32)
        yield
        part = _dot(gg * jax.nn.sigmoid(gg) * uu, wd_ref[j * fc:(j + 1) * fc])
        acc = part if acc is None else acc + part
        yield
    o_ref[0, rows] = x1 + gt2_ref[0] * _rms_norm(acc, ng_ref[3:4])


def _out_kernel(nsub, *refs):
    tm = refs[0].shape[1]
    sub = tm // nsub
    _round_robin([_out_rows(pl.ds(i * sub, sub), *refs) for i in range(nsub)])


def _mix_out_ffn(x, o_sg, o_att, yf, yb, r, kd0, kd1, v, gd, gt1, sh2, sc2, gt2, norm_g,
                 lnx_g, lnx_b, rk, g2_bf16, wo_bf16, wgu_bf16, wd_bf16, tm):
    b, t, d = x.shape
    row = lambda bi, i: (bi, i, 0)
    per_b = lambda bi, i: (bi, 0, 0)
    const2 = lambda bi, i: (0, 0)
    single = pl.Buffered(1)
    in_specs = [
        pl.BlockSpec((1, tm, d), row),
        pl.BlockSpec((1, tm, A_W), row),
        pl.BlockSpec((1, tm, ATT_Q), row),
    ] + [pl.BlockSpec((1, tm, RW_W), row)] * 6 + [
        pl.BlockSpec((1, tm, G_LORA), row),
    ] + [pl.BlockSpec((1, 1, d), per_b)] * 4 + [
        pl.BlockSpec((4, d), const2),
        pl.BlockSpec((1, RW_W), const2),
        pl.BlockSpec((1, RW_W), const2),
        pl.BlockSpec((1, RW_W), const2),
        pl.BlockSpec((G_LORA, RW_W), const2),
        pl.BlockSpec((MIX_W, d), const2, pipeline_mode=single),
        pl.BlockSpec((d, 2 * D_FF), const2, pipeline_mode=single),
        pl.BlockSpec((D_FF, d), const2, pipeline_mode=single),
    ]
    return pl.pallas_call(
        functools.partial(_out_kernel, OUT_SUBTILES if tm % (OUT_SUBTILES * V7X_SUBLANES) == 0 else 1),
        grid=(b, t // tm),
        in_specs=in_specs,
        out_specs=pl.BlockSpec((1, tm, d), row),
        out_shape=jax.ShapeDtypeStruct((b, t, d), F32),
        compiler_params=_params(2),
        name="mix_out_ffn",
    )(x, o_sg, o_att, yf, yb, r, kd0, kd1, v, gd, gt1, sh2, sc2, gt2, norm_g,
      lnx_g, lnx_b, rk, g2_bf16, wo_bf16, wgu_bf16, wd_bf16)


def _rope_tables(t):
    pos = jnp.arange(t)
    inv = ROPE_THETA ** (-jnp.arange(ROPE_FREQS, dtype=F32) / ROPE_FREQS)
    ang_r = (pos // GRID_W).astype(F32)[:, None] * inv
    ang_c = (pos % GRID_W).astype(F32)[:, None] * inv
    cos = jnp.concatenate([jnp.cos(ang_r)] * 2 + [jnp.cos(ang_c)] * 2, axis=-1)
    sin = jnp.concatenate([-jnp.sin(ang_r), jnp.sin(ang_r), -jnp.sin(ang_c), jnp.sin(ang_c)], axis=-1)
    return jnp.concatenate([cos, cos], axis=-1), jnp.concatenate([sin, sin], axis=-1)


def _row_tile(t, target):
    tm = min(t, target)
    while t % tm:
        tm //= 2
    return tm


def kernel(x, c, ctx, c_ctx, w_mod, b_mod, norm_g, w_in, w_out, sg_ln_g, sg_ln_b, sg_w, sg_b,
           attn_sink, rw_mu, rw_w0, rw_w2, rw_a0, rw_a2, rw_kk, rw_ka, rw_rk, rw_g2,
           rw_lnx_g, rw_lnx_b, w_gu, w_down):
    bsz, t, d = x.shape
    clen = ctx.shape[1]
    depth = w_mod.shape[0]

    rows = -(-(bsz + 1) // V7X_SUBLANES) * V7X_SUBLANES
    c_rows = jnp.zeros((rows, d), F32).at[:bsz].set(c).at[bsz].set(c_ctx)
    mods = _modulation(c_rows, w_mod, b_mod)

    cos, sin = _rope_tables(t)
    cos_c, sin_c = cos[:clen], sin[:clen]
    s_zero = jnp.zeros((bsz, N_DIR, HEAD_DIM, RW_W), F32)

    tm_proj = _row_tile(t, 512)
    tm_gmlp = _row_tile(t, 512)
    tm_prep = _row_tile(t, 512)
    tm_out = _row_tile(t, 512)
    tc_proj = _row_tile(clen, 512)
    tc_out = _row_tile(clen, 256)
    nb_scan = SCAN_BATCH if bsz % SCAN_BATCH == 0 else 1

    xc = ctx
    for l in range(depth):
        last = l == depth - 1
        lat = [mods[l, :bsz, i * d:(i + 1) * d][:, None, :] for i in range(6)]
        cm = [jnp.broadcast_to(mods[l, bsz, i * d:(i + 1) * d][None, None, :], (bsz, 1, d))
              for i in range(6)]
        g = norm_g[l]
        w_in_b = w_in[l].astype(BF16)
        w2_b, a2_b = rw_w2[l].astype(BF16), rw_a2[l].astype(BF16)
        prep_params = (rw_mu[l][None], rw_kk[l][None], rw_ka[l][None], rw_w0[l][:, None, :], w2_b,
                       rw_a0[l][:, None, :], a2_b)

        c_a, c_q, c_rg, c_kv, c_rwk = _proj_in(xc, g[0:1], cm[0], cm[1], w_in_b, cos_c, sin_c, False, tc_proj)
        (c_r, c_gd, c_v, c_kk, c_lw0, c_lw1, c_b0, c_b1, c_kd0, c_kd1) = _rwkv_prep(
            c_rg, c_rwk, *prep_params, False, clen)
        c_yf, c_yb, s_ctx = _scan(c_lw0, c_lw1, c_b0, c_b1, c_kd0, c_kd1, c_kk, c_v, c_r, s_zero, nb_scan)

        p_a, p_q, p_rg, p_kv, p_rwk = _proj_in(x, g[0:1], lat[0], lat[1], w_in_b, cos, sin, True, tm_proj)
        (r, gd, v, kk, lw0, lw1, b0, b1, kd0, kd1) = _rwkv_prep(p_rg, p_rwk, *prep_params, True, tm_prep)
        yf, yb, _ = _scan(lw0, lw1, b0, b1, kd0, kd1, kk, v, r, s_ctx, nb_scan)

        sg_w_b = sg_w[l].astype(BF16)
        bias_plane = jnp.repeat(sg_b[l].T, HEAD_DIM, axis=1)
        o_sg = _gmlp(p_a, sg_ln_g[l][None], sg_ln_b[l][None], sg_w_b, bias_plane, tm_gmlp)
        o_att = _attention(p_q, p_kv, c_kv, attn_sink[l], True)

        out_params = (g, rw_lnx_g[l][None], rw_lnx_b[l][None], rw_rk[l].reshape(1, RW_W),
                      rw_g2[l].astype(BF16), w_out[l].astype(BF16), w_gu[l].astype(BF16),
                      w_down[l].astype(BF16))
        x_new = _mix_out_ffn(x, o_sg, o_att, yf, yb, r, kd0, kd1, v, gd,
                             lat[2], lat[3], lat[4], lat[5], *out_params, tm_out)
        if not last:
            oc_sg = _gmlp(c_a, sg_ln_g[l][None], sg_ln_b[l][None], sg_w_b, bias_plane, _row_tile(clen, 512))
            oc_att = _attention(c_q, c_kv, c_kv, attn_sink[l], False)
            xc = _mix_out_ffn(xc, oc_sg, oc_att, c_yf, c_yb, c_r, c_kd0, c_kd1, c_v, c_gd,
                              cm[2], cm[3], cm[4], cm[5], *out_params, tc_out)
        x = x_new
    return x
```

```python
import functools
import math

import jax
import jax.numpy as jnp
from jax import lax
from jax.experimental import pallas as pl
from jax.experimental.pallas import tpu as pltpu

F32 = jnp.float32
BF16 = jnp.bfloat16

D_MODEL = 1024
HEAD_DIM = 64
GRID_W = 64
A_GROUPS = 4
A_W = A_GROUPS * HEAD_DIM
CHUNK = 128
ATT_HEADS = 8
ATT_KV_HEADS = 2
ATT_Q = ATT_HEADS * HEAD_DIM
ATT_KV = ATT_KV_HEADS * HEAD_DIM
WINDOW = 128
QBLOCK = 128
ROPE_THETA = 10000.0
ROPE_FREQS = HEAD_DIM // 4
NEG_INF = -1e30
ATT_QBLOCKS_PER_STEP = 2
ATT_SCALE = HEAD_DIM ** -0.5
RW_HEADS = 4
RW_W = RW_HEADS * HEAD_DIM
W_LORA = 64
A_LORA = 64
G_LORA = 128
N_DIR = 2
GN_EPS = 64e-5
MIX_W = A_W + ATT_Q + RW_W
D_FF = -(-8 * D_MODEL // (3 * 256)) * 256
RMS_EPS = 1e-6
LN_EPS = 1e-5
OFF_Q = 2 * A_W
OFF_R = OFF_Q + ATT_Q
OFF_CTX = OFF_R + RW_W + G_LORA
OFF_RWK = OFF_CTX + 2 * ATT_KV
IN_COLS = OFF_RWK + 2 * RW_W + N_DIR * (W_LORA + A_LORA)
MU_HEAD = RW_W + G_LORA
RWK_W = 2 * RW_W + N_DIR * (W_LORA + A_LORA)

V7X_LANES = 128
V7X_SUBLANES = 8
V7X_VMEM_LIMIT_BYTES = 56 * 1024 * 1024

SCAN_CHUNK = 64
SCAN_CHUNKS_PER_STEP = 2
SCAN_BATCH = 4
LOG_DECAY_SCALE = -math.exp(-0.5)


def _params(n_grid):
    return pltpu.CompilerParams(
        dimension_semantics=("arbitrary",) * n_grid, vmem_limit_bytes=V7X_VMEM_LIMIT_BYTES)


def _dot(a, b):
    return jnp.dot(a.astype(BF16), b.astype(BF16), preferred_element_type=F32)


def _dot_nt(a, b):
    return lax.dot_general(a.astype(BF16), b.astype(BF16), (((1,), (1,)), ((), ())),
                           preferred_element_type=F32)


def _split_bf16(x, n):
    parts = []
    for _ in range(n - 1):
        h = x.astype(BF16)
        parts.append(h)
        x = x - h.astype(F32)
    parts.append(x.astype(BF16))
    return parts


def _dot_exact_rhs(x, m_bf16, n=3):
    acc = None
    for p in _split_bf16(x, n):
        t = jnp.dot(p, m_bf16, preferred_element_type=F32)
        acc = t if acc is None else acc + t
    return acc


def _dot_exact_lhs(m_bf16, x, n=3):
    acc = None
    for p in _split_bf16(x, n):
        t = jnp.dot(m_bf16, p, preferred_element_type=F32)
        acc = t if acc is None else acc + t
    return acc


def _head_sum_matrix(width):
    i = lax.broadcasted_iota(jnp.int32, (width, width), 0) // HEAD_DIM
    j = lax.broadcasted_iota(jnp.int32, (width, width), 1) // HEAD_DIM
    return jnp.where(i == j, 1.0, 0.0).astype(BF16)


def _round_robin(chains):
    while chains:
        chains = [g for g in chains if next(g, True) is None]


def _rms_norm(x, g):
    return x * lax.rsqrt(jnp.mean(x * x, axis=-1, keepdims=True) + RMS_EPS) * g


def _tile_lanes(x, width):
    reps = width // x.shape[-1]
    return x if reps == 1 else jnp.concatenate([x] * reps, axis=-1)


def _rope(x, cos, sin_signed):
    w = x.shape[-1]
    lane = lax.broadcasted_iota(jnp.int32, x.shape, 1)
    lo = (lane % (2 * ROPE_FREQS)) < ROPE_FREQS
    partner = jnp.where(lo, pltpu.roll(x, w - ROPE_FREQS, 1), pltpu.roll(x, ROPE_FREQS, 1))
    return x * _tile_lanes(cos, w) + partner * _tile_lanes(sin_signed, w)


def _mod_kernel(c_ref, w_ref, b_ref, o_ref):
    c = c_ref[...]
    act = c * jax.nn.sigmoid(c)
    o_ref[0] = _dot(act, w_ref[0]) + b_ref[0]


def _modulation(c_rows, w_mod, b_mod):
    depth, d, n = w_mod.shape
    rows = c_rows.shape[0]
    tn = n // 4
    return pl.pallas_call(
        _mod_kernel,
        grid=(depth, n // tn),
        in_specs=[
            pl.BlockSpec((rows, d), lambda l, j: (0, 0)),
            pl.BlockSpec((1, d, tn), lambda l, j: (l, 0, j)),
            pl.BlockSpec((1, 1, tn), lambda l, j: (l, 0, j)),
        ],
        out_specs=pl.BlockSpec((1, rows, tn), lambda l, j: (l, 0, j)),
        out_shape=jax.ShapeDtypeStruct((depth, rows, n), F32),
        compiler_params=_params(2),
        name="modulation",
    )(c_rows, w_mod, b_mod.reshape(depth, 1, n))


def _shift_rows(x, k):
    return pltpu.roll(x, k % x.shape[0], 0)


def _token_shift(cur, prev_rows, next_rows, grid_mode, first, last):
    tm, width = cur.shape
    ch = lax.broadcasted_iota(jnp.int32, (tm, width), 1)
    pos = lax.broadcasted_iota(jnp.int32, (tm, width), 0)
    if grid_mode:
        qw = width // 4
        col = pos % GRID_W
        left = jnp.where(col == 0, 0.0, _shift_rows(cur, 1))
        right = jnp.where(col == GRID_W - 1, 0.0, _shift_rows(cur, -1))
        up = jnp.concatenate([prev_rows, cur[:tm - GRID_W]], axis=0)
        up = jnp.where((pos < GRID_W) & first, 0.0, up)
        down = jnp.concatenate([cur[GRID_W:], next_rows], axis=0)
        down = jnp.where((pos >= tm - GRID_W) & last, 0.0, down)
        return jnp.where(ch < qw, left, jnp.where(ch < 2 * qw, right, jnp.where(ch < 3 * qw, up, down)))
    prev = jnp.where(pos == 0, 0.0, _shift_rows(cur, 1))
    nxt = jnp.where(pos == tm - 1, 0.0, _shift_rows(cur, -1))
    return jnp.where(ch < width // 2, prev, nxt)


def _lane_slab(lo, hi):
    return (lo // V7X_LANES) * V7X_LANES, -(-hi // V7X_LANES) * V7X_LANES


def _halo_columns(hb, w_ref, off, width, quarter):
    qw = width // 4
    lo, hi = _lane_slab(quarter * qw, (quarter + 1) * qw)
    p = jnp.dot(hb, w_ref[:, off + lo:off + hi], preferred_element_type=F32)
    parts = []
    if lo:
        parts.append(jnp.zeros((p.shape[0], lo), F32))
    parts.append(p)
    if hi < width:
        parts.append(jnp.zeros((p.shape[0], width - hi), F32))
    return jnp.concatenate(parts, axis=1)


def _proj_prep_kernel(grid_mode, nblk, *refs):
    if grid_mode:
        x_ref, xp_ref, xn_ref = refs[:3]
        refs = refs[3:]
    else:
        x_ref = refs[0]
        refs = refs[1:]
    (g_ref, sh_ref, sc_ref, w_ref, cos_ref, sin_ref,
     mu_rg_ref, mu_rwk_ref, kkp_ref, ka_ref, w0_ref, w2_ref, a0_ref, a2_ref,
     pa_ref, pq_ref, pkv_ref,
     r_ref, gd_ref, v_ref, kk_ref, lw0_ref, lw1_ref, b0_ref, b1_ref, kd0_ref, kd1_ref) = refs
    i = pl.program_id(1)
    first, last = i == 0, i == nblk - 1

    def normed(x):
        return (_rms_norm(x, g_ref[...]) * (1.0 + sc_ref[0]) + sh_ref[0]).astype(BF16)

    def project(hb, start, width):
        return jnp.dot(hb, w_ref[:, start:start + width], preferred_element_type=F32)

    hb = normed(x_ref[0])
    pa_ref[0] = project(hb, 0, OFF_Q)
    q = project(hb, OFF_Q, ATT_Q)
    kv = project(hb, OFF_CTX, 2 * ATT_KV)
    if grid_mode:
        q = _rope(q, cos_ref[...], sin_ref[...])
        kv = jnp.concatenate([_rope(kv[:, :ATT_KV], cos_ref[...], sin_ref[...]), kv[:, ATT_KV:]], axis=-1)
    pq_ref[0] = q * ATT_SCALE
    pkv_ref[0] = kv
    rg = project(hb, OFF_R, MU_HEAD)
    rwk = project(hb, OFF_RWK, RWK_W)
    if grid_mode:
        hp, hn = normed(xp_ref[0]), normed(xn_ref[0])
        rg_s = _token_shift(rg, _halo_columns(hp, w_ref, OFF_R, MU_HEAD, 2),
                            _halo_columns(hn, w_ref, OFF_R, MU_HEAD, 3), True, first, last)
        rwk_s = _token_shift(rwk, _halo_columns(hp, w_ref, OFF_RWK, RWK_W, 2),
                             _halo_columns(hn, w_ref, OFF_RWK, RWK_W, 3), True, first, last)
    else:
        rg_s = _token_shift(rg, None, None, False, first, last)
        rwk_s = _token_shift(rwk, None, None, False, first, last)
    rg = rg + (rg_s - rg) * mu_rg_ref[...]
    rwk = rwk + (rwk_s - rwk) * mu_rwk_ref[...]
    r_ref[0] = rg[:, :RW_W]
    gd_ref[0] = rg[:, RW_W:]
    k = rwk[:, :RW_W]
    v_ref[0] = rwk[:, RW_W:2 * RW_W]
    kk = k * kkp_ref[...]
    norm = jnp.sqrt(_dot_exact_rhs(kk * kk, _head_sum_matrix(RW_W)))
    kk = kk / jnp.maximum(norm, 1e-12)
    kk_ref[0] = kk
    base = 2 * RW_W
    for d, (lw_ref, b_ref, kd_ref) in enumerate(((lw0_ref, b0_ref, kd0_ref), (lw1_ref, b1_ref, kd1_ref))):
        wd = rwk[:, base + d * W_LORA:base + (d + 1) * W_LORA]
        ad = rwk[:, base + N_DIR * W_LORA + d * A_LORA:base + N_DIR * W_LORA + (d + 1) * A_LORA]
        z = w0_ref[d] + _dot(jnp.tanh(wd), w2_ref[d])
        lw = LOG_DECAY_SCALE * jax.nn.sigmoid(z)
        ci = lax.broadcasted_iota(jnp.int32, (SCAN_CHUNK, SCAN_CHUNK), 0)
        cj = lax.broadcasted_iota(jnp.int32, (SCAN_CHUNK, SCAN_CHUNK), 1)
        cum = jnp.where((ci <= cj) if d else (ci >= cj), 1.0, 0.0).astype(BF16)
        for c in range(lw.shape[0] // SCAN_CHUNK):
            rows = slice(c * SCAN_CHUNK, (c + 1) * SCAN_CHUNK)
            lw_ref[0, rows] = _dot_exact_lhs(cum, lw[rows])
        a = jax.nn.sigmoid(a0_ref[d] + _dot(ad, a2_ref[d]))
        b_ref[0] = kk * a
        kd_ref[0] = k * (1.0 + (a - 1.0) * ka_ref[...])


def _proj_prep(x, g, sh, sc, w_bf16, cos, sin_signed, mu, kkp, ka, w0, w2_bf16, a0, a2_bf16, grid_mode, tm):
    b, t, d = x.shape
    nblk = t // tm
    assert tm % SCAN_CHUNK == 0 and (grid_mode or nblk == 1)
    row = lambda bi, i: (bi, i, 0)
    per_b = lambda bi, i: (bi, 0, 0)
    const2 = lambda bi, i: (0, 0)
    const3 = lambda bi, i: (0, 0, 0)
    in_specs, args = [pl.BlockSpec((1, tm, d), row)], [x]
    if grid_mode:
        per = tm // GRID_W
        nrow = t // GRID_W
        in_specs += [
            pl.BlockSpec((1, GRID_W, d), lambda bi, i: (bi, jnp.maximum(i * per - 1, 0), 0)),
            pl.BlockSpec((1, GRID_W, d), lambda bi, i: (bi, jnp.minimum((i + 1) * per, nrow - 1), 0)),
        ]
        args += [x, x]
    in_specs += [
        pl.BlockSpec((1, d), const2),
        pl.BlockSpec((1, 1, d), per_b),
        pl.BlockSpec((1, 1, d), per_b),
        pl.BlockSpec((d, IN_COLS), const2),
        pl.BlockSpec((tm, V7X_LANES), lambda bi, i: (i, 0)),
        pl.BlockSpec((tm, V7X_LANES), lambda bi, i: (i, 0)),
        pl.BlockSpec((1, MU_HEAD), const2),
        pl.BlockSpec((1, RWK_W), const2),
        pl.BlockSpec((1, RW_W), const2),
        pl.BlockSpec((1, RW_W), const2),
        pl.BlockSpec((N_DIR, 1, RW_W), const3),
        pl.BlockSpec((N_DIR, W_LORA, RW_W), const3),
        pl.BlockSpec((N_DIR, 1, RW_W), const3),
        pl.BlockSpec((N_DIR, A_LORA, RW_W), const3),
    ]
    args += [g, sh, sc, w_bf16, cos, sin_signed, mu[:, :MU_HEAD], mu[:, MU_HEAD:], kkp, ka,
             w0, w2_bf16, a0, a2_bf16]
    widths = [OFF_Q, ATT_Q, 2 * ATT_KV, RW_W, G_LORA] + [RW_W] * 8
    return pl.pallas_call(
        functools.partial(_proj_prep_kernel, grid_mode, nblk),
        grid=(b, nblk),
        in_specs=in_specs,
        out_specs=[pl.BlockSpec((1, tm, w), row) for w in widths],
        out_shape=[jax.ShapeDtypeStruct((b, t, w), F32) for w in widths],
        compiler_params=_params(2),
        name="proj_prep",
    )(*args)


def _gmlp_kernel(z_ref, g_ref, b_ref, w_ref, bias_ref, o_ref):
    z = jax.nn.gelu(z_ref[0], approximate=True)
    u, v = z[:, :A_W], z[:, A_W:]
    mu = jnp.mean(v, axis=-1, keepdims=True)
    var = jnp.mean(jnp.square(v - mu), axis=-1, keepdims=True)
    vn = ((v - mu) * lax.rsqrt(var + LN_EPS) * g_ref[...] + b_ref[...]).astype(BF16)
    bias = bias_ref[...]
    for c in range(z.shape[0] // CHUNK):
        rows = slice(c * CHUNK, (c + 1) * CHUNK)
        mixed = jnp.concatenate(
            [jnp.dot(w_ref[gi], vn[rows, gi * HEAD_DIM:(gi + 1) * HEAD_DIM],
                     preferred_element_type=F32) for gi in range(A_GROUPS)], axis=-1)
        o_ref[0, rows, :] = u[rows] * (mixed + bias)


def _gmlp(z, ln_g, ln_b, w_bf16, bias_plane, tm):
    b, t, _ = z.shape
    row = lambda bi, i: (bi, i, 0)
    const2 = lambda bi, i: (0, 0)
    return pl.pallas_call(
        _gmlp_kernel,
        grid=(b, t // tm),
        in_specs=[
            pl.BlockSpec((1, tm, 2 * A_W), row),
            pl.BlockSpec((1, A_W), const2),
            pl.BlockSpec((1, A_W), const2),
            pl.BlockSpec((A_GROUPS, CHUNK, CHUNK), lambda bi, i: (0, 0, 0)),
            pl.BlockSpec((CHUNK, A_W), const2),
        ],
        out_specs=pl.BlockSpec((1, tm, A_W), row),
        out_shape=jax.ShapeDtypeStruct((b, t, A_W), F32),
        compiler_params=_params(2),
        name="gmlp",
    )(z, ln_g, ln_b, w_bf16, bias_plane)


def _attn_kernel(local, nblk, nq, sink_ref, q_ref, *refs):
    if local:
        kvp_ref, kvc_ref, kvn_ref, ctx_ref, o_ref = refs
    else:
        ctx_ref, o_ref = refs
    step = pl.program_id(1)
    group = ATT_HEADS // ATT_KV_HEADS
    ctx = ctx_ref[0]
    rows = group * QBLOCK
    row_head = lax.broadcasted_iota(jnp.int32, (rows, 1), 0) // QBLOCK
    if local:
        kv_all = jnp.concatenate([kvp_ref[0], kvc_ref[0], kvn_ref[0]], axis=0)
        qi = lax.broadcasted_iota(jnp.int32, (rows, 3 * QBLOCK), 0) % QBLOCK
        kj = lax.broadcasted_iota(jnp.int32, (rows, 3 * QBLOCK), 1)
        blk = kj // QBLOCK
        in_band = (jnp.abs(kj - QBLOCK - qi) <= WINDOW)
    sinks = []
    for g in range(ATT_KV_HEADS):
        sink = jnp.zeros((rows, 1), F32)
        for i in range(group):
            sink = jnp.where(row_head == i, sink_ref[g * group + i], sink)
        sinks.append(sink)
    units = [(j, g) for j in range(nq) for g in range(ATT_KV_HEADS)]
    s_ctx, s_loc, kv_loc = {}, {}, {}
    for j, g in units:
        q = q_ref[0, j * QBLOCK:(j + 1) * QBLOCK]
        qs = jnp.concatenate(
            [q[:, (g * group + i) * HEAD_DIM:(g * group + i + 1) * HEAD_DIM] for i in range(group)],
            axis=0)
        s_ctx[j, g] = _dot_nt(qs, ctx[:, g * HEAD_DIM:(g + 1) * HEAD_DIM])
        if local:
            n = step * nq + j
            valid = in_band & ((blk != 0) | (n > 0)) & ((blk != 2) | (n < nblk - 1))
            kv_loc[j] = kv_all[j * QBLOCK:(j + 3) * QBLOCK]
            s_loc[j, g] = jnp.where(valid, _dot_nt(qs, kv_loc[j][:, g * HEAD_DIM:(g + 1) * HEAD_DIM]), NEG_INF)
    p_ctx, p_loc, denoms = {}, {}, {}
    for u in units:
        g = u[1]
        m = jnp.maximum(jnp.max(s_ctx[u], axis=-1, keepdims=True), sinks[g])
        if local:
            m = jnp.maximum(m, jnp.max(s_loc[u], axis=-1, keepdims=True))
        p = jnp.exp(s_ctx[u] - m)
        denom = jnp.sum(p, axis=-1, keepdims=True) + jnp.exp(sinks[g] - m)
        p_ctx[u] = p.astype(BF16)
        if local:
            p = jnp.exp(s_loc[u] - m)
            denom = denom + jnp.sum(p, axis=-1, keepdims=True)
            p_loc[u] = p.astype(BF16)
        denoms[u] = denom
    for u in units:
        j, g = u
        acc = _dot(p_ctx[u], ctx[:, ATT_KV + g * HEAD_DIM:ATT_KV + (g + 1) * HEAD_DIM])
        if local:
            acc = acc + _dot(p_loc[u], kv_loc[j][:, ATT_KV + g * HEAD_DIM:ATT_KV + (g + 1) * HEAD_DIM])
        out = acc / denoms[u]
        for i in range(group):
            h = g * group + i
            o_ref[0, j * QBLOCK:(j + 1) * QBLOCK, h * HEAD_DIM:(h + 1) * HEAD_DIM] = out[i * QBLOCK:(i + 1) * QBLOCK]


def _attention(q, kv, kv_ctx, sink, local):
    b, t, _ = q.shape
    c = kv_ctx.shape[1]
    nblk = t // QBLOCK
    nq = ATT_QBLOCKS_PER_STEP if nblk % ATT_QBLOCKS_PER_STEP == 0 else 1
    row = lambda bi, i: (bi, i, 0)
    in_specs = [
        pl.BlockSpec(memory_space=pltpu.SMEM),
        pl.BlockSpec((1, nq * QBLOCK, ATT_Q), row),
    ]
    args = [sink, q]
    if local:
        in_specs += [
            pl.BlockSpec((1, QBLOCK, 2 * ATT_KV), lambda bi, i: (bi, jnp.maximum(i * nq - 1, 0), 0)),
            pl.BlockSpec((1, nq * QBLOCK, 2 * ATT_KV), row),
            pl.BlockSpec((1, QBLOCK, 2 * ATT_KV), lambda bi, i: (bi, jnp.minimum((i + 1) * nq, nblk - 1), 0)),
        ]
        args += [kv, kv, kv]
    in_specs.append(pl.BlockSpec((1, c, 2 * ATT_KV), lambda bi, i: (bi, 0, 0)))
    args.append(kv_ctx)
    return pl.pallas_call(
        functools.partial(_attn_kernel, local, nblk, nq),
        grid=(b, nblk // nq),
        in_specs=in_specs,
        out_specs=pl.BlockSpec((1, nq * QBLOCK, ATT_Q), row),
        out_shape=jax.ShapeDtypeStruct((b, t, ATT_Q), F32),
        compiler_params=_params(2),
        name="attention",
    )(*args)


def _scan_direction(reverse, lw_cum, b, kd, kk, v, r, st_ref, y_ref):
    L = SCAN_CHUNK
    ti = lax.broadcasted_iota(jnp.int32, (L, RW_W), 0)
    if reverse:
        lw_before = jnp.where(ti == L - 1, 0.0, pltpu.roll(lw_cum, L - 1, 0))
    else:
        lw_before = jnp.where(ti == 0, 0.0, pltpu.roll(lw_cum, 1, 0))
    lw_tot = lw_cum[0:1] if reverse else lw_cum[L - 1:L]
    e_neg = jnp.exp(-lw_cum)
    e_end = jnp.exp(lw_tot - lw_cum)

    lane = lax.broadcasted_iota(jnp.int32, (L, RW_W), 1)
    si = lane % HEAD_DIM
    strict = (ti < si) if reverse else (ti > si)
    incl = (ti <= si) if reverse else (ti >= si)
    eye = ti == si
    row = lax.broadcasted_iota(jnp.int32, (RW_W, RW_W), 0)
    col = lax.broadcasted_iota(jnp.int32, (RW_W, RW_W), 1)
    blk = (row // HEAD_DIM) == (col // HEAD_DIM)

    def block_diag(x):
        return jnp.where(blk, jnp.concatenate([x] * RW_HEADS, axis=0), 0.0).astype(BF16)

    def head_transpose(x):
        xt = x.T
        return jnp.concatenate([xt[h * HEAD_DIM:(h + 1) * HEAD_DIM] for h in range(RW_HEADS)], axis=1)

    alpha = kk * jnp.exp(lw_before)
    rho = r * jnp.exp(lw_cum)
    beta_end_t = head_transpose(b * e_end)
    kappa_end_t = head_transpose(kd * e_end)
    v_bd = block_diag(v)
    p1 = _dot_nt(jnp.concatenate([alpha, rho], axis=0),
                 jnp.concatenate([block_diag(b * e_neg), block_diag(kd * e_neg)], axis=0))
    yield
    nn = jnp.where(strict, p1[:L, :RW_W], 0.0)
    ak = jnp.where(strict, p1[:L, RW_W:], 0.0)
    rb = jnp.where(incl, p1[L:, :RW_W], 0.0)
    rk = jnp.where(incl, p1[L:, RW_W:], 0.0)
    tinv = jnp.where(eye, 1.0, 0.0) - nn
    npow = _dot(nn, block_diag(nn))
    kv = _dot(jnp.concatenate([ak, rk, kappa_end_t], axis=0), v_bd)
    akv, rkv, kev = kv[:L], kv[L:2 * L], kv[2 * L:]
    yield
    for _ in range(int(math.log2(L)) - 2):
        both = _dot(jnp.concatenate([npow, tinv], axis=0), block_diag(npow))
        npow, tinv = both[:L], tinv + both[L:]
        yield
    tinv = tinv + _dot(tinv, block_diag(npow))
    yield
    z = _dot(tinv, jnp.concatenate([block_diag(alpha), block_diag(-akv)], axis=1))
    yield
    t_alpha, x0 = z[:, :RW_W], z[:, RW_W:]
    bz = _dot(jnp.concatenate([rb, beta_end_t], axis=0),
              jnp.concatenate([block_diag(t_alpha), block_diag(x0)], axis=1))
    q_mat = rho - bz[:L, :RW_W]
    y0 = bz[:L, RW_W:] + rkv
    m_mat = jnp.where(eye, jnp.exp(lw_tot), 0.0) - bz[L:, :RW_W]
    c_mat = bz[L:, RW_W:] + kev
    yield
    both = _dot(jnp.concatenate([q_mat, m_mat], axis=0), block_diag(st_ref[...]))
    y_ref[...] = both[:L] + y0
    st_ref[...] = both[L:] + c_mat


def _scan_kernel(nchunk, nbatch, nsub, lw0_ref, b0_ref, kd0_ref, kkf_ref, vf_ref, rf_ref,
                 lw1_ref, b1_ref, kd1_ref, kkb_ref, vb_ref, rb_ref, s0_ref,
                 yf_ref, yb_ref, sT_ref, st_ref):
    c = pl.program_id(1)

    @pl.when(c == 0)
    def _():
        st_ref[...] = s0_ref[...]

    def chain(reverse, bi, refs, y_ref):
        order = range(nsub - 1, -1, -1) if reverse else range(nsub)
        for sub in order:
            rows = pl.ds(sub * SCAN_CHUNK, SCAN_CHUNK)
            yield from _scan_direction(reverse, *[ref[bi, rows] for ref in refs],
                                       st_ref.at[bi, int(reverse)], y_ref.at[bi, rows])

    chains = []
    for bi in range(nbatch):
        chains.append(chain(False, bi, (lw0_ref, b0_ref, kd0_ref, kkf_ref, vf_ref, rf_ref), yf_ref))
        chains.append(chain(True, bi, (lw1_ref, b1_ref, kd1_ref, kkb_ref, vb_ref, rb_ref), yb_ref))
    _round_robin(chains)

    @pl.when(c == nchunk - 1)
    def _():
        sT_ref[...] = st_ref[...]


def _scan(lw0, lw1, b0, b1, kd0, kd1, kk, v, r, s0, nbatch):
    assert SCAN_CHUNK == HEAD_DIM
    bsz, t, _ = kk.shape
    nsub = SCAN_CHUNKS_PER_STEP if (t // SCAN_CHUNK) % SCAN_CHUNKS_PER_STEP == 0 else 1
    nchunk = t // (SCAN_CHUNK * nsub)
    fwd = lambda bi, c: (bi, c, 0)
    bwd = lambda bi, c: (bi, nchunk - 1 - c, 0)
    blk = (nbatch, nsub * SCAN_CHUNK, RW_W)
    st_blk = (nbatch, N_DIR, HEAD_DIM, RW_W)
    st_map = lambda bi, c: (bi, 0, 0, 0)
    return pl.pallas_call(
        functools.partial(_scan_kernel, nchunk, nbatch, nsub),
        grid=(bsz // nbatch, nchunk),
        in_specs=[pl.BlockSpec(blk, fwd)] * 6 + [pl.BlockSpec(blk, bwd)] * 6
        + [pl.BlockSpec(st_blk, st_map)],
        out_specs=[pl.BlockSpec(blk, fwd), pl.BlockSpec(blk, bwd), pl.BlockSpec(st_blk, st_map)],
        out_shape=[jax.ShapeDtypeStruct((bsz, t, RW_W), F32),
                   jax.ShapeDtypeStruct((bsz, t, RW_W), F32),
                   jax.ShapeDtypeStruct(s0.shape, F32)],
        scratch_shapes=[pltpu.VMEM(st_blk, F32)],
        compiler_params=_params(2),
        name="rwkv_scan",
    )(lw0, b0, kd0, kk, v, r, lw1, b1, kd1, kk, v, r, s0)


FFN_SPLIT = 2
OUT_SUBTILES = 2


def _out_rows(rows, x_ref, sg_ref, att_ref, yf_ref, yb_ref, r_ref, kd0_ref, kd1_ref, v_ref, gd_ref,
              gt1_ref, sh2_ref, sc2_ref, gt2_ref, ng_ref, lng_ref, lnb_ref, rk_ref, g2_ref,
              wo_ref, wgu_ref, wd_ref, o_ref):
    hsum = _head_sum_matrix(RW_W)
    y = yf_ref[0, rows] + yb_ref[0, rows]
    mu = _dot_exact_rhs(y, hsum) * (1.0 / HEAD_DIM)
    bonus = _dot_exact_rhs(r_ref[0, rows] * (kd0_ref[0, rows] + kd1_ref[0, rows]) * rk_ref[...], hsum) * v_ref[0, rows]
    gate = _dot(jax.nn.sigmoid(gd_ref[0, rows]), g2_ref[...])
    y_mix = _dot(sg_ref[0, rows], wo_ref[:A_W]) + _dot(att_ref[0, rows], wo_ref[A_W:A_W + ATT_Q])
    yield
    yc = y - mu
    var = _dot_exact_rhs(yc * yc, hsum) * (1.0 / HEAD_DIM)
    yield
    yn = yc * lax.rsqrt(var + GN_EPS) * lng_ref[...] + lnb_ref[...]
    o_rw = (yn + bonus) * gate
    y_mix = y_mix + _dot(o_rw, wo_ref[A_W + ATT_Q:])
    yield
    x1 = x_ref[0, rows] + gt1_ref[0] * _rms_norm(y_mix, ng_ref[1:2])
    h2 = (_rms_norm(x1, ng_ref[2:3]) * (1.0 + sc2_ref[0]) + sh2_ref[0]).astype(BF16)
    fc = D_FF // FFN_SPLIT
    acc = None
    for j in range(FFN_SPLIT):
        gg = jnp.dot(h2, wgu_ref[:, j * fc:(j + 1) * fc], preferred_element_type=F32)
        uu = jnp.dot(h2, wgu_ref[:, D_FF + j * fc:D_FF + (j + 1) * fc], preferred_element_type=F32)
        yield
        part = _dot(gg * jax.nn.sigmoid(gg) * uu, wd_ref[j * fc:(j + 1) * fc])
        acc = part if acc is None else acc + part
        yield
    o_ref[0, rows] = x1 + gt2_ref[0] * _rms_norm(acc, ng_ref[3:4])


def _out_kernel(nsub, *refs):
    tm = refs[0].shape[1]
    sub = tm // nsub
    _round_robin([_out_rows(pl.ds(i * sub, sub), *refs) for i in range(nsub)])


def _mix_out_ffn(x, o_sg, o_att, yf, yb, r, kd0, kd1, v, gd, gt1, sh2, sc2, gt2, norm_g,
                 lnx_g, lnx_b, rk, g2_bf16, wo_bf16, wgu_bf16, wd_bf16, tm):
    b, t, d = x.shape
    row = lambda bi, i: (bi, i, 0)
    per_b = lambda bi, i: (bi, 0, 0)
    const2 = lambda bi, i: (0, 0)
    single = pl.Buffered(1)
    in_specs = [
        pl.BlockSpec((1, tm, d), row),
        pl.BlockSpec((1, tm, A_W), row),
        pl.BlockSpec((1, tm, ATT_Q), row),
    ] + [pl.BlockSpec((1, tm, RW_W), row)] * 6 + [
        pl.BlockSpec((1, tm, G_LORA), row),
    ] + [pl.BlockSpec((1, 1, d), per_b)] * 4 + [
        pl.BlockSpec((4, d), const2),
        pl.BlockSpec((1, RW_W), const2),
        pl.BlockSpec((1, RW_W), const2),
        pl.BlockSpec((1, RW_W), const2),
        pl.BlockSpec((G_LORA, RW_W), const2),
        pl.BlockSpec((MIX_W, d), const2, pipeline_mode=single),
        pl.BlockSpec((d, 2 * D_FF), const2, pipeline_mode=single),
        pl.BlockSpec((D_FF, d), const2, pipeline_mode=single),
    ]
    return pl.pallas_call(
        functools.partial(_out_kernel, OUT_SUBTILES if tm % (OUT_SUBTILES * V7X_SUBLANES) == 0 else 1),
        grid=(b, t // tm),
        in_specs=in_specs,
        out_specs=pl.BlockSpec((1, tm, d), row),
        out_shape=jax.ShapeDtypeStruct((b, t, d), F32),
        compiler_params=_params(2),
        name="mix_out_ffn",
    )(x, o_sg, o_att, yf, yb, r, kd0, kd1, v, gd, gt1, sh2, sc2, gt2, norm_g,
      lnx_g, lnx_b, rk, g2_bf16, wo_bf16, wgu_bf16, wd_bf16)


def _rope_tables(t):
    pos = jnp.arange(t)
    inv = ROPE_THETA ** (-jnp.arange(ROPE_FREQS, dtype=F32) / ROPE_FREQS)
    ang_r = (pos // GRID_W).astype(F32)[:, None] * inv
    ang_c = (pos % GRID_W).astype(F32)[:, None] * inv
    cos = jnp.concatenate([jnp.cos(ang_r)] * 2 + [jnp.cos(ang_c)] * 2, axis=-1)
    sin = jnp.concatenate([-jnp.sin(ang_r), jnp.sin(ang_r), -jnp.sin(ang_c), jnp.sin(ang_c)], axis=-1)
    return jnp.concatenate([cos, cos], axis=-1), jnp.concatenate([sin, sin], axis=-1)


def _row_tile(t, target):
    tm = min(t, target)
    while t % tm:
        tm //= 2
    return tm


def kernel(x, c, ctx, c_ctx, w_mod, b_mod, norm_g, w_in, w_out, sg_ln_g, sg_ln_b, sg_w, sg_b,
           attn_sink, rw_mu, rw_w0, rw_w2, rw_a0, rw_a2, rw_kk, rw_ka, rw_rk, rw_g2,
           rw_lnx_g, rw_lnx_b, w_gu, w_down):
    bsz, t, d = x.shape
    clen = ctx.shape[1]
    depth = w_mod.shape[0]

    rows = -(-(bsz + 1) // V7X_SUBLANES) * V7X_SUBLANES
    c_rows = jnp.zeros((rows, d), F32).at[:bsz].set(c).at[bsz].set(c_ctx)
    mods = _modulation(c_rows, w_mod, b_mod)

    cos, sin = _rope_tables(t)
    cos_c, sin_c = cos[:clen], sin[:clen]
    s_zero = jnp.zeros((bsz, N_DIR, HEAD_DIM, RW_W), F32)

    tm_proj = _row_tile(t, 512)
    tm_gmlp = _row_tile(t, 512)
    tm_out = _row_tile(t, 512)
    tc_out = _row_tile(clen, 256)
    nb_scan = SCAN_BATCH if bsz % SCAN_BATCH == 0 else 1

    xc = ctx
    for l in range(depth):
        last = l == depth - 1
        lat = [mods[l, :bsz, i * d:(i + 1) * d][:, None, :] for i in range(6)]
        cm = [jnp.broadcast_to(mods[l, bsz, i * d:(i + 1) * d][None, None, :], (bsz, 1, d))
              for i in range(6)]
        g = norm_g[l]
        w_in_b = w_in[l].astype(BF16)
        w2_b, a2_b = rw_w2[l].astype(BF16), rw_a2[l].astype(BF16)
        prep_params = (rw_mu[l][None], rw_kk[l][None], rw_ka[l][None], rw_w0[l][:, None, :], w2_b,
                       rw_a0[l][:, None, :], a2_b)

        (c_a, c_q, c_kv, c_r, c_gd, c_v, c_kk, c_lw0, c_lw1, c_b0, c_b1, c_kd0, c_kd1) = _proj_prep(
            xc, g[0:1], cm[0], cm[1], w_in_b, cos_c, sin_c, *prep_params, False, clen)
        c_yf, c_yb, s_ctx = _scan(c_lw0, c_lw1, c_b0, c_b1, c_kd0, c_kd1, c_kk, c_v, c_r, s_zero, nb_scan)

        (p_a, p_q, p_kv, r, gd, v, kk, lw0, lw1, b0, b1, kd0, kd1) = _proj_prep(
            x, g[0:1], lat[0], lat[1], w_in_b, cos, sin, *prep_params, True, tm_proj)
        yf, yb, _ = _scan(lw0, lw1, b0, b1, kd0, kd1, kk, v, r, s_ctx, nb_scan)

        sg_w_b = sg_w[l].astype(BF16)
        bias_plane = jnp.repeat(sg_b[l].T, HEAD_DIM, axis=1)
        o_sg = _gmlp(p_a, sg_ln_g[l][None], sg_ln_b[l][None], sg_w_b, bias_plane, tm_gmlp)
        o_att = _attention(p_q, p_kv, c_kv, attn_sink[l], True)

        out_params = (g, rw_lnx_g[l][None], rw_lnx_b[l][None], rw_rk[l].reshape(1, RW_W),
                      rw_g2[l].astype(BF16), w_out[l].astype(BF16), w_gu[l].astype(BF16),
                      w_down[l].astype(BF16))
        x_new = _mix_out_ffn(x, o_sg, o_att, yf, yb, r, kd0, kd1, v, gd,
                             lat[2], lat[3], lat[4], lat[5], *out_params, tm_out)
        if not last:
            oc_sg = _gmlp(c_a, sg_ln_g[l][None], sg_ln_b[l][None], sg_w_b, bias_plane, _row_tile(clen, 512))
            oc_att = _attention(c_q, c_kv, c_kv, attn_sink[l], False)
            xc = _mix_out_ffn(xc, oc_sg, oc_att, c_yf, c_yb, c_r, c_kd0, c_kd1, c_v, c_gd,
                              cm[2], cm[3], cm[4], cm[5], *out_params, tc_out)
        x = x_new
    return x
```

```python
import functools
import math

import jax
import jax.numpy as jnp
import numpy as np
from jax import lax
from jax.experimental import pallas as pl
from jax.experimental.pallas import tpu as pltpu

F32 = jnp.float32
BF16 = jnp.bfloat16

D_MODEL = 1024
HEAD_DIM = 64
GRID_W = 64
A_GROUPS = 4
A_W = A_GROUPS * HEAD_DIM
CHUNK = 128
ATT_HEADS = 8
ATT_KV_HEADS = 2
ATT_Q = ATT_HEADS * HEAD_DIM
ATT_KV = ATT_KV_HEADS * HEAD_DIM
WINDOW = 128
QBLOCK = 128
ROPE_THETA = 10000.0
ROPE_FREQS = HEAD_DIM // 4
NEG_INF = -1e30
ATT_QBLOCKS_PER_STEP = 4
ATT_SCALE = HEAD_DIM ** -0.5
RW_HEADS = 4
RW_W = RW_HEADS * HEAD_DIM
W_LORA = 64
A_LORA = 64
G_LORA = 128
N_DIR = 2
GN_EPS = 64e-5
MIX_W = A_W + ATT_Q + RW_W
D_FF = -(-8 * D_MODEL // (3 * 256)) * 256
RMS_EPS = 1e-6
LN_EPS = 1e-5
OFF_Q = 2 * A_W
OFF_R = OFF_Q + ATT_Q
OFF_CTX = OFF_R + RW_W + G_LORA
OFF_RWK = OFF_CTX + 2 * ATT_KV
IN_COLS = OFF_RWK + 2 * RW_W + N_DIR * (W_LORA + A_LORA)
MU_HEAD = RW_W + G_LORA
RWK_W = 2 * RW_W + N_DIR * (W_LORA + A_LORA)

V7X_LANES = 128
V7X_SUBLANES = 8
V7X_VMEM_LIMIT_BYTES = 56 * 1024 * 1024

SCAN_CHUNK = 64
SCAN_CHUNKS_PER_STEP = 4
SCAN_BATCH = 4
LOG_DECAY_SCALE = -math.exp(-0.5)


def _params(n_grid):
    return pltpu.CompilerParams(
        dimension_semantics=("arbitrary",) * n_grid, vmem_limit_bytes=V7X_VMEM_LIMIT_BYTES)


def _dot(a, b):
    return jnp.dot(a.astype(BF16), b.astype(BF16), preferred_element_type=F32)


def _dot_nt(a, b):
    return lax.dot_general(a.astype(BF16), b.astype(BF16), (((1,), (1,)), ((), ())),
                           preferred_element_type=F32)


def _split_bf16(x, n):
    parts = []
    for _ in range(n - 1):
        h = x.astype(BF16)
        parts.append(h)
        x = x - h.astype(F32)
    parts.append(x.astype(BF16))
    return parts


def _dot_exact_rhs(x, m_bf16, n=3):
    acc = None
    for p in _split_bf16(x, n):
        t = jnp.dot(p, m_bf16, preferred_element_type=F32)
        acc = t if acc is None else acc + t
    return acc


def _dot_exact_lhs(m_bf16, x, n=3):
    acc = None
    for p in _split_bf16(x, n):
        t = jnp.dot(m_bf16, p, preferred_element_type=F32)
        acc = t if acc is None else acc + t
    return acc


def _head_sum_matrix(width):
    i = lax.broadcasted_iota(jnp.int32, (width, width), 0) // HEAD_DIM
    j = lax.broadcasted_iota(jnp.int32, (width, width), 1) // HEAD_DIM
    return jnp.where(i == j, 1.0, 0.0).astype(BF16)


def _round_robin(chains):
    while chains:
        chains = [g for g in chains if next(g, True) is None]


def _rms_norm(x, g):
    return x * lax.rsqrt(jnp.mean(x * x, axis=-1, keepdims=True) + RMS_EPS) * g


def _tile_lanes(x, width):
    reps = width // x.shape[-1]
    return x if reps == 1 else jnp.concatenate([x] * reps, axis=-1)


def _rope(x, cos, sin_signed):
    w = x.shape[-1]
    lane = lax.broadcasted_iota(jnp.int32, x.shape, 1)
    lo = (lane % (2 * ROPE_FREQS)) < ROPE_FREQS
    partner = jnp.where(lo, pltpu.roll(x, w - ROPE_FREQS, 1), pltpu.roll(x, ROPE_FREQS, 1))
    return x * _tile_lanes(cos, w) + partner * _tile_lanes(sin_signed, w)


def _mod_kernel(c_ref, w_ref, b_ref, o_ref):
    c = c_ref[...]
    act = c * jax.nn.sigmoid(c)
    o_ref[0] = _dot(act, w_ref[0]) + b_ref[0]


def _modulation(c_rows, w_mod, b_mod):
    depth, d, n = w_mod.shape
    rows = c_rows.shape[0]
    tn = n // 4
    return pl.pallas_call(
        _mod_kernel,
        grid=(depth, n // tn),
        in_specs=[
            pl.BlockSpec((rows, d), lambda l, j: (0, 0)),
            pl.BlockSpec((1, d, tn), lambda l, j: (l, 0, j)),
            pl.BlockSpec((1, 1, tn), lambda l, j: (l, 0, j)),
        ],
        out_specs=pl.BlockSpec((1, rows, tn), lambda l, j: (l, 0, j)),
        out_shape=jax.ShapeDtypeStruct((depth, rows, n), F32),
        compiler_params=_params(2),
        name="modulation",
    )(c_rows, w_mod, b_mod.reshape(depth, 1, n))


def _shift_rows(x, k):
    return pltpu.roll(x, k % x.shape[0], 0)


def _token_shift(cur, prev_rows, next_rows, grid_mode, first, last):
    tm, width = cur.shape
    ch = lax.broadcasted_iota(jnp.int32, (tm, width), 1)
    pos = lax.broadcasted_iota(jnp.int32, (tm, width), 0)
    if grid_mode:
        qw = width // 4
        col = pos % GRID_W
        left = jnp.where(col == 0, 0.0, _shift_rows(cur, 1))
        right = jnp.where(col == GRID_W - 1, 0.0, _shift_rows(cur, -1))
        up = jnp.concatenate([prev_rows, cur[:tm - GRID_W]], axis=0)
        up = jnp.where((pos < GRID_W) & first, 0.0, up)
        down = jnp.concatenate([cur[GRID_W:], next_rows], axis=0)
        down = jnp.where((pos >= tm - GRID_W) & last, 0.0, down)
        return jnp.where(ch < qw, left, jnp.where(ch < 2 * qw, right, jnp.where(ch < 3 * qw, up, down)))
    prev = jnp.where(pos == 0, 0.0, _shift_rows(cur, 1))
    nxt = jnp.where(pos == tm - 1, 0.0, _shift_rows(cur, -1))
    return jnp.where(ch < width // 2, prev, nxt)


def _lane_slab(lo, hi):
    return (lo // V7X_LANES) * V7X_LANES, -(-hi // V7X_LANES) * V7X_LANES


def _halo_columns(hb, w_ref, off, width, quarter):
    qw = width // 4
    lo, hi = _lane_slab(quarter * qw, (quarter + 1) * qw)
    p = jnp.dot(hb, w_ref[:, off + lo:off + hi], preferred_element_type=F32)
    parts = []
    if lo:
        parts.append(jnp.zeros((p.shape[0], lo), F32))
    parts.append(p)
    if hi < width:
        parts.append(jnp.zeros((p.shape[0], width - hi), F32))
    return jnp.concatenate(parts, axis=1)


def _proj_prep_kernel(grid_mode, nblk, *refs):
    if grid_mode:
        x_ref, xp_ref, xn_ref = refs[:3]
        refs = refs[3:]
    else:
        x_ref = refs[0]
        refs = refs[1:]
    (g_ref, sh_ref, sc_ref, w_ref, cos_ref, sin_ref,
     mu_rg_ref, mu_rwk_ref, kkp_ref, ka_ref, w0_ref, w2_ref, a0_ref, a2_ref,
     pa_ref, pq_ref, pkv_ref,
     r_ref, gd_ref, v_ref, kk_ref, lw0_ref, lw1_ref, b0_ref, b1_ref, kd0_ref, kd1_ref) = refs
    i = pl.program_id(1)
    first, last = i == 0, i == nblk - 1

    def normed(x):
        return (_rms_norm(x, g_ref[...]) * (1.0 + sc_ref[0]) + sh_ref[0]).astype(BF16)

    def project(hb, start, width):
        return jnp.dot(hb, w_ref[:, start:start + width], preferred_element_type=F32)

    hb = normed(x_ref[0])
    pa_ref[0] = project(hb, 0, OFF_Q)
    q = project(hb, OFF_Q, ATT_Q)
    kv = project(hb, OFF_CTX, 2 * ATT_KV)
    if grid_mode:
        q = _rope(q, cos_ref[...], sin_ref[...])
        kv = jnp.concatenate([_rope(kv[:, :ATT_KV], cos_ref[...], sin_ref[...]), kv[:, ATT_KV:]], axis=-1)
    pq_ref[0] = q * ATT_SCALE
    pkv_ref[0] = kv
    rg = project(hb, OFF_R, MU_HEAD)
    rwk = project(hb, OFF_RWK, RWK_W)
    if grid_mode:
        hp, hn = normed(xp_ref[0]), normed(xn_ref[0])
        rg_s = _token_shift(rg, _halo_columns(hp, w_ref, OFF_R, MU_HEAD, 2),
                            _halo_columns(hn, w_ref, OFF_R, MU_HEAD, 3), True, first, last)
        rwk_s = _token_shift(rwk, _halo_columns(hp, w_ref, OFF_RWK, RWK_W, 2),
                             _halo_columns(hn, w_ref, OFF_RWK, RWK_W, 3), True, first, last)
    else:
        rg_s = _token_shift(rg, None, None, False, first, last)
        rwk_s = _token_shift(rwk, None, None, False, first, last)
    rg = rg + (rg_s - rg) * mu_rg_ref[...]
    rwk = rwk + (rwk_s - rwk) * mu_rwk_ref[...]
    r_ref[0] = rg[:, :RW_W]
    gd_ref[0] = rg[:, RW_W:]
    k = rwk[:, :RW_W]
    v_ref[0] = rwk[:, RW_W:2 * RW_W]
    kk = k * kkp_ref[...]
    norm = jnp.sqrt(_dot_exact_rhs(kk * kk, _head_sum_matrix(RW_W)))
    kk = kk / jnp.maximum(norm, 1e-12)
    kk_ref[0] = kk
    base = 2 * RW_W
    for d, (lw_ref, b_ref, kd_ref) in enumerate(((lw0_ref, b0_ref, kd0_ref), (lw1_ref, b1_ref, kd1_ref))):
        wd = rwk[:, base + d * W_LORA:base + (d + 1) * W_LORA]
        ad = rwk[:, base + N_DIR * W_LORA + d * A_LORA:base + N_DIR * W_LORA + (d + 1) * A_LORA]
        z = w0_ref[d] + _dot(jnp.tanh(wd), w2_ref[d])
        lw = LOG_DECAY_SCALE * jax.nn.sigmoid(z)
        ci = lax.broadcasted_iota(jnp.int32, (SCAN_CHUNK, SCAN_CHUNK), 0)
        cj = lax.broadcasted_iota(jnp.int32, (SCAN_CHUNK, SCAN_CHUNK), 1)
        cum = jnp.where((ci <= cj) if d else (ci >= cj), 1.0, 0.0).astype(BF16)
        for c in range(lw.shape[0] // SCAN_CHUNK):
            rows = slice(c * SCAN_CHUNK, (c + 1) * SCAN_CHUNK)
            lw_ref[0, rows] = _dot_exact_lhs(cum, lw[rows])
        a = jax.nn.sigmoid(a0_ref[d] + _dot(ad, a2_ref[d]))
        b_ref[0] = kk * a
        kd_ref[0] = k * (1.0 + (a - 1.0) * ka_ref[...])


def _proj_prep(x, g, sh, sc, w_bf16, cos, sin_signed, mu, kkp, ka, w0, w2_bf16, a0, a2_bf16, grid_mode, tm):
    b, t, d = x.shape
    nblk = t // tm
    assert tm % SCAN_CHUNK == 0 and (grid_mode or nblk == 1)
    row = lambda bi, i: (bi, i, 0)
    per_b = lambda bi, i: (bi, 0, 0)
    const2 = lambda bi, i: (0, 0)
    const3 = lambda bi, i: (0, 0, 0)
    in_specs, args = [pl.BlockSpec((1, tm, d), row)], [x]
    if grid_mode:
        per = tm // GRID_W
        nrow = t // GRID_W
        in_specs += [
            pl.BlockSpec((1, GRID_W, d), lambda bi, i: (bi, jnp.maximum(i * per - 1, 0), 0)),
            pl.BlockSpec((1, GRID_W, d), lambda bi, i: (bi, jnp.minimum((i + 1) * per, nrow - 1), 0)),
        ]
        args += [x, x]
    in_specs += [
        pl.BlockSpec((1, d), const2),
        pl.BlockSpec((1, 1, d), per_b),
        pl.BlockSpec((1, 1, d), per_b),
        pl.BlockSpec((d, IN_COLS), const2),
        pl.BlockSpec((tm, V7X_LANES), lambda bi, i: (i, 0)),
        pl.BlockSpec((tm, V7X_LANES), lambda bi, i: (i, 0)),
        pl.BlockSpec((1, MU_HEAD), const2),
        pl.BlockSpec((1, RWK_W), const2),
        pl.BlockSpec((1, RW_W), const2),
        pl.BlockSpec((1, RW_W), const2),
        pl.BlockSpec((N_DIR, 1, RW_W), const3),
        pl.BlockSpec((N_DIR, W_LORA, RW_W), const3),
        pl.BlockSpec((N_DIR, 1, RW_W), const3),
        pl.BlockSpec((N_DIR, A_LORA, RW_W), const3),
    ]
    args += [g, sh, sc, w_bf16, cos, sin_signed, mu[:, :MU_HEAD], mu[:, MU_HEAD:], kkp, ka,
             w0, w2_bf16, a0, a2_bf16]
    widths = [OFF_Q, ATT_Q, 2 * ATT_KV, RW_W, G_LORA] + [RW_W] * 8
    return pl.pallas_call(
        functools.partial(_proj_prep_kernel, grid_mode, nblk),
        grid=(b, nblk),
        in_specs=in_specs,
        out_specs=[pl.BlockSpec((1, tm, w), row) for w in widths],
        out_shape=[jax.ShapeDtypeStruct((b, t, w), F32) for w in widths],
        compiler_params=_params(2),
        name="proj_prep",
    )(*args)


def _gmlp_kernel(z_ref, g_ref, b_ref, w_ref, bias_ref, o_ref):
    z = jax.nn.gelu(z_ref[0], approximate=True)
    u, v = z[:, :A_W], z[:, A_W:]
    mu = jnp.mean(v, axis=-1, keepdims=True)
    var = jnp.mean(jnp.square(v - mu), axis=-1, keepdims=True)
    vn = ((v - mu) * lax.rsqrt(var + LN_EPS) * g_ref[...] + b_ref[...]).astype(BF16)
    bias = bias_ref[...]
    for c in range(z.shape[0] // CHUNK):
        rows = slice(c * CHUNK, (c + 1) * CHUNK)
        mixed = jnp.concatenate(
            [jnp.dot(w_ref[gi], vn[rows, gi * HEAD_DIM:(gi + 1) * HEAD_DIM],
                     preferred_element_type=F32) for gi in range(A_GROUPS)], axis=-1)
        o_ref[0, rows, :] = u[rows] * (mixed + bias)


def _gmlp(z, ln_g, ln_b, w_bf16, bias_plane, tm):
    b, t, _ = z.shape
    row = lambda bi, i: (bi, i, 0)
    const2 = lambda bi, i: (0, 0)
    return pl.pallas_call(
        _gmlp_kernel,
        grid=(b, t // tm),
        in_specs=[
            pl.BlockSpec((1, tm, 2 * A_W), row),
            pl.BlockSpec((1, A_W), const2),
            pl.BlockSpec((1, A_W), const2),
            pl.BlockSpec((A_GROUPS, CHUNK, CHUNK), lambda bi, i: (0, 0, 0)),
            pl.BlockSpec((CHUNK, A_W), const2),
        ],
        out_specs=pl.BlockSpec((1, tm, A_W), row),
        out_shape=jax.ShapeDtypeStruct((b, t, A_W), F32),
        compiler_params=_params(2),
        name="gmlp",
    )(z, ln_g, ln_b, w_bf16, bias_plane)


def _attn_kernel(local, nblk, nq, sink_ref, q_ref, *refs):
    if local:
        kvp_ref, kvc_ref, kvn_ref, ctx_ref, o_ref = refs
    else:
        ctx_ref, o_ref = refs
    step = pl.program_id(1)
    group = ATT_HEADS // ATT_KV_HEADS
    ctx = ctx_ref[0]
    rows = group * QBLOCK
    row_head = lax.broadcasted_iota(jnp.int32, (rows, 1), 0) // QBLOCK
    if local:
        kv_all = jnp.concatenate([kvp_ref[0], kvc_ref[0], kvn_ref[0]], axis=0)
        qi = lax.broadcasted_iota(jnp.int32, (rows, 3 * QBLOCK), 0) % QBLOCK
        kj = lax.broadcasted_iota(jnp.int32, (rows, 3 * QBLOCK), 1)
        blk = kj // QBLOCK
        in_band = (jnp.abs(kj - QBLOCK - qi) <= WINDOW)
    sinks = []
    for g in range(ATT_KV_HEADS):
        sink = jnp.zeros((rows, 1), F32)
        for i in range(group):
            sink = jnp.where(row_head == i, sink_ref[g * group + i], sink)
        sinks.append(sink)
    units = [(j, g) for j in range(nq) for g in range(ATT_KV_HEADS)]
    s_ctx, s_loc, kv_loc = {}, {}, {}
    for j, g in units:
        q = q_ref[0, j * QBLOCK:(j + 1) * QBLOCK]
        qs = jnp.concatenate(
            [q[:, (g * group + i) * HEAD_DIM:(g * group + i + 1) * HEAD_DIM] for i in range(group)],
            axis=0)
        s_ctx[j, g] = _dot_nt(qs, ctx[:, g * HEAD_DIM:(g + 1) * HEAD_DIM])
        if local:
            n = step * nq + j
            valid = in_band & ((blk != 0) | (n > 0)) & ((blk != 2) | (n < nblk - 1))
            kv_loc[j] = kv_all[j * QBLOCK:(j + 3) * QBLOCK]
            s_loc[j, g] = jnp.where(valid, _dot_nt(qs, kv_loc[j][:, g * HEAD_DIM:(g + 1) * HEAD_DIM]), NEG_INF)
    p_ctx, p_loc, p_sink = {}, {}, {}
    for u in units:
        g = u[1]
        m = jnp.maximum(jnp.max(s_ctx[u], axis=-1, keepdims=True), sinks[g])
        if local:
            m = jnp.maximum(m, jnp.max(s_loc[u], axis=-1, keepdims=True))
        p_ctx[u] = jnp.exp(s_ctx[u] - m).astype(BF16)
        if local:
            p_loc[u] = jnp.exp(s_loc[u] - m).astype(BF16)
        p_sink[u] = jnp.exp(sinks[g] - m)

    def with_ones(v):
        return jnp.concatenate([v.astype(BF16), jnp.ones(v.shape, BF16)], axis=1)

    for u in units:
        j, g = u
        acc = _dot(p_ctx[u], with_ones(ctx[:, ATT_KV + g * HEAD_DIM:ATT_KV + (g + 1) * HEAD_DIM]))
        if local:
            acc = acc + _dot(p_loc[u], with_ones(kv_loc[j][:, ATT_KV + g * HEAD_DIM:ATT_KV + (g + 1) * HEAD_DIM]))
        out = acc[:, :HEAD_DIM] / (acc[:, HEAD_DIM:] + p_sink[u])
        for i in range(group):
            h = g * group + i
            o_ref[0, j * QBLOCK:(j + 1) * QBLOCK, h * HEAD_DIM:(h + 1) * HEAD_DIM] = out[i * QBLOCK:(i + 1) * QBLOCK]


def _attention(q, kv, kv_ctx, sink, local):
    b, t, _ = q.shape
    c = kv_ctx.shape[1]
    nblk = t // QBLOCK
    nq = ATT_QBLOCKS_PER_STEP if nblk % ATT_QBLOCKS_PER_STEP == 0 else 1
    row = lambda bi, i: (bi, i, 0)
    in_specs = [
        pl.BlockSpec(memory_space=pltpu.SMEM),
        pl.BlockSpec((1, nq * QBLOCK, ATT_Q), row),
    ]
    args = [sink, q]
    if local:
        in_specs += [
            pl.BlockSpec((1, QBLOCK, 2 * ATT_KV), lambda bi, i: (bi, jnp.maximum(i * nq - 1, 0), 0)),
            pl.BlockSpec((1, nq * QBLOCK, 2 * ATT_KV), row),
            pl.BlockSpec((1, QBLOCK, 2 * ATT_KV), lambda bi, i: (bi, jnp.minimum((i + 1) * nq, nblk - 1), 0)),
        ]
        args += [kv, kv, kv]
    in_specs.append(pl.BlockSpec((1, c, 2 * ATT_KV), lambda bi, i: (bi, 0, 0)))
    args.append(kv_ctx)
    return pl.pallas_call(
        functools.partial(_attn_kernel, local, nblk, nq),
        grid=(b, nblk // nq),
        in_specs=in_specs,
        out_specs=pl.BlockSpec((1, nq * QBLOCK, ATT_Q), row),
        out_shape=jax.ShapeDtypeStruct((b, t, ATT_Q), F32),
        compiler_params=_params(2),
        name="attention",
    )(*args)


def _scan_direction(reverse, lw_cum, b, kd, kk, v, r, st_ref, y_ref):
    L = SCAN_CHUNK
    ti = lax.broadcasted_iota(jnp.int32, (L, RW_W), 0)
    if reverse:
        lw_before = jnp.where(ti == L - 1, 0.0, pltpu.roll(lw_cum, L - 1, 0))
    else:
        lw_before = jnp.where(ti == 0, 0.0, pltpu.roll(lw_cum, 1, 0))
    lw_tot = lw_cum[0:1] if reverse else lw_cum[L - 1:L]
    e_neg = jnp.exp(-lw_cum)
    e_end = jnp.exp(lw_tot - lw_cum)

    lane = lax.broadcasted_iota(jnp.int32, (L, RW_W), 1)
    si = lane % HEAD_DIM
    strict = (ti < si) if reverse else (ti > si)
    incl = (ti <= si) if reverse else (ti >= si)
    eye = ti == si
    row = lax.broadcasted_iota(jnp.int32, (RW_W, RW_W), 0)
    col = lax.broadcasted_iota(jnp.int32, (RW_W, RW_W), 1)
    blk = (row // HEAD_DIM) == (col // HEAD_DIM)

    def block_diag(x):
        return jnp.where(blk, jnp.concatenate([x] * RW_HEADS, axis=0), 0.0).astype(BF16)

    def head_transpose(x):
        xt = x.T
        return jnp.concatenate([xt[h * HEAD_DIM:(h + 1) * HEAD_DIM] for h in range(RW_HEADS)], axis=1)

    alpha = kk * jnp.exp(lw_before)
    rho = r * jnp.exp(lw_cum)
    beta_end_t = head_transpose(b * e_end)
    kappa_end_t = head_transpose(kd * e_end)
    v_bd = block_diag(v)
    p1 = _dot_nt(jnp.concatenate([alpha, rho], axis=0),
                 jnp.concatenate([block_diag(b * e_neg), block_diag(kd * e_neg)], axis=0))
    yield
    nn = jnp.where(strict, p1[:L, :RW_W], 0.0)
    ak = jnp.where(strict, p1[:L, RW_W:], 0.0)
    rb = jnp.where(incl, p1[L:, :RW_W], 0.0)
    rk = jnp.where(incl, p1[L:, RW_W:], 0.0)
    tinv = jnp.where(eye, 1.0, 0.0) - nn
    npow = _dot(nn, block_diag(nn))
    kv = _dot(jnp.concatenate([ak, rk, kappa_end_t], axis=0), v_bd)
    akv, rkv, kev = kv[:L], kv[L:2 * L], kv[2 * L:]
    yield
    for _ in range(int(math.log2(L)) - 2):
        both = _dot(jnp.concatenate([npow, tinv], axis=0), block_diag(npow))
        npow, tinv = both[:L], tinv + both[L:]
        yield
    tinv = tinv + _dot(tinv, block_diag(npow))
    yield
    z = _dot(tinv, jnp.concatenate([block_diag(alpha), block_diag(-akv)], axis=1))
    yield
    t_alpha, x0 = z[:, :RW_W], z[:, RW_W:]
    bz = _dot(jnp.concatenate([rb, beta_end_t], axis=0),
              jnp.concatenate([block_diag(t_alpha), block_diag(x0)], axis=1))
    q_mat = rho - bz[:L, :RW_W]
    y0 = bz[:L, RW_W:] + rkv
    m_mat = jnp.where(eye, jnp.exp(lw_tot), 0.0) - bz[L:, :RW_W]
    c_mat = bz[L:, RW_W:] + kev
    yield
    both = _dot(jnp.concatenate([q_mat, m_mat], axis=0), block_diag(st_ref[...]))
    y_ref[...] = both[:L] + y0
    st_ref[...] = both[L:] + c_mat


def _scan_kernel(nchunk, nbatch, nsub, lw0_ref, b0_ref, kd0_ref, kkf_ref, vf_ref, rf_ref,
                 lw1_ref, b1_ref, kd1_ref, kkb_ref, vb_ref, rb_ref, s0_ref,
                 yf_ref, yb_ref, sT_ref, st_ref):
    c = pl.program_id(1)

    @pl.when(c == 0)
    def _():
        st_ref[...] = s0_ref[...]

    def chain(reverse, bi, refs, y_ref):
        order = range(nsub - 1, -1, -1) if reverse else range(nsub)
        for sub in order:
            rows = pl.ds(sub * SCAN_CHUNK, SCAN_CHUNK)
            yield from _scan_direction(reverse, *[ref[bi, rows] for ref in refs],
                                       st_ref.at[bi, int(reverse)], y_ref.at[bi, rows])

    chains = []
    for bi in range(nbatch):
        chains.append(chain(False, bi, (lw0_ref, b0_ref, kd0_ref, kkf_ref, vf_ref, rf_ref), yf_ref))
        chains.append(chain(True, bi, (lw1_ref, b1_ref, kd1_ref, kkb_ref, vb_ref, rb_ref), yb_ref))
    _round_robin(chains)

    @pl.when(c == nchunk - 1)
    def _():
        sT_ref[...] = st_ref[...]


def _scan(lw0, lw1, b0, b1, kd0, kd1, kk, v, r, s0, nbatch):
    assert SCAN_CHUNK == HEAD_DIM
    bsz, t, _ = kk.shape
    nsub = SCAN_CHUNKS_PER_STEP if (t // SCAN_CHUNK) % SCAN_CHUNKS_PER_STEP == 0 else 1
    nchunk = t // (SCAN_CHUNK * nsub)
    fwd = lambda bi, c: (bi, c, 0)
    bwd = lambda bi, c: (bi, nchunk - 1 - c, 0)
    blk = (nbatch, nsub * SCAN_CHUNK, RW_W)
    st_blk = (nbatch, N_DIR, HEAD_DIM, RW_W)
    st_map = lambda bi, c: (bi, 0, 0, 0)
    return pl.pallas_call(
        functools.partial(_scan_kernel, nchunk, nbatch, nsub),
        grid=(bsz // nbatch, nchunk),
        in_specs=[pl.BlockSpec(blk, fwd)] * 6 + [pl.BlockSpec(blk, bwd)] * 6
        + [pl.BlockSpec(st_blk, st_map)],
        out_specs=[pl.BlockSpec(blk, fwd), pl.BlockSpec(blk, bwd), pl.BlockSpec(st_blk, st_map)],
        out_shape=[jax.ShapeDtypeStruct((bsz, t, RW_W), F32),
                   jax.ShapeDtypeStruct((bsz, t, RW_W), F32),
                   jax.ShapeDtypeStruct(s0.shape, F32)],
        scratch_shapes=[pltpu.VMEM(st_blk, F32)],
        compiler_params=_params(2),
        name="rwkv_scan",
    )(lw0, b0, kd0, kk, v, r, lw1, b1, kd1, kk, v, r, s0)


FFN_SPLIT = 1
OUT_SUBTILES = 2


def _out_rows(rows, x_ref, sg_ref, att_ref, yf_ref, yb_ref, r_ref, kd0_ref, kd1_ref, v_ref, gd_ref,
              gt1_ref, sh2_ref, sc2_ref, gt2_ref, ng_ref, lng_ref, lnb_ref, rk_ref, g2_ref,
              wo_ref, wgu_ref, wd_ref, o_ref):
    hsum = _head_sum_matrix(RW_W)
    y = yf_ref[0, rows] + yb_ref[0, rows]
    mu = _dot_exact_rhs(y, hsum) * (1.0 / HEAD_DIM)
    bonus = _dot_exact_rhs(r_ref[0, rows] * (kd0_ref[0, rows] + kd1_ref[0, rows]) * rk_ref[...], hsum) * v_ref[0, rows]
    gate = _dot(jax.nn.sigmoid(gd_ref[0, rows]), g2_ref[...])
    y_mix = _dot(sg_ref[0, rows], wo_ref[:A_W]) + _dot(att_ref[0, rows], wo_ref[A_W:A_W + ATT_Q])
    yield
    yc = y - mu
    var = _dot_exact_rhs(yc * yc, hsum) * (1.0 / HEAD_DIM)
    yield
    yn = yc * lax.rsqrt(var + GN_EPS) * lng_ref[...] + lnb_ref[...]
    o_rw = (yn + bonus) * gate
    y_mix = y_mix + _dot(o_rw, wo_ref[A_W + ATT_Q:])
    yield
    x1 = x_ref[0, rows] + gt1_ref[0] * _rms_norm(y_mix, ng_ref[1:2])
    h2 = (_rms_norm(x1, ng_ref[2:3]) * (1.0 + sc2_ref[0]) + sh2_ref[0]).astype(BF16)
    fc = D_FF // FFN_SPLIT
    acc = None
    for j in range(FFN_SPLIT):
        gg = jnp.dot(h2, wgu_ref[:, j * fc:(j + 1) * fc], preferred_element_type=F32)
        uu = jnp.dot(h2, wgu_ref[:, D_FF + j * fc:D_FF + (j + 1) * fc], preferred_element_type=F32)
        yield
        part = _dot(gg * jax.nn.sigmoid(gg) * uu, wd_ref[j * fc:(j + 1) * fc])
        acc = part if acc is None else acc + part
        yield
    o_ref[0, rows] = x1 + gt2_ref[0] * _rms_norm(acc, ng_ref[3:4])


def _out_kernel(nsub, *refs):
    tm = refs[0].shape[1]
    sub = tm // nsub
    _round_robin([_out_rows(pl.ds(i * sub, sub), *refs) for i in range(nsub)])


def _mix_out_ffn(x, o_sg, o_att, yf, yb, r, kd0, kd1, v, gd, gt1, sh2, sc2, gt2, norm_g,
                 lnx_g, lnx_b, rk, g2_bf16, wo_bf16, wgu_bf16, wd_bf16, tm):
    b, t, d = x.shape
    row = lambda bi, i: (bi, i, 0)
    per_b = lambda bi, i: (bi, 0, 0)
    const2 = lambda bi, i: (0, 0)
    single = pl.Buffered(1)
    in_specs = [
        pl.BlockSpec((1, tm, d), row),
        pl.BlockSpec((1, tm, A_W), row),
        pl.BlockSpec((1, tm, ATT_Q), row),
    ] + [pl.BlockSpec((1, tm, RW_W), row)] * 6 + [
        pl.BlockSpec((1, tm, G_LORA), row),
    ] + [pl.BlockSpec((1, 1, d), per_b)] * 4 + [
        pl.BlockSpec((4, d), const2),
        pl.BlockSpec((1, RW_W), const2),
        pl.BlockSpec((1, RW_W), const2),
        pl.BlockSpec((1, RW_W), const2),
        pl.BlockSpec((G_LORA, RW_W), const2),
        pl.BlockSpec((MIX_W, d), const2, pipeline_mode=single),
        pl.BlockSpec((d, 2 * D_FF), const2, pipeline_mode=single),
        pl.BlockSpec((D_FF, d), const2, pipeline_mode=single),
    ]
    return pl.pallas_call(
        functools.partial(_out_kernel, OUT_SUBTILES if tm % (OUT_SUBTILES * V7X_SUBLANES) == 0 else 1),
        grid=(b, t // tm),
        in_specs=in_specs,
        out_specs=pl.BlockSpec((1, tm, d), row),
        out_shape=jax.ShapeDtypeStruct((b, t, d), F32),
        compiler_params=_params(2),
        name="mix_out_ffn",
    )(x, o_sg, o_att, yf, yb, r, kd0, kd1, v, gd, gt1, sh2, sc2, gt2, norm_g,
      lnx_g, lnx_b, rk, g2_bf16, wo_bf16, wgu_bf16, wd_bf16)


def _rope_tables(t):
    pos = np.arange(t)
    inv = np.float32(ROPE_THETA) ** (-np.arange(ROPE_FREQS, dtype=np.float32) / np.float32(ROPE_FREQS))
    ang_r = (pos // GRID_W).astype(np.float32)[:, None] * inv
    ang_c = (pos % GRID_W).astype(np.float32)[:, None] * inv
    cos = np.concatenate([np.cos(ang_r)] * 2 + [np.cos(ang_c)] * 2, axis=-1)
    sin = np.concatenate([-np.sin(ang_r), np.sin(ang_r), -np.sin(ang_c), np.sin(ang_c)], axis=-1)
    return (jnp.asarray(np.concatenate([cos, cos], axis=-1), F32),
            jnp.asarray(np.concatenate([sin, sin], axis=-1), F32))


def _row_tile(t, target):
    tm = min(t, target)
    while t % tm:
        tm //= 2
    return tm


def kernel(x, c, ctx, c_ctx, w_mod, b_mod, norm_g, w_in, w_out, sg_ln_g, sg_ln_b, sg_w, sg_b,
           attn_sink, rw_mu, rw_w0, rw_w2, rw_a0, rw_a2, rw_kk, rw_ka, rw_rk, rw_g2,
           rw_lnx_g, rw_lnx_b, w_gu, w_down):
    bsz, t, d = x.shape
    clen = ctx.shape[1]
    depth = w_mod.shape[0]

    rows = -(-(bsz + 1) // V7X_SUBLANES) * V7X_SUBLANES
    c_rows = jnp.zeros((rows, d), F32).at[:bsz].set(c).at[bsz].set(c_ctx)
    mods = _modulation(c_rows, w_mod, b_mod)

    cos, sin = _rope_tables(t)
    cos_c, sin_c = cos[:clen], sin[:clen]
    s_zero = jnp.zeros((bsz, N_DIR, HEAD_DIM, RW_W), F32)

    tm_proj = _row_tile(t, 512)
    tm_gmlp = _row_tile(t, 512)
    tm_out = _row_tile(t, 512)
    tc_out = _row_tile(clen, 256)
    nb_scan = SCAN_BATCH if bsz % SCAN_BATCH == 0 else 1

    xc = ctx
    for l in range(depth):
        last = l == depth - 1
        lat = [mods[l, :bsz, i * d:(i + 1) * d][:, None, :] for i in range(6)]
        cm = [jnp.broadcast_to(mods[l, bsz, i * d:(i + 1) * d][None, None, :], (bsz, 1, d))
              for i in range(6)]
        g = norm_g[l]
        w_in_b = w_in[l].astype(BF16)
        w2_b, a2_b = rw_w2[l].astype(BF16), rw_a2[l].astype(BF16)
        prep_params = (rw_mu[l][None], rw_kk[l][None], rw_ka[l][None], rw_w0[l][:, None, :], w2_b,
                       rw_a0[l][:, None, :], a2_b)

        (c_a, c_q, c_kv, c_r, c_gd, c_v, c_kk, c_lw0, c_lw1, c_b0, c_b1, c_kd0, c_kd1) = _proj_prep(
            xc, g[0:1], cm[0], cm[1], w_in_b, cos_c, sin_c, *prep_params, False, clen)
        c_yf, c_yb, s_ctx = _scan(c_lw0, c_lw1, c_b0, c_b1, c_kd0, c_kd1, c_kk, c_v, c_r, s_zero, nb_scan)

        (p_a, p_q, p_kv, r, gd, v, kk, lw0, lw1, b0, b1, kd0, kd1) = _proj_prep(
            x, g[0:1], lat[0], lat[1], w_in_b, cos, sin, *prep_params, True, tm_proj)
        yf, yb, _ = _scan(lw0, lw1, b0, b1, kd0, kd1, kk, v, r, s_ctx, nb_scan)

        sg_w_b = sg_w[l].astype(BF16)
        bias_plane = jnp.repeat(sg_b[l].T, HEAD_DIM, axis=1)
        o_sg = _gmlp(p_a, sg_ln_g[l][None], sg_ln_b[l][None], sg_w_b, bias_plane, tm_gmlp)
        o_att = _attention(p_q, p_kv, c_kv, attn_sink[l], True)

        out_params = (g, rw_lnx_g[l][None], rw_lnx_b[l][None], rw_rk[l].reshape(1, RW_W),
                      rw_g2[l].astype(BF16), w_out[l].astype(BF16), w_gu[l].astype(BF16),
                      w_down[l].astype(BF16))
        x_new = _mix_out_ffn(x, o_sg, o_att, yf, yb, r, kd0, kd1, v, gd,
                             lat[2], lat[3], lat[4], lat[5], *out_params, tm_out)
        if not last:
            oc_sg = _gmlp(c_a, sg_ln_g[l][None], sg_ln_b[l][None], sg_w_b, bias_plane, _row_tile(clen, 512))
            oc_att = _attention(c_q, c_kv, c_kv, attn_sink[l], False)
            xc = _mix_out_ffn(xc, oc_sg, oc_att, c_yf, c_yb, c_r, c_kd0, c_kd1, c_v, c_gd,
                              cm[2], cm[3], cm[4], cm[5], *out_params, tc_out)
        x = x_new
    return x
```

```python
import functools
import math

import jax
import jax.numpy as jnp
import numpy as np
from jax import lax
from jax.experimental import pallas as pl
from jax.experimental.pallas import tpu as pltpu

F32 = jnp.float32
BF16 = jnp.bfloat16

D_MODEL = 1024
HEAD_DIM = 64
GRID_W = 64
A_GROUPS = 4
A_W = A_GROUPS * HEAD_DIM
CHUNK = 128
ATT_HEADS = 8
ATT_KV_HEADS = 2
ATT_Q = ATT_HEADS * HEAD_DIM
ATT_KV = ATT_KV_HEADS * HEAD_DIM
WINDOW = 128
QBLOCK = 128
ROPE_THETA = 10000.0
ROPE_FREQS = HEAD_DIM // 4
NEG_INF = -1e30
ATT_QBLOCKS_PER_STEP = 4
ATT_SCALE = HEAD_DIM ** -0.5
RW_HEADS = 4
RW_W = RW_HEADS * HEAD_DIM
W_LORA = 64
A_LORA = 64
G_LORA = 128
N_DIR = 2
GN_EPS = 64e-5
MIX_W = A_W + ATT_Q + RW_W
D_FF = -(-8 * D_MODEL // (3 * 256)) * 256
RMS_EPS = 1e-6
LN_EPS = 1e-5
OFF_Q = 2 * A_W
OFF_R = OFF_Q + ATT_Q
OFF_CTX = OFF_R + RW_W + G_LORA
OFF_RWK = OFF_CTX + 2 * ATT_KV
IN_COLS = OFF_RWK + 2 * RW_W + N_DIR * (W_LORA + A_LORA)
MU_HEAD = RW_W + G_LORA
RWK_W = 2 * RW_W + N_DIR * (W_LORA + A_LORA)

V7X_LANES = 128
V7X_SUBLANES = 8
V7X_VMEM_LIMIT_BYTES = 56 * 1024 * 1024

SCAN_CHUNK = 64
SCAN_CHUNKS_PER_STEP = 4
SCAN_BATCH = 4
LOG_DECAY_SCALE = -math.exp(-0.5)


def _params(n_grid):
    return pltpu.CompilerParams(
        dimension_semantics=("arbitrary",) * n_grid, vmem_limit_bytes=V7X_VMEM_LIMIT_BYTES)


def _dot(a, b):
    return jnp.dot(a.astype(BF16), b.astype(BF16), preferred_element_type=F32)


def _dot_nt(a, b):
    return lax.dot_general(a.astype(BF16), b.astype(BF16), (((1,), (1,)), ((), ())),
                           preferred_element_type=F32)


def _split_bf16(x, n):
    parts = []
    for _ in range(n - 1):
        h = x.astype(BF16)
        parts.append(h)
        x = x - h.astype(F32)
    parts.append(x.astype(BF16))
    return parts


def _dot_exact_rhs(x, m_bf16, n=3):
    acc = None
    for p in _split_bf16(x, n):
        t = jnp.dot(p, m_bf16, preferred_element_type=F32)
        acc = t if acc is None else acc + t
    return acc


def _dot_exact_lhs(m_bf16, x, n=3):
    acc = None
    for p in _split_bf16(x, n):
        t = jnp.dot(m_bf16, p, preferred_element_type=F32)
        acc = t if acc is None else acc + t
    return acc


def _head_sum_matrix(width):
    i = lax.broadcasted_iota(jnp.int32, (width, width), 0) // HEAD_DIM
    j = lax.broadcasted_iota(jnp.int32, (width, width), 1) // HEAD_DIM
    return jnp.where(i == j, 1.0, 0.0).astype(BF16)


def _round_robin(chains):
    while chains:
        chains = [g for g in chains if next(g, True) is None]


def _rms_norm(x, g):
    return x * lax.rsqrt(jnp.mean(x * x, axis=-1, keepdims=True) + RMS_EPS) * g


def _tile_lanes(x, width):
    reps = width // x.shape[-1]
    return x if reps == 1 else jnp.concatenate([x] * reps, axis=-1)


def _rope(x, cos, sin_signed):
    w = x.shape[-1]
    lane = lax.broadcasted_iota(jnp.int32, x.shape, 1)
    lo = (lane % (2 * ROPE_FREQS)) < ROPE_FREQS
    partner = jnp.where(lo, pltpu.roll(x, w - ROPE_FREQS, 1), pltpu.roll(x, ROPE_FREQS, 1))
    return x * _tile_lanes(cos, w) + partner * _tile_lanes(sin_signed, w)


def _mod_kernel(c_ref, w_ref, b_ref, o_ref):
    c = c_ref[...]
    act = c * jax.nn.sigmoid(c)
    o_ref[0] = _dot(act, w_ref[0]) + b_ref[0]


def _modulation(c_rows, w_mod, b_mod):
    depth, d, n = w_mod.shape
    rows = c_rows.shape[0]
    tn = n // 4
    return pl.pallas_call(
        _mod_kernel,
        grid=(depth, n // tn),
        in_specs=[
            pl.BlockSpec((rows, d), lambda l, j: (0, 0)),
            pl.BlockSpec((1, d, tn), lambda l, j: (l, 0, j)),
            pl.BlockSpec((1, 1, tn), lambda l, j: (l, 0, j)),
        ],
        out_specs=pl.BlockSpec((1, rows, tn), lambda l, j: (l, 0, j)),
        out_shape=jax.ShapeDtypeStruct((depth, rows, n), F32),
        compiler_params=_params(2),
        name="modulation",
    )(c_rows, w_mod, b_mod.reshape(depth, 1, n))


def _shift_rows(x, k):
    return pltpu.roll(x, k % x.shape[0], 0)


def _token_shift(cur, prev_rows, next_rows, grid_mode, first, last):
    tm, width = cur.shape
    ch = lax.broadcasted_iota(jnp.int32, (tm, width), 1)
    pos = lax.broadcasted_iota(jnp.int32, (tm, width), 0)
    if grid_mode:
        qw = width // 4
        col = pos % GRID_W
        left = jnp.where(col == 0, 0.0, _shift_rows(cur, 1))
        right = jnp.where(col == GRID_W - 1, 0.0, _shift_rows(cur, -1))
        up = jnp.concatenate([prev_rows, cur[:tm - GRID_W]], axis=0)
        up = jnp.where((pos < GRID_W) & first, 0.0, up)
        down = jnp.concatenate([cur[GRID_W:], next_rows], axis=0)
        down = jnp.where((pos >= tm - GRID_W) & last, 0.0, down)
        return jnp.where(ch < qw, left, jnp.where(ch < 2 * qw, right, jnp.where(ch < 3 * qw, up, down)))
    prev = jnp.where(pos == 0, 0.0, _shift_rows(cur, 1))
    nxt = jnp.where(pos == tm - 1, 0.0, _shift_rows(cur, -1))
    return jnp.where(ch < width // 2, prev, nxt)


def _lane_slab(lo, hi):
    return (lo // V7X_LANES) * V7X_LANES, -(-hi // V7X_LANES) * V7X_LANES


def _halo_columns(hb, w_ref, off, width, quarter):
    qw = width // 4
    lo, hi = _lane_slab(quarter * qw, (quarter + 1) * qw)
    p = jnp.dot(hb, w_ref[:, off + lo:off + hi], preferred_element_type=F32)
    parts = []
    if lo:
        parts.append(jnp.zeros((p.shape[0], lo), F32))
    parts.append(p)
    if hi < width:
        parts.append(jnp.zeros((p.shape[0], width - hi), F32))
    return jnp.concatenate(parts, axis=1)


def _spatial_gating(z, ln_g, ln_b, w_ref, bias):
    z = jax.nn.gelu(z, approximate=True)
    u, v = z[:, :A_W], z[:, A_W:]
    mu = jnp.mean(v, axis=-1, keepdims=True)
    var = jnp.mean(jnp.square(v - mu), axis=-1, keepdims=True)
    vn = ((v - mu) * lax.rsqrt(var + LN_EPS) * ln_g + ln_b).astype(BF16)
    outs = []
    for c in range(z.shape[0] // CHUNK):
        rows = slice(c * CHUNK, (c + 1) * CHUNK)
        mixed = jnp.concatenate(
            [jnp.dot(w_ref[gi], vn[rows, gi * HEAD_DIM:(gi + 1) * HEAD_DIM],
                     preferred_element_type=F32) for gi in range(A_GROUPS)], axis=-1)
        outs.append(u[rows] * (mixed + bias))
    return jnp.concatenate(outs, axis=0)


def _proj_prep_kernel(grid_mode, nblk, *refs):
    if grid_mode:
        x_ref, xp_ref, xn_ref = refs[:3]
        refs = refs[3:]
    else:
        x_ref = refs[0]
        refs = refs[1:]
    (g_ref, sh_ref, sc_ref, w_ref, cos_ref, sin_ref,
     mu_rg_ref, mu_rwk_ref, kkp_ref, ka_ref, w0_ref, w2_ref, a0_ref, a2_ref,
     sg_g_ref, sg_b_ref, sg_w_ref, sg_bias_ref,
     osg_ref, pq_ref, pkv_ref,
     r_ref, gd_ref, v_ref, kk_ref, lw0_ref, lw1_ref, b0_ref, b1_ref, kd0_ref, kd1_ref) = refs
    i = pl.program_id(1)
    first, last = i == 0, i == nblk - 1

    def normed(x):
        return (_rms_norm(x, g_ref[...]) * (1.0 + sc_ref[0]) + sh_ref[0]).astype(BF16)

    def project(hb, start, width):
        return jnp.dot(hb, w_ref[:, start:start + width], preferred_element_type=F32)

    hb = normed(x_ref[0])
    osg_ref[0] = _spatial_gating(project(hb, 0, OFF_Q), sg_g_ref[...], sg_b_ref[...], sg_w_ref, sg_bias_ref[...])
    q = project(hb, OFF_Q, ATT_Q)
    kv = project(hb, OFF_CTX, 2 * ATT_KV)
    if grid_mode:
        q = _rope(q, cos_ref[...], sin_ref[...])
        kv = jnp.concatenate([_rope(kv[:, :ATT_KV], cos_ref[...], sin_ref[...]), kv[:, ATT_KV:]], axis=-1)
    pq_ref[0] = q * ATT_SCALE
    pkv_ref[0] = kv
    rg = project(hb, OFF_R, MU_HEAD)
    rwk = project(hb, OFF_RWK, RWK_W)
    if grid_mode:
        hp, hn = normed(xp_ref[0]), normed(xn_ref[0])
        rg_s = _token_shift(rg, _halo_columns(hp, w_ref, OFF_R, MU_HEAD, 2),
                            _halo_columns(hn, w_ref, OFF_R, MU_HEAD, 3), True, first, last)
        rwk_s = _token_shift(rwk, _halo_columns(hp, w_ref, OFF_RWK, RWK_W, 2),
                             _halo_columns(hn, w_ref, OFF_RWK, RWK_W, 3), True, first, last)
    else:
        rg_s = _token_shift(rg, None, None, False, first, last)
        rwk_s = _token_shift(rwk, None, None, False, first, last)
    rg = rg + (rg_s - rg) * mu_rg_ref[...]
    rwk = rwk + (rwk_s - rwk) * mu_rwk_ref[...]
    r_ref[0] = rg[:, :RW_W]
    gd_ref[0] = rg[:, RW_W:]
    k = rwk[:, :RW_W]
    v_ref[0] = rwk[:, RW_W:2 * RW_W]
    kk = k * kkp_ref[...]
    norm = jnp.sqrt(_dot_exact_rhs(kk * kk, _head_sum_matrix(RW_W)))
    kk = kk / jnp.maximum(norm, 1e-12)
    kk_ref[0] = kk
    base = 2 * RW_W
    for d, (lw_ref, b_ref, kd_ref) in enumerate(((lw0_ref, b0_ref, kd0_ref), (lw1_ref, b1_ref, kd1_ref))):
        wd = rwk[:, base + d * W_LORA:base + (d + 1) * W_LORA]
        ad = rwk[:, base + N_DIR * W_LORA + d * A_LORA:base + N_DIR * W_LORA + (d + 1) * A_LORA]
        z = w0_ref[d] + _dot(jnp.tanh(wd), w2_ref[d])
        lw = LOG_DECAY_SCALE * jax.nn.sigmoid(z)
        ci = lax.broadcasted_iota(jnp.int32, (SCAN_CHUNK, SCAN_CHUNK), 0)
        cj = lax.broadcasted_iota(jnp.int32, (SCAN_CHUNK, SCAN_CHUNK), 1)
        cum = jnp.where((ci <= cj) if d else (ci >= cj), 1.0, 0.0).astype(BF16)
        for c in range(lw.shape[0] // SCAN_CHUNK):
            rows = slice(c * SCAN_CHUNK, (c + 1) * SCAN_CHUNK)
            lw_ref[0, rows] = _dot_exact_lhs(cum, lw[rows])
        a = jax.nn.sigmoid(a0_ref[d] + _dot(ad, a2_ref[d]))
        b_ref[0] = kk * a
        kd_ref[0] = k * (1.0 + (a - 1.0) * ka_ref[...])


def _proj_prep(x, g, sh, sc, w_bf16, layer, cos, sin_signed, mu, kkp, ka, w0, w2_bf16, a0, a2_bf16,
               sg_ln_g, sg_ln_b, sg_w_bf16, sg_bias, grid_mode, tm):
    b, t, d = x.shape
    nblk = t // tm
    assert tm % SCAN_CHUNK == 0 and tm % CHUNK == 0 and (grid_mode or nblk == 1)
    row = lambda bi, i: (bi, i, 0)
    per_b = lambda bi, i: (bi, 0, 0)
    const2 = lambda bi, i: (0, 0)
    const3 = lambda bi, i: (0, 0, 0)
    in_specs, args = [pl.BlockSpec((1, tm, d), row)], [x]
    if grid_mode:
        per = tm // GRID_W
        nrow = t // GRID_W
        in_specs += [
            pl.BlockSpec((1, GRID_W, d), lambda bi, i: (bi, jnp.maximum(i * per - 1, 0), 0)),
            pl.BlockSpec((1, GRID_W, d), lambda bi, i: (bi, jnp.minimum((i + 1) * per, nrow - 1), 0)),
        ]
        args += [x, x]
    in_specs += [
        pl.BlockSpec((1, d), const2),
        pl.BlockSpec((1, 1, d), per_b),
        pl.BlockSpec((1, 1, d), per_b),
        pl.BlockSpec((None, d, IN_COLS), lambda bi, i: (layer, 0, 0)),
        pl.BlockSpec((tm, V7X_LANES), lambda bi, i: (i, 0)),
        pl.BlockSpec((tm, V7X_LANES), lambda bi, i: (i, 0)),
        pl.BlockSpec((1, MU_HEAD), const2),
        pl.BlockSpec((1, RWK_W), const2),
        pl.BlockSpec((1, RW_W), const2),
        pl.BlockSpec((1, RW_W), const2),
        pl.BlockSpec((N_DIR, 1, RW_W), const3),
        pl.BlockSpec((N_DIR, W_LORA, RW_W), const3),
        pl.BlockSpec((N_DIR, 1, RW_W), const3),
        pl.BlockSpec((N_DIR, A_LORA, RW_W), const3),
        pl.BlockSpec((1, A_W), const2),
        pl.BlockSpec((1, A_W), const2),
        pl.BlockSpec((A_GROUPS, CHUNK, CHUNK), const3),
        pl.BlockSpec((CHUNK, A_W), const2),
    ]
    args += [g, sh, sc, w_bf16, cos, sin_signed, mu[:, :MU_HEAD], mu[:, MU_HEAD:], kkp, ka,
             w0, w2_bf16, a0, a2_bf16, sg_ln_g, sg_ln_b, sg_w_bf16, sg_bias]
    widths = [A_W, ATT_Q, 2 * ATT_KV, RW_W, G_LORA] + [RW_W] * 8
    return pl.pallas_call(
        functools.partial(_proj_prep_kernel, grid_mode, nblk),
        grid=(b, nblk),
        in_specs=in_specs,
        out_specs=[pl.BlockSpec((1, tm, w), row) for w in widths],
        out_shape=[jax.ShapeDtypeStruct((b, t, w), F32) for w in widths],
        compiler_params=_params(2),
        name="proj_prep",
    )(*args)


def _attn_kernel(local, nblk, nq, sink_ref, q_ref, *refs):
    if local:
        kvp_ref, kvc_ref, kvn_ref, ctx_ref, o_ref = refs
    else:
        ctx_ref, o_ref = refs
    step = pl.program_id(1)
    group = ATT_HEADS // ATT_KV_HEADS
    ctx = ctx_ref[0]
    rows = group * QBLOCK
    row_head = lax.broadcasted_iota(jnp.int32, (rows, 1), 0) // QBLOCK
    if local:
        kv_all = jnp.concatenate([kvp_ref[0], kvc_ref[0], kvn_ref[0]], axis=0)
        qi = lax.broadcasted_iota(jnp.int32, (rows, 3 * QBLOCK), 0) % QBLOCK
        kj = lax.broadcasted_iota(jnp.int32, (rows, 3 * QBLOCK), 1)
        blk = kj // QBLOCK
        in_band = (jnp.abs(kj - QBLOCK - qi) <= WINDOW)
    sinks = []
    for g in range(ATT_KV_HEADS):
        sink = jnp.zeros((rows, 1), F32)
        for i in range(group):
            sink = jnp.where(row_head == i, sink_ref[g * group + i], sink)
        sinks.append(sink)
    units = [(j, g) for j in range(nq) for g in range(ATT_KV_HEADS)]
    s_ctx, s_loc, kv_loc = {}, {}, {}
    for j, g in units:
        q = q_ref[0, j * QBLOCK:(j + 1) * QBLOCK]
        qs = jnp.concatenate(
            [q[:, (g * group + i) * HEAD_DIM:(g * group + i + 1) * HEAD_DIM] for i in range(group)],
            axis=0)
        s_ctx[j, g] = _dot_nt(qs, ctx[:, g * HEAD_DIM:(g + 1) * HEAD_DIM])
        if local:
            n = step * nq + j
            valid = in_band & ((blk != 0) | (n > 0)) & ((blk != 2) | (n < nblk - 1))
            kv_loc[j] = kv_all[j * QBLOCK:(j + 3) * QBLOCK]
            s_loc[j, g] = jnp.where(valid, _dot_nt(qs, kv_loc[j][:, g * HEAD_DIM:(g + 1) * HEAD_DIM]), NEG_INF)
    p_ctx, p_loc, p_sink = {}, {}, {}
    for u in units:
        g = u[1]
        m = jnp.maximum(jnp.max(s_ctx[u], axis=-1, keepdims=True), sinks[g])
        if local:
            m = jnp.maximum(m, jnp.max(s_loc[u], axis=-1, keepdims=True))
        p_ctx[u] = jnp.exp(s_ctx[u] - m).astype(BF16)
        if local:
            p_loc[u] = jnp.exp(s_loc[u] - m).astype(BF16)
        p_sink[u] = jnp.exp(sinks[g] - m)

    def with_ones(v):
        return jnp.concatenate([v.astype(BF16), jnp.ones(v.shape, BF16)], axis=1)

    for u in units:
        j, g = u
        acc = _dot(p_ctx[u], with_ones(ctx[:, ATT_KV + g * HEAD_DIM:ATT_KV + (g + 1) * HEAD_DIM]))
        if local:
            acc = acc + _dot(p_loc[u], with_ones(kv_loc[j][:, ATT_KV + g * HEAD_DIM:ATT_KV + (g + 1) * HEAD_DIM]))
        out = acc[:, :HEAD_DIM] / (acc[:, HEAD_DIM:] + p_sink[u])
        for i in range(group):
            h = g * group + i
            o_ref[0, j * QBLOCK:(j + 1) * QBLOCK, h * HEAD_DIM:(h + 1) * HEAD_DIM] = out[i * QBLOCK:(i + 1) * QBLOCK]


def _attention(q, kv, kv_ctx, sink, local):
    b, t, _ = q.shape
    c = kv_ctx.shape[1]
    nblk = t // QBLOCK
    nq = ATT_QBLOCKS_PER_STEP if nblk % ATT_QBLOCKS_PER_STEP == 0 else 1
    row = lambda bi, i: (bi, i, 0)
    in_specs = [
        pl.BlockSpec(memory_space=pltpu.SMEM),
        pl.BlockSpec((1, nq * QBLOCK, ATT_Q), row),
    ]
    args = [sink, q]
    if local:
        in_specs += [
            pl.BlockSpec((1, QBLOCK, 2 * ATT_KV), lambda bi, i: (bi, jnp.maximum(i * nq - 1, 0), 0)),
            pl.BlockSpec((1, nq * QBLOCK, 2 * ATT_KV), row),
            pl.BlockSpec((1, QBLOCK, 2 * ATT_KV), lambda bi, i: (bi, jnp.minimum((i + 1) * nq, nblk - 1), 0)),
        ]
        args += [kv, kv, kv]
    in_specs.append(pl.BlockSpec((1, c, 2 * ATT_KV), lambda bi, i: (bi, 0, 0)))
    args.append(kv_ctx)
    return pl.pallas_call(
        functools.partial(_attn_kernel, local, nblk, nq),
        grid=(b, nblk // nq),
        in_specs=in_specs,
        out_specs=pl.BlockSpec((1, nq * QBLOCK, ATT_Q), row),
        out_shape=jax.ShapeDtypeStruct((b, t, ATT_Q), F32),
        compiler_params=_params(2),
        name="attention",
    )(*args)


def _scan_direction(reverse, lw_cum, b, kd, kk, v, r, st_ref, y_ref):
    L = SCAN_CHUNK
    ti = lax.broadcasted_iota(jnp.int32, (L, RW_W), 0)
    if reverse:
        lw_before = jnp.where(ti == L - 1, 0.0, pltpu.roll(lw_cum, L - 1, 0))
    else:
        lw_before = jnp.where(ti == 0, 0.0, pltpu.roll(lw_cum, 1, 0))
    lw_tot = lw_cum[0:1] if reverse else lw_cum[L - 1:L]
    e_neg = jnp.exp(-lw_cum)
    e_end = jnp.exp(lw_tot - lw_cum)

    lane = lax.broadcasted_iota(jnp.int32, (L, RW_W), 1)
    si = lane % HEAD_DIM
    strict = (ti < si) if reverse else (ti > si)
    incl = (ti <= si) if reverse else (ti >= si)
    eye = ti == si
    row = lax.broadcasted_iota(jnp.int32, (RW_W, RW_W), 0)
    col = lax.broadcasted_iota(jnp.int32, (RW_W, RW_W), 1)
    blk = (row // HEAD_DIM) == (col // HEAD_DIM)

    def block_diag(x):
        return jnp.where(blk, jnp.concatenate([x] * RW_HEADS, axis=0), 0.0).astype(BF16)

    def head_transpose(x):
        xt = x.T
        return jnp.concatenate([xt[h * HEAD_DIM:(h + 1) * HEAD_DIM] for h in range(RW_HEADS)], axis=1)

    alpha = kk * jnp.exp(lw_before)
    rho = r * jnp.exp(lw_cum)
    beta_end_t = head_transpose(b * e_end)
    kappa_end_t = head_transpose(kd * e_end)
    v_bd = block_diag(v)
    alpha_bd = block_diag(alpha)
    p1 = _dot_nt(jnp.concatenate([alpha, rho], axis=0),
                 jnp.concatenate([block_diag(b * e_neg), block_diag(kd * e_neg)], axis=0))
    yield
    nn = jnp.where(strict, p1[:L, :RW_W], 0.0)
    ak = jnp.where(strict, p1[:L, RW_W:], 0.0)
    rb = jnp.where(incl, p1[L:, :RW_W], 0.0)
    rk = jnp.where(incl, p1[L:, RW_W:], 0.0)
    tinv = jnp.where(eye, 1.0, 0.0) - nn
    npow = _dot(nn, block_diag(nn))
    kv = _dot(jnp.concatenate([ak, rk, kappa_end_t], axis=0), v_bd)
    akv, rkv, kev = kv[:L], kv[L:2 * L], kv[2 * L:]
    yield
    for _ in range(int(math.log2(L)) - 2):
        both = _dot(jnp.concatenate([npow, tinv], axis=0), block_diag(npow))
        npow, tinv = both[:L], tinv + both[L:]
        yield
    tinv = tinv + _dot(tinv, block_diag(npow))
    yield
    u = _dot(jnp.concatenate([rb, beta_end_t], axis=0), block_diag(tinv))
    yield
    bz = _dot(u, jnp.concatenate([alpha_bd, block_diag(-akv)], axis=1))
    q_mat = rho - bz[:L, :RW_W]
    y0 = bz[:L, RW_W:] + rkv
    m_mat = jnp.where(eye, jnp.exp(lw_tot), 0.0) - bz[L:, :RW_W]
    c_mat = bz[L:, RW_W:] + kev
    yield
    both = _dot(jnp.concatenate([q_mat, m_mat], axis=0), block_diag(st_ref[...]))
    y_ref[...] = both[:L] + y0
    st_ref[...] = both[L:] + c_mat


def _scan_kernel(nchunk, nbatch, nsub, lw0_ref, b0_ref, kd0_ref, kkf_ref, vf_ref, rf_ref,
                 lw1_ref, b1_ref, kd1_ref, kkb_ref, vb_ref, rb_ref, s0_ref,
                 yf_ref, yb_ref, sT_ref, st_ref):
    c = pl.program_id(1)

    @pl.when(c == 0)
    def _():
        st_ref[...] = s0_ref[...]

    def chain(reverse, bi, refs, y_ref):
        order = range(nsub - 1, -1, -1) if reverse else range(nsub)
        for sub in order:
            rows = pl.ds(sub * SCAN_CHUNK, SCAN_CHUNK)
            yield from _scan_direction(reverse, *[ref[bi, rows] for ref in refs],
                                       st_ref.at[bi, int(reverse)], y_ref.at[bi, rows])

    chains = []
    for bi in range(nbatch):
        chains.append(chain(False, bi, (lw0_ref, b0_ref, kd0_ref, kkf_ref, vf_ref, rf_ref), yf_ref))
        chains.append(chain(True, bi, (lw1_ref, b1_ref, kd1_ref, kkb_ref, vb_ref, rb_ref), yb_ref))
    _round_robin(chains)

    @pl.when(c == nchunk - 1)
    def _():
        sT_ref[...] = st_ref[...]


def _scan(lw0, lw1, b0, b1, kd0, kd1, kk, v, r, s0, nbatch):
    assert SCAN_CHUNK == HEAD_DIM
    bsz, t, _ = kk.shape
    nsub = SCAN_CHUNKS_PER_STEP if (t // SCAN_CHUNK) % SCAN_CHUNKS_PER_STEP == 0 else 1
    nchunk = t // (SCAN_CHUNK * nsub)
    fwd = lambda bi, c: (bi, c, 0)
    bwd = lambda bi, c: (bi, nchunk - 1 - c, 0)
    blk = (nbatch, nsub * SCAN_CHUNK, RW_W)
    st_blk = (nbatch, N_DIR, HEAD_DIM, RW_W)
    st_map = lambda bi, c: (bi, 0, 0, 0)
    return pl.pallas_call(
        functools.partial(_scan_kernel, nchunk, nbatch, nsub),
        grid=(bsz // nbatch, nchunk),
        in_specs=[pl.BlockSpec(blk, fwd)] * 6 + [pl.BlockSpec(blk, bwd)] * 6
        + [pl.BlockSpec(st_blk, st_map)],
        out_specs=[pl.BlockSpec(blk, fwd), pl.BlockSpec(blk, bwd), pl.BlockSpec(st_blk, st_map)],
        out_shape=[jax.ShapeDtypeStruct((bsz, t, RW_W), F32),
                   jax.ShapeDtypeStruct((bsz, t, RW_W), F32),
                   jax.ShapeDtypeStruct(s0.shape, F32)],
        scratch_shapes=[pltpu.VMEM(st_blk, F32)],
        compiler_params=_params(2),
        name="rwkv_scan",
    )(lw0, b0, kd0, kk, v, r, lw1, b1, kd1, kk, v, r, s0)


FFN_SPLIT = 1
OUT_SUBTILES = 2


def _out_rows(rows, x_ref, sg_ref, att_ref, yf_ref, yb_ref, r_ref, kd0_ref, kd1_ref, v_ref, gd_ref,
              gt1_ref, sh2_ref, sc2_ref, gt2_ref, ng_ref, lng_ref, lnb_ref, rk_ref, g2_ref,
              wo_ref, wgu_ref, wd_ref, o_ref):
    hsum = _head_sum_matrix(RW_W)
    y = yf_ref[0, rows] + yb_ref[0, rows]
    mu = _dot_exact_rhs(y, hsum) * (1.0 / HEAD_DIM)
    bonus = _dot_exact_rhs(r_ref[0, rows] * (kd0_ref[0, rows] + kd1_ref[0, rows]) * rk_ref[...], hsum, 2) * v_ref[0, rows]
    gate = _dot(jax.nn.sigmoid(gd_ref[0, rows]), g2_ref[...])
    y_mix = _dot(sg_ref[0, rows], wo_ref[:A_W]) + _dot(att_ref[0, rows], wo_ref[A_W:A_W + ATT_Q])
    yield
    yc = y - mu
    var = _dot_exact_rhs(yc * yc, hsum, 2) * (1.0 / HEAD_DIM)
    yield
    yn = yc * lax.rsqrt(var + GN_EPS) * lng_ref[...] + lnb_ref[...]
    o_rw = (yn + bonus) * gate
    y_mix = y_mix + _dot(o_rw, wo_ref[A_W + ATT_Q:])
    yield
    x1 = x_ref[0, rows] + gt1_ref[0] * _rms_norm(y_mix, ng_ref[1:2])
    h2 = (_rms_norm(x1, ng_ref[2:3]) * (1.0 + sc2_ref[0]) + sh2_ref[0]).astype(BF16)
    fc = D_FF // FFN_SPLIT
    acc = None
    for j in range(FFN_SPLIT):
        gg = jnp.dot(h2, wgu_ref[:, j * fc:(j + 1) * fc], preferred_element_type=F32)
        uu = jnp.dot(h2, wgu_ref[:, D_FF + j * fc:D_FF + (j + 1) * fc], preferred_element_type=F32)
        yield
        part = _dot(gg * jax.nn.sigmoid(gg) * uu, wd_ref[j * fc:(j + 1) * fc])
        acc = part if acc is None else acc + part
        yield
    o_ref[0, rows] = x1 + gt2_ref[0] * _rms_norm(acc, ng_ref[3:4])


def _out_kernel(nsub, *refs):
    tm = refs[0].shape[1]
    sub = tm // nsub
    _round_robin([_out_rows(pl.ds(i * sub, sub), *refs) for i in range(nsub)])


def _mix_out_ffn(x, o_sg, o_att, yf, yb, r, kd0, kd1, v, gd, gt1, sh2, sc2, gt2, norm_g,
                 lnx_g, lnx_b, rk, g2_bf16, wo_bf16, wgu_bf16, wd_bf16, layer, tm):
    b, t, d = x.shape
    row = lambda bi, i: (bi, i, 0)
    per_b = lambda bi, i: (bi, 0, 0)
    const2 = lambda bi, i: (0, 0)
    single = pl.Buffered(1)
    stacked = lambda bi, i: (layer, 0, 0)
    in_specs = [
        pl.BlockSpec((1, tm, d), row),
        pl.BlockSpec((1, tm, A_W), row),
        pl.BlockSpec((1, tm, ATT_Q), row),
    ] + [pl.BlockSpec((1, tm, RW_W), row)] * 6 + [
        pl.BlockSpec((1, tm, G_LORA), row),
    ] + [pl.BlockSpec((1, 1, d), per_b)] * 4 + [
        pl.BlockSpec((4, d), const2),
        pl.BlockSpec((1, RW_W), const2),
        pl.BlockSpec((1, RW_W), const2),
        pl.BlockSpec((1, RW_W), const2),
        pl.BlockSpec((G_LORA, RW_W), const2),
        pl.BlockSpec((None, MIX_W, d), stacked, pipeline_mode=single),
        pl.BlockSpec((None, d, 2 * D_FF), stacked, pipeline_mode=single),
        pl.BlockSpec((None, D_FF, d), stacked, pipeline_mode=single),
    ]
    return pl.pallas_call(
        functools.partial(_out_kernel, OUT_SUBTILES if tm % (OUT_SUBTILES * V7X_SUBLANES) == 0 else 1),
        grid=(b, t // tm),
        in_specs=in_specs,
        out_specs=pl.BlockSpec((1, tm, d), row),
        out_shape=jax.ShapeDtypeStruct((b, t, d), F32),
        compiler_params=_params(2),
        name="mix_out_ffn",
    )(x, o_sg, o_att, yf, yb, r, kd0, kd1, v, gd, gt1, sh2, sc2, gt2, norm_g,
      lnx_g, lnx_b, rk, g2_bf16, wo_bf16, wgu_bf16, wd_bf16)


def _rope_tables(t):
    pos = np.arange(t)
    inv = np.float32(ROPE_THETA) ** (-np.arange(ROPE_FREQS, dtype=np.float32) / np.float32(ROPE_FREQS))
    ang_r = (pos // GRID_W).astype(np.float32)[:, None] * inv
    ang_c = (pos % GRID_W).astype(np.float32)[:, None] * inv
    cos = np.concatenate([np.cos(ang_r)] * 2 + [np.cos(ang_c)] * 2, axis=-1)
    sin = np.concatenate([-np.sin(ang_r), np.sin(ang_r), -np.sin(ang_c), np.sin(ang_c)], axis=-1)
    return (jnp.asarray(np.concatenate([cos, cos], axis=-1), F32),
            jnp.asarray(np.concatenate([sin, sin], axis=-1), F32))


def _row_tile(t, target):
    tm = min(t, target)
    while t % tm:
        tm //= 2
    return tm


def kernel(x, c, ctx, c_ctx, w_mod, b_mod, norm_g, w_in, w_out, sg_ln_g, sg_ln_b, sg_w, sg_b,
           attn_sink, rw_mu, rw_w0, rw_w2, rw_a0, rw_a2, rw_kk, rw_ka, rw_rk, rw_g2,
           rw_lnx_g, rw_lnx_b, w_gu, w_down):
    bsz, t, d = x.shape
    clen = ctx.shape[1]
    depth = w_mod.shape[0]

    rows = -(-(bsz + 1) // V7X_SUBLANES) * V7X_SUBLANES
    c_rows = jnp.zeros((rows, d), F32).at[:bsz].set(c).at[bsz].set(c_ctx)
    mods = _modulation(c_rows, w_mod, b_mod)

    cos, sin = _rope_tables(t)
    cos_c, sin_c = cos[:clen], sin[:clen]
    s_zero = jnp.zeros((bsz, N_DIR, HEAD_DIM, RW_W), F32)

    tm_proj = _row_tile(t, 512)
    tm_out = _row_tile(t, 512)
    tc_out = _row_tile(clen, 256)
    nb_scan = SCAN_BATCH if bsz % SCAN_BATCH == 0 else 1

    w_in_b, w_out_b, w_gu_b, w_down_b = (w.astype(BF16) for w in (w_in, w_out, w_gu, w_down))

    xc = ctx
    for l in range(depth):
        last = l == depth - 1
        lat = [mods[l, :bsz, i * d:(i + 1) * d][:, None, :] for i in range(6)]
        cm = [jnp.broadcast_to(mods[l, bsz, i * d:(i + 1) * d][None, None, :], (bsz, 1, d))
              for i in range(6)]
        g = norm_g[l]
        w2_b, a2_b = rw_w2[l].astype(BF16), rw_a2[l].astype(BF16)
        prep_params = (rw_mu[l][None], rw_kk[l][None], rw_ka[l][None], rw_w0[l][:, None, :], w2_b,
                       rw_a0[l][:, None, :], a2_b,
                       sg_ln_g[l][None], sg_ln_b[l][None], sg_w[l].astype(BF16),
                       jnp.repeat(sg_b[l].T, HEAD_DIM, axis=1))

        (oc_sg, c_q, c_kv, c_r, c_gd, c_v, c_kk, c_lw0, c_lw1, c_b0, c_b1, c_kd0, c_kd1) = _proj_prep(
            xc, g[0:1], cm[0], cm[1], w_in_b, l, cos_c, sin_c, *prep_params, False, clen)
        c_yf, c_yb, s_ctx = _scan(c_lw0, c_lw1, c_b0, c_b1, c_kd0, c_kd1, c_kk, c_v, c_r, s_zero, nb_scan)

        (o_sg, p_q, p_kv, r, gd, v, kk, lw0, lw1, b0, b1, kd0, kd1) = _proj_prep(
            x, g[0:1], lat[0], lat[1], w_in_b, l, cos, sin, *prep_params, True, tm_proj)
        yf, yb, _ = _scan(lw0, lw1, b0, b1, kd0, kd1, kk, v, r, s_ctx, nb_scan)

        o_att = _attention(p_q, p_kv, c_kv, attn_sink[l], True)

        out_params = (g, rw_lnx_g[l][None], rw_lnx_b[l][None], rw_rk[l].reshape(1, RW_W),
                      rw_g2[l].astype(BF16), w_out_b, w_gu_b, w_down_b, l)
        x_new = _mix_out_ffn(x, o_sg, o_att, yf, yb, r, kd0, kd1, v, gd,
                             lat[2], lat[3], lat[4], lat[5], *out_params, tm_out)
        if not last:
            oc_att = _attention(c_q, c_kv, c_kv, attn_sink[l], False)
            xc = _mix_out_ffn(xc, oc_sg, oc_att, c_yf, c_yb, c_r, c_kd0, c_kd1, c_v, c_gd,
                              cm[2], cm[3], cm[4], cm[5], *out_params, tc_out)
        x = x_new
    return x
```

```python
import functools
import math

import jax
import jax.numpy as jnp
import numpy as np
from jax import lax
from jax.experimental import pallas as pl
from jax.experimental.pallas import tpu as pltpu

F32 = jnp.float32
BF16 = jnp.bfloat16

D_MODEL = 1024
HEAD_DIM = 64
GRID_W = 64
A_GROUPS = 4
A_W = A_GROUPS * HEAD_DIM
CHUNK = 128
ATT_HEADS = 8
ATT_KV_HEADS = 2
ATT_Q = ATT_HEADS * HEAD_DIM
ATT_KV = ATT_KV_HEADS * HEAD_DIM
WINDOW = 128
QBLOCK = 128
ROPE_THETA = 10000.0
ROPE_FREQS = HEAD_DIM // 4
NEG_INF = -1e30
ATT_QBLOCKS_PER_STEP = 4
ATT_SCALE = HEAD_DIM ** -0.5
RW_HEADS = 4
RW_W = RW_HEADS * HEAD_DIM
W_LORA = 64
A_LORA = 64
G_LORA = 128
N_DIR = 2
GN_EPS = 64e-5
MIX_W = A_W + ATT_Q + RW_W
D_FF = -(-8 * D_MODEL // (3 * 256)) * 256
RMS_EPS = 1e-6
LN_EPS = 1e-5
OFF_Q = 2 * A_W
OFF_R = OFF_Q + ATT_Q
OFF_CTX = OFF_R + RW_W + G_LORA
OFF_RWK = OFF_CTX + 2 * ATT_KV
IN_COLS = OFF_RWK + 2 * RW_W + N_DIR * (W_LORA + A_LORA)
MU_HEAD = RW_W + G_LORA
RWK_W = 2 * RW_W + N_DIR * (W_LORA + A_LORA)

V7X_LANES = 128
V7X_SUBLANES = 8
V7X_VMEM_LIMIT_BYTES = 56 * 1024 * 1024

SCAN_CHUNK = 64
SCAN_CHUNKS_PER_STEP = 4
SCAN_BATCH = 4
LOG_DECAY_SCALE = -math.exp(-0.5)


def _params(n_grid):
    return pltpu.CompilerParams(
        dimension_semantics=("arbitrary",) * n_grid, vmem_limit_bytes=V7X_VMEM_LIMIT_BYTES)


def _dot(a, b):
    return jnp.dot(a.astype(BF16), b.astype(BF16), preferred_element_type=F32)


def _dot_nt(a, b):
    return lax.dot_general(a.astype(BF16), b.astype(BF16), (((1,), (1,)), ((), ())),
                           preferred_element_type=F32)


def _split_bf16(x, n):
    parts = []
    for _ in range(n - 1):
        h = x.astype(BF16)
        parts.append(h)
        x = x - h.astype(F32)
    parts.append(x.astype(BF16))
    return parts


def _dot_exact_rhs(x, m_bf16, n=3):
    acc = None
    for p in _split_bf16(x, n):
        t = jnp.dot(p, m_bf16, preferred_element_type=F32)
        acc = t if acc is None else acc + t
    return acc


def _dot_exact_lhs(m_bf16, x, n=3):
    acc = None
    for p in _split_bf16(x, n):
        t = jnp.dot(m_bf16, p, preferred_element_type=F32)
        acc = t if acc is None else acc + t
    return acc


def _head_sum_matrix(width):
    i = lax.broadcasted_iota(jnp.int32, (width, width), 0) // HEAD_DIM
    j = lax.broadcasted_iota(jnp.int32, (width, width), 1) // HEAD_DIM
    return jnp.where(i == j, 1.0, 0.0).astype(BF16)


def _round_robin(chains):
    while chains:
        chains = [g for g in chains if next(g, True) is None]


def _rms_norm(x, g):
    return x * lax.rsqrt(jnp.mean(x * x, axis=-1, keepdims=True) + RMS_EPS) * g


def _tile_lanes(x, width):
    reps = width // x.shape[-1]
    return x if reps == 1 else jnp.concatenate([x] * reps, axis=-1)


def _rope(x, cos, sin_signed):
    w = x.shape[-1]
    lane = lax.broadcasted_iota(jnp.int32, x.shape, 1)
    lo = (lane % (2 * ROPE_FREQS)) < ROPE_FREQS
    partner = jnp.where(lo, pltpu.roll(x, w - ROPE_FREQS, 1), pltpu.roll(x, ROPE_FREQS, 1))
    return x * _tile_lanes(cos, w) + partner * _tile_lanes(sin_signed, w)


def _mod_kernel(c_ref, w_ref, b_ref, o_ref):
    c = c_ref[...]
    act = c * jax.nn.sigmoid(c)
    o_ref[0] = _dot(act, w_ref[0]) + b_ref[0]


def _modulation(c_rows, w_mod, b_mod):
    depth, d, n = w_mod.shape
    rows = c_rows.shape[0]
    tn = n // 4
    return pl.pallas_call(
        _mod_kernel,
        grid=(depth, n // tn),
        in_specs=[
            pl.BlockSpec((rows, d), lambda l, j: (0, 0)),
            pl.BlockSpec((1, d, tn), lambda l, j: (l, 0, j)),
            pl.BlockSpec((1, 1, tn), lambda l, j: (l, 0, j)),
        ],
        out_specs=pl.BlockSpec((1, rows, tn), lambda l, j: (l, 0, j)),
        out_shape=jax.ShapeDtypeStruct((depth, rows, n), F32),
        compiler_params=_params(2),
        name="modulation",
    )(c_rows, w_mod, b_mod.reshape(depth, 1, n))


def _shift_rows(x, k):
    return pltpu.roll(x, k % x.shape[0], 0)


def _token_shift(cur, prev_rows, next_rows, grid_mode, first, last):
    tm, width = cur.shape
    ch = lax.broadcasted_iota(jnp.int32, (tm, width), 1)
    pos = lax.broadcasted_iota(jnp.int32, (tm, width), 0)
    if grid_mode:
        qw = width // 4
        col = pos % GRID_W
        left = jnp.where(col == 0, 0.0, _shift_rows(cur, 1))
        right = jnp.where(col == GRID_W - 1, 0.0, _shift_rows(cur, -1))
        up = jnp.concatenate([prev_rows, cur[:tm - GRID_W]], axis=0)
        up = jnp.where((pos < GRID_W) & first, 0.0, up)
        down = jnp.concatenate([cur[GRID_W:], next_rows], axis=0)
        down = jnp.where((pos >= tm - GRID_W) & last, 0.0, down)
        return jnp.where(ch < qw, left, jnp.where(ch < 2 * qw, right, jnp.where(ch < 3 * qw, up, down)))
    prev = jnp.where(pos == 0, 0.0, _shift_rows(cur, 1))
    nxt = jnp.where(pos == tm - 1, 0.0, _shift_rows(cur, -1))
    return jnp.where(ch < width // 2, prev, nxt)


def _lane_slab(lo, hi):
    return (lo // V7X_LANES) * V7X_LANES, -(-hi // V7X_LANES) * V7X_LANES


def _halo_columns(hb, w_ref, off, width, quarter):
    qw = width // 4
    lo, hi = _lane_slab(quarter * qw, (quarter + 1) * qw)
    p = jnp.dot(hb, w_ref[:, off + lo:off + hi], preferred_element_type=F32)
    parts = []
    if lo:
        parts.append(jnp.zeros((p.shape[0], lo), F32))
    parts.append(p)
    if hi < width:
        parts.append(jnp.zeros((p.shape[0], width - hi), F32))
    return jnp.concatenate(parts, axis=1)


def _spatial_gating(z, ln_g, ln_b, w_ref, bias):
    z = jax.nn.gelu(z, approximate=True)
    u, v = z[:, :A_W], z[:, A_W:]
    mu = jnp.mean(v, axis=-1, keepdims=True)
    var = jnp.mean(jnp.square(v - mu), axis=-1, keepdims=True)
    vn = ((v - mu) * lax.rsqrt(var + LN_EPS) * ln_g + ln_b).astype(BF16)
    outs = []
    for c in range(z.shape[0] // CHUNK):
        rows = slice(c * CHUNK, (c + 1) * CHUNK)
        mixed = jnp.concatenate(
            [jnp.dot(w_ref[gi], vn[rows, gi * HEAD_DIM:(gi + 1) * HEAD_DIM],
                     preferred_element_type=F32) for gi in range(A_GROUPS)], axis=-1)
        outs.append(u[rows] * (mixed + bias))
    return jnp.concatenate(outs, axis=0)


def _proj_prep_kernel(grid_mode, nblk, *refs):
    if grid_mode:
        x_ref, xp_ref, xn_ref = refs[:3]
        refs = refs[3:]
    else:
        x_ref = refs[0]
        refs = refs[1:]
    (g_ref, sh_ref, sc_ref, w_ref, cos_ref, sin_ref,
     mu_rg_ref, mu_rwk_ref, kkp_ref, ka_ref, w0_ref, w2_ref, a0_ref, a2_ref,
     sg_g_ref, sg_b_ref, sg_w_ref, sg_bias_ref,
     osg_ref, pq_ref, pkv_ref,
     r_ref, gd_ref, v_ref, kk_ref, lw0_ref, lw1_ref, b0_ref, b1_ref, kd0_ref, kd1_ref) = refs
    i = pl.program_id(1)
    first, last = i == 0, i == nblk - 1

    def normed(x):
        return (_rms_norm(x, g_ref[...]) * (1.0 + sc_ref[0]) + sh_ref[0]).astype(BF16)

    def project(hb, start, width):
        return jnp.dot(hb, w_ref[:, start:start + width], preferred_element_type=F32)

    hb = normed(x_ref[0])
    rg = project(hb, OFF_R, MU_HEAD)
    rwk = project(hb, OFF_RWK, RWK_W)
    if grid_mode:
        hp, hn = normed(xp_ref[0]), normed(xn_ref[0])
        halos = [_halo_columns(hp, w_ref, OFF_R, MU_HEAD, 2), _halo_columns(hn, w_ref, OFF_R, MU_HEAD, 3),
                 _halo_columns(hp, w_ref, OFF_RWK, RWK_W, 2), _halo_columns(hn, w_ref, OFF_RWK, RWK_W, 3)]
    q = project(hb, OFF_Q, ATT_Q)
    kv = project(hb, OFF_CTX, 2 * ATT_KV)
    z_a = project(hb, 0, OFF_Q)
    if grid_mode:
        q = _rope(q, cos_ref[...], sin_ref[...])
        kv = jnp.concatenate([_rope(kv[:, :ATT_KV], cos_ref[...], sin_ref[...]), kv[:, ATT_KV:]], axis=-1)
    pq_ref[0] = q * ATT_SCALE
    pkv_ref[0] = kv
    if grid_mode:
        rg_s = _token_shift(rg, halos[0], halos[1], True, first, last)
        rwk_s = _token_shift(rwk, halos[2], halos[3], True, first, last)
    else:
        rg_s = _token_shift(rg, None, None, False, first, last)
        rwk_s = _token_shift(rwk, None, None, False, first, last)
    rg = rg + (rg_s - rg) * mu_rg_ref[...]
    rwk = rwk + (rwk_s - rwk) * mu_rwk_ref[...]
    r_ref[0] = rg[:, :RW_W]
    gd_ref[0] = rg[:, RW_W:]
    k = rwk[:, :RW_W]
    v_ref[0] = rwk[:, RW_W:2 * RW_W]
    kk = k * kkp_ref[...]
    norm = jnp.sqrt(_dot_exact_rhs(kk * kk, _head_sum_matrix(RW_W)))
    kk = kk / jnp.maximum(norm, 1e-12)
    kk_ref[0] = kk
    base = 2 * RW_W
    lws = []
    for d, (b_ref, kd_ref) in enumerate(((b0_ref, kd0_ref), (b1_ref, kd1_ref))):
        wd = rwk[:, base + d * W_LORA:base + (d + 1) * W_LORA]
        ad = rwk[:, base + N_DIR * W_LORA + d * A_LORA:base + N_DIR * W_LORA + (d + 1) * A_LORA]
        z = w0_ref[d] + _dot(jnp.tanh(wd), w2_ref[d])
        lws.append(LOG_DECAY_SCALE * jax.nn.sigmoid(z))
        a = jax.nn.sigmoid(a0_ref[d] + _dot(ad, a2_ref[d]))
        b_ref[0] = kk * a
        kd_ref[0] = k * (1.0 + (a - 1.0) * ka_ref[...])
    osg_ref[0] = _spatial_gating(z_a, sg_g_ref[...], sg_b_ref[...], sg_w_ref, sg_bias_ref[...])
    ci = lax.broadcasted_iota(jnp.int32, (SCAN_CHUNK, SCAN_CHUNK), 0)
    cj = lax.broadcasted_iota(jnp.int32, (SCAN_CHUNK, SCAN_CHUNK), 1)
    for d, lw_ref in enumerate((lw0_ref, lw1_ref)):
        cum = jnp.where((ci <= cj) if d else (ci >= cj), 1.0, 0.0).astype(BF16)
        for c in range(lws[d].shape[0] // SCAN_CHUNK):
            rows = slice(c * SCAN_CHUNK, (c + 1) * SCAN_CHUNK)
            lw_ref[0, rows] = _dot_exact_lhs(cum, lws[d][rows])


def _proj_prep(x, g, sh, sc, w_bf16, layer, cos, sin_signed, mu, kkp, ka, w0, w2_bf16, a0, a2_bf16,
               sg_ln_g, sg_ln_b, sg_w_bf16, sg_bias, grid_mode, tm):
    b, t, d = x.shape
    nblk = t // tm
    assert tm % SCAN_CHUNK == 0 and tm % CHUNK == 0 and (grid_mode or nblk == 1)
    row = lambda bi, i: (bi, i, 0)
    per_b = lambda bi, i: (bi, 0, 0)
    const2 = lambda bi, i: (0, 0)
    const3 = lambda bi, i: (0, 0, 0)
    in_specs, args = [pl.BlockSpec((1, tm, d), row)], [x]
    if grid_mode:
        per = tm // GRID_W
        nrow = t // GRID_W
        in_specs += [
            pl.BlockSpec((1, GRID_W, d), lambda bi, i: (bi, jnp.maximum(i * per - 1, 0), 0)),
            pl.BlockSpec((1, GRID_W, d), lambda bi, i: (bi, jnp.minimum((i + 1) * per, nrow - 1), 0)),
        ]
        args += [x, x]
    in_specs += [
        pl.BlockSpec((1, d), const2),
        pl.BlockSpec((1, 1, d), per_b),
        pl.BlockSpec((1, 1, d), per_b),
        pl.BlockSpec((None, d, IN_COLS), lambda bi, i: (layer, 0, 0)),
        pl.BlockSpec((tm, V7X_LANES), lambda bi, i: (i, 0)),
        pl.BlockSpec((tm, V7X_LANES), lambda bi, i: (i, 0)),
        pl.BlockSpec((1, MU_HEAD), const2),
        pl.BlockSpec((1, RWK_W), const2),
        pl.BlockSpec((1, RW_W), const2),
        pl.BlockSpec((1, RW_W), const2),
        pl.BlockSpec((N_DIR, 1, RW_W), const3),
        pl.BlockSpec((N_DIR, W_LORA, RW_W), const3),
        pl.BlockSpec((N_DIR, 1, RW_W), const3),
        pl.BlockSpec((N_DIR, A_LORA, RW_W), const3),
        pl.BlockSpec((1, A_W), const2),
        pl.BlockSpec((1, A_W), const2),
        pl.BlockSpec((A_GROUPS, CHUNK, CHUNK), const3),
        pl.BlockSpec((CHUNK, A_W), const2),
    ]
    args += [g, sh, sc, w_bf16, cos, sin_signed, mu[:, :MU_HEAD], mu[:, MU_HEAD:], kkp, ka,
             w0, w2_bf16, a0, a2_bf16, sg_ln_g, sg_ln_b, sg_w_bf16, sg_bias]
    widths = [A_W, ATT_Q, 2 * ATT_KV, RW_W, G_LORA] + [RW_W] * 8
    return pl.pallas_call(
        functools.partial(_proj_prep_kernel, grid_mode, nblk),
        grid=(b, nblk),
        in_specs=in_specs,
        out_specs=[pl.BlockSpec((1, tm, w), row) for w in widths],
        out_shape=[jax.ShapeDtypeStruct((b, t, w), F32) for w in widths],
        compiler_params=_params(2),
        name="proj_prep",
    )(*args)


def _attn_kernel(local, nblk, nq, sink_ref, q_ref, *refs):
    if local:
        kvp_ref, kvc_ref, kvn_ref, ctx_ref, o_ref = refs
    else:
        ctx_ref, o_ref = refs
    step = pl.program_id(1)
    group = ATT_HEADS // ATT_KV_HEADS
    ctx = ctx_ref[0]
    rows = group * QBLOCK
    row_head = lax.broadcasted_iota(jnp.int32, (rows, 1), 0) // QBLOCK
    if local:
        kv_all = jnp.concatenate([kvp_ref[0], kvc_ref[0], kvn_ref[0]], axis=0)
        qi = lax.broadcasted_iota(jnp.int32, (rows, 3 * QBLOCK), 0) % QBLOCK
        kj = lax.broadcasted_iota(jnp.int32, (rows, 3 * QBLOCK), 1)
        blk = kj // QBLOCK
        in_band = (jnp.abs(kj - QBLOCK - qi) <= WINDOW)
    sinks = []
    for g in range(ATT_KV_HEADS):
        sink = jnp.zeros((rows, 1), F32)
        for i in range(group):
            sink = jnp.where(row_head == i, sink_ref[g * group + i], sink)
        sinks.append(sink)
    units = [(j, g) for j in range(nq) for g in range(ATT_KV_HEADS)]
    s_ctx, s_loc, kv_loc = {}, {}, {}
    for j, g in units:
        q = q_ref[0, j * QBLOCK:(j + 1) * QBLOCK]
        qs = jnp.concatenate(
            [q[:, (g * group + i) * HEAD_DIM:(g * group + i + 1) * HEAD_DIM] for i in range(group)],
            axis=0)
        s_ctx[j, g] = _dot_nt(qs, ctx[:, g * HEAD_DIM:(g + 1) * HEAD_DIM])
        if local:
            n = step * nq + j
            valid = in_band & ((blk != 0) | (n > 0)) & ((blk != 2) | (n < nblk - 1))
            kv_loc[j] = kv_all[j * QBLOCK:(j + 3) * QBLOCK]
            s_loc[j, g] = jnp.where(valid, _dot_nt(qs, kv_loc[j][:, g * HEAD_DIM:(g + 1) * HEAD_DIM]), NEG_INF)
    p_ctx, p_loc, p_sink = {}, {}, {}
    for u in units:
        g = u[1]
        m = jnp.maximum(jnp.max(s_ctx[u], axis=-1, keepdims=True), sinks[g])
        if local:
            m = jnp.maximum(m, jnp.max(s_loc[u], axis=-1, keepdims=True))
        p_ctx[u] = jnp.exp(s_ctx[u] - m).astype(BF16)
        if local:
            p_loc[u] = jnp.exp(s_loc[u] - m).astype(BF16)
        p_sink[u] = jnp.exp(sinks[g] - m)

    def with_ones(v):
        return jnp.concatenate([v.astype(BF16), jnp.ones(v.shape, BF16)], axis=1)

    for u in units:
        j, g = u
        acc = _dot(p_ctx[u], with_ones(ctx[:, ATT_KV + g * HEAD_DIM:ATT_KV + (g + 1) * HEAD_DIM]))
        if local:
            acc = acc + _dot(p_loc[u], with_ones(kv_loc[j][:, ATT_KV + g * HEAD_DIM:ATT_KV + (g + 1) * HEAD_DIM]))
        out = acc[:, :HEAD_DIM] / (acc[:, HEAD_DIM:] + p_sink[u])
        for i in range(group):
            h = g * group + i
            o_ref[0, j * QBLOCK:(j + 1) * QBLOCK, h * HEAD_DIM:(h + 1) * HEAD_DIM] = out[i * QBLOCK:(i + 1) * QBLOCK]


def _attention(q, kv, kv_ctx, sink, local):
    b, t, _ = q.shape
    c = kv_ctx.shape[1]
    nblk = t // QBLOCK
    nq = ATT_QBLOCKS_PER_STEP if nblk % ATT_QBLOCKS_PER_STEP == 0 else 1
    row = lambda bi, i: (bi, i, 0)
    in_specs = [
        pl.BlockSpec(memory_space=pltpu.SMEM),
        pl.BlockSpec((1, nq * QBLOCK, ATT_Q), row),
    ]
    args = [sink, q]
    if local:
        in_specs += [
            pl.BlockSpec((1, QBLOCK, 2 * ATT_KV), lambda bi, i: (bi, jnp.maximum(i * nq - 1, 0), 0)),
            pl.BlockSpec((1, nq * QBLOCK, 2 * ATT_KV), row),
            pl.BlockSpec((1, QBLOCK, 2 * ATT_KV), lambda bi, i: (bi, jnp.minimum((i + 1) * nq, nblk - 1), 0)),
        ]
        args += [kv, kv, kv]
    in_specs.append(pl.BlockSpec((1, c, 2 * ATT_KV), lambda bi, i: (bi, 0, 0)))
    args.append(kv_ctx)
    return pl.pallas_call(
        functools.partial(_attn_kernel, local, nblk, nq),
        grid=(b, nblk // nq),
        in_specs=in_specs,
        out_specs=pl.BlockSpec((1, nq * QBLOCK, ATT_Q), row),
        out_shape=jax.ShapeDtypeStruct((b, t, ATT_Q), F32),
        compiler_params=_params(2),
        name="attention",
    )(*args)


def _scan_direction(reverse, lw_cum, b, kd, kk, v, r, st_ref, y_ref):
    L = SCAN_CHUNK
    ti = lax.broadcasted_iota(jnp.int32, (L, RW_W), 0)
    if reverse:
        lw_before = jnp.where(ti == L - 1, 0.0, pltpu.roll(lw_cum, L - 1, 0))
    else:
        lw_before = jnp.where(ti == 0, 0.0, pltpu.roll(lw_cum, 1, 0))
    lw_tot = lw_cum[0:1] if reverse else lw_cum[L - 1:L]
    e_neg = jnp.exp(-lw_cum)
    e_end = jnp.exp(lw_tot - lw_cum)

    lane = lax.broadcasted_iota(jnp.int32, (L, RW_W), 1)
    si = lane % HEAD_DIM
    strict = (ti < si) if reverse else (ti > si)
    incl = (ti <= si) if reverse else (ti >= si)
    eye = ti == si
    row = lax.broadcasted_iota(jnp.int32, (RW_W, RW_W), 0)
    col = lax.broadcasted_iota(jnp.int32, (RW_W, RW_W), 1)
    blk = (row // HEAD_DIM) == (col // HEAD_DIM)

    def block_diag(x):
        return jnp.where(blk, jnp.concatenate([x] * RW_HEADS, axis=0), 0.0).astype(BF16)

    def head_transpose(x):
        xt = x.T
        return jnp.concatenate([xt[h * HEAD_DIM:(h + 1) * HEAD_DIM] for h in range(RW_HEADS)], axis=1)

    alpha = kk * jnp.exp(lw_before)
    rho = r * jnp.exp(lw_cum)
    beta_end_t = head_transpose(b * e_end)
    kappa_end_t = head_transpose(kd * e_end)
    v_bd = block_diag(v)
    alpha_bd = block_diag(alpha)
    p1 = _dot_nt(jnp.concatenate([alpha, rho], axis=0),
                 jnp.concatenate([block_diag(b * e_neg), block_diag(kd * e_neg)], axis=0))
    yield
    nn = jnp.where(strict, p1[:L, :RW_W], 0.0)
    ak = jnp.where(strict, p1[:L, RW_W:], 0.0)
    rb = jnp.where(incl, p1[L:, :RW_W], 0.0)
    rk = jnp.where(incl, p1[L:, RW_W:], 0.0)
    tinv = jnp.where(eye, 1.0, 0.0) - nn
    npow = _dot(nn, block_diag(nn))
    kv = _dot(jnp.concatenate([ak, rk, kappa_end_t], axis=0), v_bd)
    akv, rkv, kev = kv[:L], kv[L:2 * L], kv[2 * L:]
    yield
    for _ in range(int(math.log2(L)) - 2):
        both = _dot(jnp.concatenate([npow, tinv], axis=0), block_diag(npow))
        npow, tinv = both[:L], tinv + both[L:]
        yield
    tinv = tinv + _dot(tinv, block_diag(npow))
    yield
    u = _dot(jnp.concatenate([rb, beta_end_t], axis=0), block_diag(tinv))
    yield
    bz = _dot(u, jnp.concatenate([alpha_bd, block_diag(-akv)], axis=1))
    q_mat = rho - bz[:L, :RW_W]
    y0 = bz[:L, RW_W:] + rkv
    m_mat = jnp.where(eye, jnp.exp(lw_tot), 0.0) - bz[L:, :RW_W]
    c_mat = bz[L:, RW_W:] + kev
    yield
    both = _dot(jnp.concatenate([q_mat, m_mat], axis=0), block_diag(st_ref[...]))
    y_ref[...] = both[:L] + y0
    st_ref[...] = both[L:] + c_mat


def _scan_kernel(nchunk, nbatch, nsub, lw0_ref, b0_ref, kd0_ref, kkf_ref, vf_ref, rf_ref,
                 lw1_ref, b1_ref, kd1_ref, kkb_ref, vb_ref, rb_ref, s0_ref,
                 yf_ref, yb_ref, sT_ref, st_ref):
    c = pl.program_id(1)

    @pl.when(c == 0)
    def _():
        st_ref[...] = s0_ref[...]

    def chain(reverse, bi, refs, y_ref):
        order = range(nsub - 1, -1, -1) if reverse else range(nsub)
        for sub in order:
            rows = pl.ds(sub * SCAN_CHUNK, SCAN_CHUNK)
            yield from _scan_direction(reverse, *[ref[bi, rows] for ref in refs],
                                       st_ref.at[bi, int(reverse)], y_ref.at[bi, rows])

    chains = []
    for bi in range(nbatch):
        chains.append(chain(False, bi, (lw0_ref, b0_ref, kd0_ref, kkf_ref, vf_ref, rf_ref), yf_ref))
        chains.append(chain(True, bi, (lw1_ref, b1_ref, kd1_ref, kkb_ref, vb_ref, rb_ref), yb_ref))
    _round_robin(chains)

    @pl.when(c == nchunk - 1)
    def _():
        sT_ref[...] = st_ref[...]


def _scan(lw0, lw1, b0, b1, kd0, kd1, kk, v, r, s0, nbatch):
    assert SCAN_CHUNK == HEAD_DIM
    bsz, t, _ = kk.shape
    nsub = SCAN_CHUNKS_PER_STEP if (t // SCAN_CHUNK) % SCAN_CHUNKS_PER_STEP == 0 else 1
    nchunk = t // (SCAN_CHUNK * nsub)
    fwd = lambda bi, c: (bi, c, 0)
    bwd = lambda bi, c: (bi, nchunk - 1 - c, 0)
    blk = (nbatch, nsub * SCAN_CHUNK, RW_W)
    st_blk = (nbatch, N_DIR, HEAD_DIM, RW_W)
    st_map = lambda bi, c: (bi, 0, 0, 0)
    return pl.pallas_call(
        functools.partial(_scan_kernel, nchunk, nbatch, nsub),
        grid=(bsz // nbatch, nchunk),
        in_specs=[pl.BlockSpec(blk, fwd)] * 6 + [pl.BlockSpec(blk, bwd)] * 6
        + [pl.BlockSpec(st_blk, st_map)],
        out_specs=[pl.BlockSpec(blk, fwd), pl.BlockSpec(blk, bwd), pl.BlockSpec(st_blk, st_map)],
        out_shape=[jax.ShapeDtypeStruct((bsz, t, RW_W), F32),
                   jax.ShapeDtypeStruct((bsz, t, RW_W), F32),
                   jax.ShapeDtypeStruct(s0.shape, F32)],
        scratch_shapes=[pltpu.VMEM(st_blk, F32)],
        compiler_params=_params(2),
        name="rwkv_scan",
    )(lw0, b0, kd0, kk, v, r, lw1, b1, kd1, kk, v, r, s0)


FFN_SPLIT = 1
OUT_SUBTILES = 2


def _out_rows(rows, x_ref, sg_ref, att_ref, yf_ref, yb_ref, r_ref, kd0_ref, kd1_ref, v_ref, gd_ref,
              gt1_ref, sh2_ref, sc2_ref, gt2_ref, ng_ref, lng_ref, lnb_ref, rk_ref, g2_ref,
              wo_ref, wgu_ref, wd_ref, o_ref):
    hsum = _head_sum_matrix(RW_W)
    y = yf_ref[0, rows] + yb_ref[0, rows]
    mu = _dot_exact_rhs(y, hsum) * (1.0 / HEAD_DIM)
    bonus = _dot_exact_rhs(r_ref[0, rows] * (kd0_ref[0, rows] + kd1_ref[0, rows]) * rk_ref[...], hsum, 2) * v_ref[0, rows]
    gate = _dot(jax.nn.sigmoid(gd_ref[0, rows]), g2_ref[...])
    y_mix = _dot(sg_ref[0, rows], wo_ref[:A_W]) + _dot(att_ref[0, rows], wo_ref[A_W:A_W + ATT_Q])
    yield
    yc = y - mu
    var = _dot_exact_rhs(yc * yc, hsum, 2) * (1.0 / HEAD_DIM)
    yield
    yn = yc * lax.rsqrt(var + GN_EPS) * lng_ref[...] + lnb_ref[...]
    o_rw = (yn + bonus) * gate
    y_mix = y_mix + _dot(o_rw, wo_ref[A_W + ATT_Q:])
    yield
    x1 = x_ref[0, rows] + gt1_ref[0] * _rms_norm(y_mix, ng_ref[1:2])
    h2 = (_rms_norm(x1, ng_ref[2:3]) * (1.0 + sc2_ref[0]) + sh2_ref[0]).astype(BF16)
    fc = D_FF // FFN_SPLIT
    acc = None
    for j in range(FFN_SPLIT):
        gg = jnp.dot(h2, wgu_ref[:, j * fc:(j + 1) * fc], preferred_element_type=F32)
        uu = jnp.dot(h2, wgu_ref[:, D_FF + j * fc:D_FF + (j + 1) * fc], preferred_element_type=F32)
        yield
        part = _dot(gg * jax.nn.sigmoid(gg) * uu, wd_ref[j * fc:(j + 1) * fc])
        acc = part if acc is None else acc + part
        yield
    o_ref[0, rows] = x1 + gt2_ref[0] * _rms_norm(acc, ng_ref[3:4])


def _out_kernel(nsub, *refs):
    tm = refs[0].shape[1]
    sub = tm // nsub
    _round_robin([_out_rows(pl.ds(i * sub, sub), *refs) for i in range(nsub)])


def _mix_out_ffn(x, o_sg, o_att, yf, yb, r, kd0, kd1, v, gd, gt1, sh2, sc2, gt2, norm_g,
                 lnx_g, lnx_b, rk, g2_bf16, wo_bf16, wgu_bf16, wd_bf16, layer, tm):
    b, t, d = x.shape
    row = lambda bi, i: (bi, i, 0)
    per_b = lambda bi, i: (bi, 0, 0)
    const2 = lambda bi, i: (0, 0)
    single = pl.Buffered(1)
    stacked = lambda bi, i: (layer, 0, 0)
    in_specs = [
        pl.BlockSpec((1, tm, d), row),
        pl.BlockSpec((1, tm, A_W), row),
        pl.BlockSpec((1, tm, ATT_Q), row),
    ] + [pl.BlockSpec((1, tm, RW_W), row)] * 6 + [
        pl.BlockSpec((1, tm, G_LORA), row),
    ] + [pl.BlockSpec((1, 1, d), per_b)] * 4 + [
        pl.BlockSpec((4, d), const2),
        pl.BlockSpec((1, RW_W), const2),
        pl.BlockSpec((1, RW_W), const2),
        pl.BlockSpec((1, RW_W), const2),
        pl.BlockSpec((G_LORA, RW_W), const2),
        pl.BlockSpec((None, MIX_W, d), stacked, pipeline_mode=single),
        pl.BlockSpec((None, d, 2 * D_FF), stacked, pipeline_mode=single),
        pl.BlockSpec((None, D_FF, d), stacked, pipeline_mode=single),
    ]
    return pl.pallas_call(
        functools.partial(_out_kernel, OUT_SUBTILES if tm % (OUT_SUBTILES * V7X_SUBLANES) == 0 else 1),
        grid=(b, t // tm),
        in_specs=in_specs,
        out_specs=pl.BlockSpec((1, tm, d), row),
        out_shape=jax.ShapeDtypeStruct((b, t, d), F32),
        compiler_params=_params(2),
        name="mix_out_ffn",
    )(x, o_sg, o_att, yf, yb, r, kd0, kd1, v, gd, gt1, sh2, sc2, gt2, norm_g,
      lnx_g, lnx_b, rk, g2_bf16, wo_bf16, wgu_bf16, wd_bf16)


def _rope_tables(t):
    pos = np.arange(t)
    inv = np.float32(ROPE_THETA) ** (-np.arange(ROPE_FREQS, dtype=np.float32) / np.float32(ROPE_FREQS))
    ang_r = (pos // GRID_W).astype(np.float32)[:, None] * inv
    ang_c = (pos % GRID_W).astype(np.float32)[:, None] * inv
    cos = np.concatenate([np.cos(ang_r)] * 2 + [np.cos(ang_c)] * 2, axis=-1)
    sin = np.concatenate([-np.sin(ang_r), np.sin(ang_r), -np.sin(ang_c), np.sin(ang_c)], axis=-1)
    return (jnp.asarray(np.concatenate([cos, cos], axis=-1), F32),
            jnp.asarray(np.concatenate([sin, sin], axis=-1), F32))


def _row_tile(t, target):
    tm = min(t, target)
    while t % tm:
        tm //= 2
    return tm


def kernel(x, c, ctx, c_ctx, w_mod, b_mod, norm_g, w_in, w_out, sg_ln_g, sg_ln_b, sg_w, sg_b,
           attn_sink, rw_mu, rw_w0, rw_w2, rw_a0, rw_a2, rw_kk, rw_ka, rw_rk, rw_g2,
           rw_lnx_g, rw_lnx_b, w_gu, w_down):
    bsz, t, d = x.shape
    clen = ctx.shape[1]
    depth = w_mod.shape[0]

    rows = -(-(bsz + 1) // V7X_SUBLANES) * V7X_SUBLANES
    c_rows = jnp.zeros((rows, d), F32).at[:bsz].set(c).at[bsz].set(c_ctx)
    mods = _modulation(c_rows, w_mod, b_mod)

    cos, sin = _rope_tables(t)
    cos_c, sin_c = cos[:clen], sin[:clen]
    s_zero = jnp.zeros((bsz, N_DIR, HEAD_DIM, RW_W), F32)

    tm_proj = _row_tile(t, 1024)
    tm_out = _row_tile(t, 512)
    tc_out = _row_tile(clen, 256)
    nb_scan = SCAN_BATCH if bsz % SCAN_BATCH == 0 else 1

    w_in_b, w_out_b, w_gu_b, w_down_b = (w.astype(BF16) for w in (w_in, w_out, w_gu, w_down))

    xc = ctx
    for l in range(depth):
        last = l == depth - 1
        lat = [mods[l, :bsz, i * d:(i + 1) * d][:, None, :] for i in range(6)]
        cm = [jnp.broadcast_to(mods[l, bsz, i * d:(i + 1) * d][None, None, :], (bsz, 1, d))
              for i in range(6)]
        g = norm_g[l]
        w2_b, a2_b = rw_w2[l].astype(BF16), rw_a2[l].astype(BF16)
        prep_params = (rw_mu[l][None], rw_kk[l][None], rw_ka[l][None], rw_w0[l][:, None, :], w2_b,
                       rw_a0[l][:, None, :], a2_b,
                       sg_ln_g[l][None], sg_ln_b[l][None], sg_w[l].astype(BF16),
                       jnp.repeat(sg_b[l].T, HEAD_DIM, axis=1))

        (oc_sg, c_q, c_kv, c_r, c_gd, c_v, c_kk, c_lw0, c_lw1, c_b0, c_b1, c_kd0, c_kd1) = _proj_prep(
            xc, g[0:1], cm[0], cm[1], w_in_b, l, cos_c, sin_c, *prep_params, False, clen)
        c_yf, c_yb, s_ctx = _scan(c_lw0, c_lw1, c_b0, c_b1, c_kd0, c_kd1, c_kk, c_v, c_r, s_zero, nb_scan)

        (o_sg, p_q, p_kv, r, gd, v, kk, lw0, lw1, b0, b1, kd0, kd1) = _proj_prep(
            x, g[0:1], lat[0], lat[1], w_in_b, l, cos, sin, *prep_params, True, tm_proj)
        yf, yb, _ = _scan(lw0, lw1, b0, b1, kd0, kd1, kk, v, r, s_ctx, nb_scan)

        o_att = _attention(p_q, p_kv, c_kv, attn_sink[l], True)

        out_params = (g, rw_lnx_g[l][None], rw_lnx_b[l][None], rw_rk[l].reshape(1, RW_W),
                      rw_g2[l].astype(BF16), w_out_b, w_gu_b, w_down_b, l)
        x_new = _mix_out_ffn(x, o_sg, o_att, yf, yb, r, kd0, kd1, v, gd,
                             lat[2], lat[3], lat[4], lat[5], *out_params, tm_out)
        if not last:
            oc_att = _attention(c_q, c_kv, c_kv, attn_sink[l], False)
            xc = _mix_out_ffn(xc, oc_sg, oc_att, c_yf, c_yb, c_r, c_kd0, c_kd1, c_v, c_gd,
                              cm[2], cm[3], cm[4], cm[5], *out_params, tc_out)
        x = x_new
    return x
```

```python
import functools
import math

import jax
import jax.numpy as jnp
import numpy as np
from jax import lax
from jax.experimental import pallas as pl
from jax.experimental.pallas import tpu as pltpu

F32 = jnp.float32
BF16 = jnp.bfloat16

D_MODEL = 1024
HEAD_DIM = 64
GRID_W = 64
A_GROUPS = 4
A_W = A_GROUPS * HEAD_DIM
CHUNK = 128
ATT_HEADS = 8
ATT_KV_HEADS = 2
ATT_Q = ATT_HEADS * HEAD_DIM
ATT_KV = ATT_KV_HEADS * HEAD_DIM
WINDOW = 128
QBLOCK = 128
ROPE_THETA = 10000.0
ROPE_FREQS = HEAD_DIM // 4
NEG_INF = -1e30
ATT_QBLOCKS_PER_STEP = 4
ATT_SCALE = HEAD_DIM ** -0.5
RW_HEADS = 4
RW_W = RW_HEADS * HEAD_DIM
W_LORA = 64
A_LORA = 64
G_LORA = 128
N_DIR = 2
GN_EPS = 64e-5
MIX_W = A_W + ATT_Q + RW_W
D_FF = -(-8 * D_MODEL // (3 * 256)) * 256
RMS_EPS = 1e-6
LN_EPS = 1e-5
OFF_Q = 2 * A_W
OFF_R = OFF_Q + ATT_Q
OFF_CTX = OFF_R + RW_W + G_LORA
OFF_RWK = OFF_CTX + 2 * ATT_KV
IN_COLS = OFF_RWK + 2 * RW_W + N_DIR * (W_LORA + A_LORA)
MU_HEAD = RW_W + G_LORA
RWK_W = 2 * RW_W + N_DIR * (W_LORA + A_LORA)

V7X_LANES = 128
V7X_SUBLANES = 8
V7X_VMEM_LIMIT_BYTES = 56 * 1024 * 1024

SCAN_CHUNK = 64
SCAN_CHUNKS_PER_STEP = 4
SCAN_BATCH = 4
LOG_DECAY_SCALE = -math.exp(-0.5)


def _params(n_grid):
    return pltpu.CompilerParams(
        dimension_semantics=("arbitrary",) * n_grid, vmem_limit_bytes=V7X_VMEM_LIMIT_BYTES)


def _dot(a, b):
    return jnp.dot(a.astype(BF16), b.astype(BF16), preferred_element_type=F32)


def _dot_nt(a, b):
    return lax.dot_general(a.astype(BF16), b.astype(BF16), (((1,), (1,)), ((), ())),
                           preferred_element_type=F32)


def _split_bf16(x, n):
    parts = []
    for _ in range(n - 1):
        h = x.astype(BF16)
        parts.append(h)
        x = x - h.astype(F32)
    parts.append(x.astype(BF16))
    return parts


def _dot_exact_rhs(x, m_bf16, n=3):
    acc = None
    for p in _split_bf16(x, n):
        t = jnp.dot(p, m_bf16, preferred_element_type=F32)
        acc = t if acc is None else acc + t
    return acc


def _dot_exact_lhs(m_bf16, x, n=3):
    acc = None
    for p in _split_bf16(x, n):
        t = jnp.dot(m_bf16, p, preferred_element_type=F32)
        acc = t if acc is None else acc + t
    return acc


def _head_sum_matrix(width):
    i = lax.broadcasted_iota(jnp.int32, (width, width), 0) // HEAD_DIM
    j = lax.broadcasted_iota(jnp.int32, (width, width), 1) // HEAD_DIM
    return jnp.where(i == j, 1.0, 0.0).astype(BF16)


def _round_robin(chains):
    while chains:
        chains = [g for g in chains if next(g, True) is None]


def _rms_norm(x, g):
    return x * lax.rsqrt(jnp.mean(x * x, axis=-1, keepdims=True) + RMS_EPS) * g


def _tile_lanes(x, width):
    reps = width // x.shape[-1]
    return x if reps == 1 else jnp.concatenate([x] * reps, axis=-1)


def _rope(x, cos, sin_signed):
    w = x.shape[-1]
    lane = lax.broadcasted_iota(jnp.int32, x.shape, 1)
    lo = (lane % (2 * ROPE_FREQS)) < ROPE_FREQS
    partner = jnp.where(lo, pltpu.roll(x, w - ROPE_FREQS, 1), pltpu.roll(x, ROPE_FREQS, 1))
    return x * _tile_lanes(cos, w) + partner * _tile_lanes(sin_signed, w)


def _mod_kernel(c_ref, w_ref, b_ref, o_ref):
    c = c_ref[...]
    act = c * jax.nn.sigmoid(c)
    o_ref[0] = _dot(act, w_ref[0]) + b_ref[0]


def _modulation(c_rows, w_mod, b_mod):
    depth, d, n = w_mod.shape
    rows = c_rows.shape[0]
    tn = n // 4
    return pl.pallas_call(
        _mod_kernel,
        grid=(depth, n // tn),
        in_specs=[
            pl.BlockSpec((rows, d), lambda l, j: (0, 0)),
            pl.BlockSpec((1, d, tn), lambda l, j: (l, 0, j)),
            pl.BlockSpec((1, 1, tn), lambda l, j: (l, 0, j)),
        ],
        out_specs=pl.BlockSpec((1, rows, tn), lambda l, j: (l, 0, j)),
        out_shape=jax.ShapeDtypeStruct((depth, rows, n), F32),
        compiler_params=_params(2),
        name="modulation",
    )(c_rows, w_mod, b_mod.reshape(depth, 1, n))


def _shift_rows(x, k):
    return pltpu.roll(x, k % x.shape[0], 0)


def _token_shift(cur, prev_rows, next_rows, grid_mode, first, last):
    tm, width = cur.shape
    ch = lax.broadcasted_iota(jnp.int32, (tm, width), 1)
    pos = lax.broadcasted_iota(jnp.int32, (tm, width), 0)
    if grid_mode:
        qw = width // 4
        col = pos % GRID_W
        left = jnp.where(col == 0, 0.0, _shift_rows(cur, 1))
        right = jnp.where(col == GRID_W - 1, 0.0, _shift_rows(cur, -1))
        up = jnp.concatenate([prev_rows, cur[:tm - GRID_W]], axis=0)
        up = jnp.where((pos < GRID_W) & first, 0.0, up)
        down = jnp.concatenate([cur[GRID_W:], next_rows], axis=0)
        down = jnp.where((pos >= tm - GRID_W) & last, 0.0, down)
        return jnp.where(ch < qw, left, jnp.where(ch < 2 * qw, right, jnp.where(ch < 3 * qw, up, down)))
    prev = jnp.where(pos == 0, 0.0, _shift_rows(cur, 1))
    nxt = jnp.where(pos == tm - 1, 0.0, _shift_rows(cur, -1))
    return jnp.where(ch < width // 2, prev, nxt)


def _lane_slab(lo, hi):
    return (lo // V7X_LANES) * V7X_LANES, -(-hi // V7X_LANES) * V7X_LANES


def _halo_columns(hb, w_ref, off, width, quarter):
    qw = width // 4
    lo, hi = _lane_slab(quarter * qw, (quarter + 1) * qw)
    p = jnp.dot(hb, w_ref[:, off + lo:off + hi], preferred_element_type=F32)
    parts = []
    if lo:
        parts.append(jnp.zeros((p.shape[0], lo), F32))
    parts.append(p)
    if hi < width:
        parts.append(jnp.zeros((p.shape[0], width - hi), F32))
    return jnp.concatenate(parts, axis=1)


def _spatial_gating(z, ln_g, ln_b, w_ref, bias):
    z = jax.nn.gelu(z, approximate=True)
    u, v = z[:, :A_W], z[:, A_W:]
    mu = jnp.mean(v, axis=-1, keepdims=True)
    var = jnp.mean(jnp.square(v - mu), axis=-1, keepdims=True)
    vn = ((v - mu) * lax.rsqrt(var + LN_EPS) * ln_g + ln_b).astype(BF16)
    outs = []
    for c in range(z.shape[0] // CHUNK):
        rows = slice(c * CHUNK, (c + 1) * CHUNK)
        mixed = jnp.concatenate(
            [jnp.dot(w_ref[gi], vn[rows, gi * HEAD_DIM:(gi + 1) * HEAD_DIM],
                     preferred_element_type=F32) for gi in range(A_GROUPS)], axis=-1)
        outs.append(u[rows] * (mixed + bias))
    return jnp.concatenate(outs, axis=0)


def _proj_prep_kernel(grid_mode, nblk, *refs):
    if grid_mode:
        x_ref, xp_ref, xn_ref = refs[:3]
        refs = refs[3:]
    else:
        x_ref = refs[0]
        refs = refs[1:]
    (g_ref, sh_ref, sc_ref, w_ref, cos_ref, sin_ref,
     mu_rg_ref, mu_rwk_ref, kkp_ref, ka_ref, w0_ref, w2_ref, a0_ref, a2_ref,
     sg_g_ref, sg_b_ref, sg_w_ref, sg_bias_ref,
     osg_ref, pq_ref, pkv_ref,
     r_ref, gd_ref, v_ref, kk_ref, lw0_ref, lw1_ref, b0_ref, b1_ref, kd0_ref, kd1_ref) = refs
    i = pl.program_id(1)
    first, last = i == 0, i == nblk - 1

    def normed(x):
        return (_rms_norm(x, g_ref[...]) * (1.0 + sc_ref[0]) + sh_ref[0]).astype(BF16)

    def project(hb, start, width):
        return jnp.dot(hb, w_ref[:, start:start + width], preferred_element_type=F32)

    hb = normed(x_ref[0])
    rg = project(hb, OFF_R, MU_HEAD)
    rwk = project(hb, OFF_RWK, RWK_W)
    if grid_mode:
        hp, hn = normed(xp_ref[0]), normed(xn_ref[0])
        halos = [_halo_columns(hp, w_ref, OFF_R, MU_HEAD, 2), _halo_columns(hn, w_ref, OFF_R, MU_HEAD, 3),
                 _halo_columns(hp, w_ref, OFF_RWK, RWK_W, 2), _halo_columns(hn, w_ref, OFF_RWK, RWK_W, 3)]
    q = project(hb, OFF_Q, ATT_Q)
    kv = project(hb, OFF_CTX, 2 * ATT_KV)
    z_a = project(hb, 0, OFF_Q)
    if grid_mode:
        q = _rope(q, cos_ref[...], sin_ref[...])
        kv = jnp.concatenate([_rope(kv[:, :ATT_KV], cos_ref[...], sin_ref[...]), kv[:, ATT_KV:]], axis=-1)
    pq_ref[0] = q * ATT_SCALE
    pkv_ref[0] = kv
    if grid_mode:
        rg_s = _token_shift(rg, halos[0], halos[1], True, first, last)
        rwk_s = _token_shift(rwk, halos[2], halos[3], True, first, last)
    else:
        rg_s = _token_shift(rg, None, None, False, first, last)
        rwk_s = _token_shift(rwk, None, None, False, first, last)
    rg = rg + (rg_s - rg) * mu_rg_ref[...]
    rwk = rwk + (rwk_s - rwk) * mu_rwk_ref[...]
    r_ref[0] = rg[:, :RW_W]
    gd_ref[0] = rg[:, RW_W:]
    k = rwk[:, :RW_W]
    v_ref[0] = rwk[:, RW_W:2 * RW_W]
    kk = k * kkp_ref[...]
    norm = jnp.sqrt(_dot_exact_rhs(kk * kk, _head_sum_matrix(RW_W)))
    kk = kk / jnp.maximum(norm, 1e-12)
    kk_ref[0] = kk
    base = 2 * RW_W
    lws = []
    for d, (b_ref, kd_ref) in enumerate(((b0_ref, kd0_ref), (b1_ref, kd1_ref))):
        wd = rwk[:, base + d * W_LORA:base + (d + 1) * W_LORA]
        ad = rwk[:, base + N_DIR * W_LORA + d * A_LORA:base + N_DIR * W_LORA + (d + 1) * A_LORA]
        z = w0_ref[d] + _dot(jnp.tanh(wd), w2_ref[d])
        lws.append(LOG_DECAY_SCALE * jax.nn.sigmoid(z))
        a = jax.nn.sigmoid(a0_ref[d] + _dot(ad, a2_ref[d]))
        b_ref[0] = kk * a
        kd_ref[0] = k * (1.0 + (a - 1.0) * ka_ref[...])
    osg_ref[0] = _spatial_gating(z_a, sg_g_ref[...], sg_b_ref[...], sg_w_ref, sg_bias_ref[...])
    ci = lax.broadcasted_iota(jnp.int32, (SCAN_CHUNK, SCAN_CHUNK), 0)
    cj = lax.broadcasted_iota(jnp.int32, (SCAN_CHUNK, SCAN_CHUNK), 1)
    for d, lw_ref in enumerate((lw0_ref, lw1_ref)):
        cum = jnp.where((ci <= cj) if d else (ci >= cj), 1.0, 0.0).astype(BF16)
        for c in range(lws[d].shape[0] // SCAN_CHUNK):
            rows = slice(c * SCAN_CHUNK, (c + 1) * SCAN_CHUNK)
            lw_ref[0, rows] = _dot_exact_lhs(cum, lws[d][rows])


def _proj_prep(x, g, sh, sc, w_bf16, layer, cos, sin_signed, mu, kkp, ka, w0, w2_bf16, a0, a2_bf16,
               sg_ln_g, sg_ln_b, sg_w_bf16, sg_bias, grid_mode, tm):
    b, t, d = x.shape
    nblk = t // tm
    assert tm % SCAN_CHUNK == 0 and tm % CHUNK == 0 and (grid_mode or nblk == 1)
    row = lambda bi, i: (bi, i, 0)
    per_b = lambda bi, i: (bi, 0, 0)
    const2 = lambda bi, i: (0, 0)
    const3 = lambda bi, i: (0, 0, 0)
    in_specs, args = [pl.BlockSpec((1, tm, d), row)], [x]
    if grid_mode:
        per = tm // GRID_W
        nrow = t // GRID_W
        in_specs += [
            pl.BlockSpec((1, GRID_W, d), lambda bi, i: (bi, jnp.maximum(i * per - 1, 0), 0)),
            pl.BlockSpec((1, GRID_W, d), lambda bi, i: (bi, jnp.minimum((i + 1) * per, nrow - 1), 0)),
        ]
        args += [x, x]
    in_specs += [
        pl.BlockSpec((1, d), const2),
        pl.BlockSpec((1, 1, d), per_b),
        pl.BlockSpec((1, 1, d), per_b),
        pl.BlockSpec((None, d, IN_COLS), lambda bi, i: (layer, 0, 0)),
        pl.BlockSpec((tm, V7X_LANES), lambda bi, i: (i, 0)),
        pl.BlockSpec((tm, V7X_LANES), lambda bi, i: (i, 0)),
        pl.BlockSpec((1, MU_HEAD), const2),
        pl.BlockSpec((1, RWK_W), const2),
        pl.BlockSpec((1, RW_W), const2),
        pl.BlockSpec((1, RW_W), const2),
        pl.BlockSpec((N_DIR, 1, RW_W), const3),
        pl.BlockSpec((N_DIR, W_LORA, RW_W), const3),
        pl.BlockSpec((N_DIR, 1, RW_W), const3),
        pl.BlockSpec((N_DIR, A_LORA, RW_W), const3),
        pl.BlockSpec((1, A_W), const2),
        pl.BlockSpec((1, A_W), const2),
        pl.BlockSpec((A_GROUPS, CHUNK, CHUNK), const3),
        pl.BlockSpec((CHUNK, A_W), const2),
    ]
    args += [g, sh, sc, w_bf16, cos, sin_signed, mu[:, :MU_HEAD], mu[:, MU_HEAD:], kkp, ka,
             w0, w2_bf16, a0, a2_bf16, sg_ln_g, sg_ln_b, sg_w_bf16, sg_bias]
    widths = [A_W, ATT_Q, 2 * ATT_KV, RW_W, G_LORA] + [RW_W] * 8
    return pl.pallas_call(
        functools.partial(_proj_prep_kernel, grid_mode, nblk),
        grid=(b, nblk),
        in_specs=in_specs,
        out_specs=[pl.BlockSpec((1, tm, w), row) for w in widths],
        out_shape=[jax.ShapeDtypeStruct((b, t, w), F32) for w in widths],
        compiler_params=_params(2),
        name="proj_prep",
    )(*args)


def _attn_kernel(local, nblk, nq, sink_ref, q_ref, *refs):
    if local:
        kvp_ref, kvc_ref, kvn_ref, ctx_ref, o_ref = refs
    else:
        ctx_ref, o_ref = refs
    step = pl.program_id(1)
    group = ATT_HEADS // ATT_KV_HEADS
    ctx = ctx_ref[0]
    rows = group * QBLOCK
    row_head = lax.broadcasted_iota(jnp.int32, (rows, 1), 0) // QBLOCK
    if local:
        kv_all = jnp.concatenate([kvp_ref[0], kvc_ref[0], kvn_ref[0]], axis=0)
        qi = lax.broadcasted_iota(jnp.int32, (rows, 3 * QBLOCK), 0) % QBLOCK
        kj = lax.broadcasted_iota(jnp.int32, (rows, 3 * QBLOCK), 1)
        blk = kj // QBLOCK
        in_band = (jnp.abs(kj - QBLOCK - qi) <= WINDOW)
    sinks = []
    for g in range(ATT_KV_HEADS):
        sink = jnp.zeros((rows, 1), F32)
        for i in range(group):
            sink = jnp.where(row_head == i, sink_ref[g * group + i], sink)
        sinks.append(sink)
    units = [(j, g) for j in range(nq) for g in range(ATT_KV_HEADS)]
    s_ctx, s_loc, kv_loc = {}, {}, {}
    for j, g in units:
        q = q_ref[0, j * QBLOCK:(j + 1) * QBLOCK]
        qs = jnp.concatenate(
            [q[:, (g * group + i) * HEAD_DIM:(g * group + i + 1) * HEAD_DIM] for i in range(group)],
            axis=0)
        s_ctx[j, g] = _dot_nt(qs, ctx[:, g * HEAD_DIM:(g + 1) * HEAD_DIM])
        if local:
            n = step * nq + j
            valid = in_band & ((blk != 0) | (n > 0)) & ((blk != 2) | (n < nblk - 1))
            kv_loc[j] = kv_all[j * QBLOCK:(j + 3) * QBLOCK]
            s_loc[j, g] = jnp.where(valid, _dot_nt(qs, kv_loc[j][:, g * HEAD_DIM:(g + 1) * HEAD_DIM]), NEG_INF)
    p_ctx, p_loc, p_sink = {}, {}, {}
    for u in units:
        g = u[1]
        m = jnp.maximum(jnp.max(s_ctx[u], axis=-1, keepdims=True), sinks[g])
        if local:
            m = jnp.maximum(m, jnp.max(s_loc[u], axis=-1, keepdims=True))
        p_ctx[u] = jnp.exp(s_ctx[u] - m).astype(BF16)
        if local:
            p_loc[u] = jnp.exp(s_loc[u] - m).astype(BF16)
        p_sink[u] = jnp.exp(sinks[g] - m)

    def with_ones(v):
        return jnp.concatenate([v.astype(BF16), jnp.ones(v.shape, BF16)], axis=1)

    for u in units:
        j, g = u
        acc = _dot(p_ctx[u], with_ones(ctx[:, ATT_KV + g * HEAD_DIM:ATT_KV + (g + 1) * HEAD_DIM]))
        if local:
            acc = acc + _dot(p_loc[u], with_ones(kv_loc[j][:, ATT_KV + g * HEAD_DIM:ATT_KV + (g + 1) * HEAD_DIM]))
        out = acc[:, :HEAD_DIM] / (acc[:, HEAD_DIM:] + p_sink[u])
        for i in range(group):
            h = g * group + i
            o_ref[0, j * QBLOCK:(j + 1) * QBLOCK, h * HEAD_DIM:(h + 1) * HEAD_DIM] = out[i * QBLOCK:(i + 1) * QBLOCK]


def _attention(q, kv, kv_ctx, sink, local):
    b, t, _ = q.shape
    c = kv_ctx.shape[1]
    nblk = t // QBLOCK
    nq = ATT_QBLOCKS_PER_STEP if nblk % ATT_QBLOCKS_PER_STEP == 0 else 1
    row = lambda bi, i: (bi, i, 0)
    in_specs = [
        pl.BlockSpec(memory_space=pltpu.SMEM),
        pl.BlockSpec((1, nq * QBLOCK, ATT_Q), row),
    ]
    args = [sink, q]
    if local:
        in_specs += [
            pl.BlockSpec((1, QBLOCK, 2 * ATT_KV), lambda bi, i: (bi, jnp.maximum(i * nq - 1, 0), 0)),
            pl.BlockSpec((1, nq * QBLOCK, 2 * ATT_KV), row),
            pl.BlockSpec((1, QBLOCK, 2 * ATT_KV), lambda bi, i: (bi, jnp.minimum((i + 1) * nq, nblk - 1), 0)),
        ]
        args += [kv, kv, kv]
    in_specs.append(pl.BlockSpec((1, c, 2 * ATT_KV), lambda bi, i: (bi, 0, 0)))
    args.append(kv_ctx)
    return pl.pallas_call(
        functools.partial(_attn_kernel, local, nblk, nq),
        grid=(b, nblk // nq),
        in_specs=in_specs,
        out_specs=pl.BlockSpec((1, nq * QBLOCK, ATT_Q), row),
        out_shape=jax.ShapeDtypeStruct((b, t, ATT_Q), F32),
        compiler_params=_params(2),
        name="attention",
    )(*args)


def _scan_direction(reverse, lw_cum, b, kd, kk, v, r, st_ref, y_ref):
    L = SCAN_CHUNK
    ti = lax.broadcasted_iota(jnp.int32, (L, RW_W), 0)
    if reverse:
        lw_before = jnp.where(ti == L - 1, 0.0, pltpu.roll(lw_cum, L - 1, 0))
    else:
        lw_before = jnp.where(ti == 0, 0.0, pltpu.roll(lw_cum, 1, 0))
    lw_tot = lw_cum[0:1] if reverse else lw_cum[L - 1:L]
    e_neg = jnp.exp(-lw_cum)
    e_end = jnp.exp(lw_tot - lw_cum)

    lane = lax.broadcasted_iota(jnp.int32, (L, RW_W), 1)
    si = lane % HEAD_DIM
    strict = (ti < si) if reverse else (ti > si)
    incl = (ti <= si) if reverse else (ti >= si)
    eye = ti == si
    row = lax.broadcasted_iota(jnp.int32, (RW_W, RW_W), 0)
    col = lax.broadcasted_iota(jnp.int32, (RW_W, RW_W), 1)
    blk = (row // HEAD_DIM) == (col // HEAD_DIM)

    def block_diag(x):
        return jnp.where(blk, jnp.concatenate([x] * RW_HEADS, axis=0), 0.0).astype(BF16)

    def head_transpose(x):
        xt = x.T
        return jnp.concatenate([xt[h * HEAD_DIM:(h + 1) * HEAD_DIM] for h in range(RW_HEADS)], axis=1)

    alpha = kk * jnp.exp(lw_before)
    rho = r * jnp.exp(lw_cum)
    beta_end_t = head_transpose(b * e_end)
    kappa_end_t = head_transpose(kd * e_end)
    v_bd = block_diag(v)
    alpha_bd = block_diag(alpha)
    p1 = _dot_nt(jnp.concatenate([alpha, rho], axis=0),
                 jnp.concatenate([block_diag(b * e_neg), block_diag(kd * e_neg)], axis=0))
    yield
    nn = jnp.where(strict, p1[:L, :RW_W], 0.0)
    ak = jnp.where(strict, p1[:L, RW_W:], 0.0)
    rb = jnp.where(incl, p1[L:, :RW_W], 0.0)
    rk = jnp.where(incl, p1[L:, RW_W:], 0.0)
    tinv = jnp.where(eye, 1.0, 0.0) - nn
    npow = _dot(nn, block_diag(nn))
    kv = _dot(jnp.concatenate([ak, rk, kappa_end_t], axis=0), v_bd)
    akv, rkv, kev = kv[:L], kv[L:2 * L], kv[2 * L:]
    yield
    for _ in range(int(math.log2(L)) - 2):
        both = _dot(jnp.concatenate([npow, tinv], axis=0), block_diag(npow))
        npow, tinv = both[:L], tinv + both[L:]
        yield
    tinv = tinv + _dot(tinv, block_diag(npow))
    yield
    u = _dot(jnp.concatenate([rb, beta_end_t], axis=0), block_diag(tinv))
    yield
    bz = _dot(u, jnp.concatenate([alpha_bd, block_diag(-akv)], axis=1))
    q_mat = rho - bz[:L, :RW_W]
    y0 = bz[:L, RW_W:] + rkv
    m_mat = jnp.where(eye, jnp.exp(lw_tot), 0.0) - bz[L:, :RW_W]
    c_mat = bz[L:, RW_W:] + kev
    yield
    both = _dot(jnp.concatenate([q_mat, m_mat], axis=0), block_diag(st_ref[...]))
    y_ref[...] = both[:L] + y0
    st_ref[...] = both[L:] + c_mat


def _scan_kernel(nchunk, nbatch, nsub, lw0_ref, b0_ref, kd0_ref, kkf_ref, vf_ref, rf_ref,
                 lw1_ref, b1_ref, kd1_ref, kkb_ref, vb_ref, rb_ref, s0_ref,
                 yf_ref, yb_ref, sT_ref, st_ref):
    c = pl.program_id(1)

    @pl.when(c == 0)
    def _():
        st_ref[...] = s0_ref[...]

    def chain(reverse, bi, refs, y_ref):
        order = range(nsub - 1, -1, -1) if reverse else range(nsub)
        for sub in order:
            rows = pl.ds(sub * SCAN_CHUNK, SCAN_CHUNK)
            yield from _scan_direction(reverse, *[ref[bi, rows] for ref in refs],
                                       st_ref.at[bi, int(reverse)], y_ref.at[bi, rows])

    chains = []
    for bi in range(nbatch):
        chains.append(chain(False, bi, (lw0_ref, b0_ref, kd0_ref, kkf_ref, vf_ref, rf_ref), yf_ref))
        chains.append(chain(True, bi, (lw1_ref, b1_ref, kd1_ref, kkb_ref, vb_ref, rb_ref), yb_ref))
    _round_robin(chains)

    @pl.when(c == nchunk - 1)
    def _():
        sT_ref[...] = st_ref[...]


def _scan(lw0, lw1, b0, b1, kd0, kd1, kk, v, r, s0, nbatch):
    assert SCAN_CHUNK == HEAD_DIM
    bsz, t, _ = kk.shape
    nsub = SCAN_CHUNKS_PER_STEP if (t // SCAN_CHUNK) % SCAN_CHUNKS_PER_STEP == 0 else 1
    nchunk = t // (SCAN_CHUNK * nsub)
    fwd = lambda bi, c: (bi, c, 0)
    bwd = lambda bi, c: (bi, nchunk - 1 - c, 0)
    blk = (nbatch, nsub * SCAN_CHUNK, RW_W)
    st_blk = (nbatch, N_DIR, HEAD_DIM, RW_W)
    st_map = lambda bi, c: (bi, 0, 0, 0)
    return pl.pallas_call(
        functools.partial(_scan_kernel, nchunk, nbatch, nsub),
        grid=(bsz // nbatch, nchunk),
        in_specs=[pl.BlockSpec(blk, fwd)] * 6 + [pl.BlockSpec(blk, bwd)] * 6
        + [pl.BlockSpec(st_blk, st_map)],
        out_specs=[pl.BlockSpec(blk, fwd), pl.BlockSpec(blk, bwd), pl.BlockSpec(st_blk, st_map)],
        out_shape=[jax.ShapeDtypeStruct((bsz, t, RW_W), F32),
                   jax.ShapeDtypeStruct((bsz, t, RW_W), F32),
                   jax.ShapeDtypeStruct(s0.shape, F32)],
        scratch_shapes=[pltpu.VMEM(st_blk, F32)],
        compiler_params=_params(2),
        name="rwkv_scan",
    )(lw0, b0, kd0, kk, v, r, lw1, b1, kd1, kk, v, r, s0)


FFN_SPLIT = 1
OUT_SUBTILES = 2


def _out_rows(rows, x_ref, sg_ref, att_ref, yf_ref, yb_ref, r_ref, kd0_ref, kd1_ref, v_ref, gd_ref,
              gt1_ref, sh2_ref, sc2_ref, gt2_ref, ng_ref, lng_ref, lnb_ref, rk_ref, g2_ref,
              wo_ref, wgu_ref, wd_ref, o_ref):
    hsum = _head_sum_matrix(RW_W)
    y = yf_ref[0, rows] + yb_ref[0, rows]
    mu = _dot_exact_rhs(y, hsum) * (1.0 / HEAD_DIM)
    bonus = _dot_exact_rhs(r_ref[0, rows] * (kd0_ref[0, rows] + kd1_ref[0, rows]) * rk_ref[...], hsum, 2) * v_ref[0, rows]
    gate = _dot(jax.nn.sigmoid(gd_ref[0, rows]), g2_ref[...])
    y_mix = _dot(sg_ref[0, rows], wo_ref[:A_W]) + _dot(att_ref[0, rows], wo_ref[A_W:A_W + ATT_Q])
    yield
    yc = y - mu
    var = _dot_exact_rhs(yc * yc, hsum, 2) * (1.0 / HEAD_DIM)
    yield
    yn = yc * lax.rsqrt(var + GN_EPS) * lng_ref[...] + lnb_ref[...]
    o_rw = (yn + bonus) * gate
    y_mix = y_mix + _dot(o_rw, wo_ref[A_W + ATT_Q:])
    yield
    x1 = x_ref[0, rows] + gt1_ref[0] * _rms_norm(y_mix, ng_ref[1:2])
    h2 = (_rms_norm(x1, ng_ref[2:3]) * (1.0 + sc2_ref[0]) + sh2_ref[0]).astype(BF16)
    fc = D_FF // FFN_SPLIT
    acc = None
    for j in range(FFN_SPLIT):
        gg = jnp.dot(h2, wgu_ref[:, j * fc:(j + 1) * fc], preferred_element_type=F32)
        uu = jnp.dot(h2, wgu_ref[:, D_FF + j * fc:D_FF + (j + 1) * fc], preferred_element_type=F32)
        yield
        part = _dot(gg * jax.nn.sigmoid(gg) * uu, wd_ref[j * fc:(j + 1) * fc])
        acc = part if acc is None else acc + part
        yield
    o_ref[0, rows] = x1 + gt2_ref[0] * _rms_norm(acc, ng_ref[3:4])


def _out_kernel(nsub, *refs):
    tm = refs[0].shape[1]
    sub = tm // nsub
    _round_robin([_out_rows(pl.ds(i * sub, sub), *refs) for i in range(nsub)])


def _mix_out_ffn(x, o_sg, o_att, yf, yb, r, kd0, kd1, v, gd, gt1, sh2, sc2, gt2, norm_g,
                 lnx_g, lnx_b, rk, g2_bf16, wo_bf16, wgu_bf16, wd_bf16, layer, tm):
    b, t, d = x.shape
    row = lambda bi, i: (bi, i, 0)
    per_b = lambda bi, i: (bi, 0, 0)
    const2 = lambda bi, i: (0, 0)
    single = pl.Buffered(1)
    stacked = lambda bi, i: (layer, 0, 0)
    in_specs = [
        pl.BlockSpec((1, tm, d), row),
        pl.BlockSpec((1, tm, A_W), row),
        pl.BlockSpec((1, tm, ATT_Q), row),
    ] + [pl.BlockSpec((1, tm, RW_W), row)] * 6 + [
        pl.BlockSpec((1, tm, G_LORA), row),
    ] + [pl.BlockSpec((1, 1, d), per_b)] * 4 + [
        pl.BlockSpec((4, d), const2),
        pl.BlockSpec((1, RW_W), const2),
        pl.BlockSpec((1, RW_W), const2),
        pl.BlockSpec((1, RW_W), const2),
        pl.BlockSpec((G_LORA, RW_W), const2),
        pl.BlockSpec((None, MIX_W, d), stacked, pipeline_mode=single),
        pl.BlockSpec((None, d, 2 * D_FF), stacked, pipeline_mode=single),
        pl.BlockSpec((None, D_FF, d), stacked, pipeline_mode=single),
    ]
    return pl.pallas_call(
        functools.partial(_out_kernel, OUT_SUBTILES if tm % (OUT_SUBTILES * V7X_SUBLANES) == 0 else 1),
        grid=(b, t // tm),
        in_specs=in_specs,
        out_specs=pl.BlockSpec((1, tm, d), row),
        out_shape=jax.ShapeDtypeStruct((b, t, d), F32),
        compiler_params=_params(2),
        name="mix_out_ffn",
    )(x, o_sg, o_att, yf, yb, r, kd0, kd1, v, gd, gt1, sh2, sc2, gt2, norm_g,
      lnx_g, lnx_b, rk, g2_bf16, wo_bf16, wgu_bf16, wd_bf16)


def _rope_tables(t):
    pos = np.arange(t)
    inv = np.float32(ROPE_THETA) ** (-np.arange(ROPE_FREQS, dtype=np.float32) / np.float32(ROPE_FREQS))
    ang_r = (pos // GRID_W).astype(np.float32)[:, None] * inv
    ang_c = (pos % GRID_W).astype(np.float32)[:, None] * inv
    cos = np.concatenate([np.cos(ang_r)] * 2 + [np.cos(ang_c)] * 2, axis=-1)
    sin = np.concatenate([-np.sin(ang_r), np.sin(ang_r), -np.sin(ang_c), np.sin(ang_c)], axis=-1)
    return (jnp.asarray(np.concatenate([cos, cos], axis=-1), F32),
            jnp.asarray(np.concatenate([sin, sin], axis=-1), F32))


def _row_tile(t, target):
    tm = min(t, target)
    while t % tm:
        tm //= 2
    return tm


def kernel(x, c, ctx, c_ctx, w_mod, b_mod, norm_g, w_in, w_out, sg_ln_g, sg_ln_b, sg_w, sg_b,
           attn_sink, rw_mu, rw_w0, rw_w2, rw_a0, rw_a2, rw_kk, rw_ka, rw_rk, rw_g2,
           rw_lnx_g, rw_lnx_b, w_gu, w_down):
    bsz, t, d = x.shape
    clen = ctx.shape[1]
    depth = w_mod.shape[0]

    rows = -(-(bsz + 1) // V7X_SUBLANES) * V7X_SUBLANES
    c_rows = jnp.zeros((rows, d), F32).at[:bsz].set(c).at[bsz].set(c_ctx)
    mods = _modulation(c_rows, w_mod, b_mod)

    cos, sin = _rope_tables(t)
    cos_c, sin_c = cos[:clen], sin[:clen]
    s_zero = jnp.zeros((bsz, N_DIR, HEAD_DIM, RW_W), F32)

    tm_proj = _row_tile(t, 1024)
    tm_out = _row_tile(t, 512)
    tc_out = _row_tile(bsz * clen, 512)
    nb_scan = SCAN_BATCH if bsz % SCAN_BATCH == 0 else 1

    w_in_b, w_out_b, w_gu_b, w_down_b = (w.astype(BF16) for w in (w_in, w_out, w_gu, w_down))

    xc = ctx
    for l in range(depth):
        last = l == depth - 1
        lat = [mods[l, :bsz, i * d:(i + 1) * d][:, None, :] for i in range(6)]
        cm = [jnp.broadcast_to(mods[l, bsz, i * d:(i + 1) * d][None, None, :], (bsz, 1, d))
              for i in range(6)]
        g = norm_g[l]
        w2_b, a2_b = rw_w2[l].astype(BF16), rw_a2[l].astype(BF16)
        prep_params = (rw_mu[l][None], rw_kk[l][None], rw_ka[l][None], rw_w0[l][:, None, :], w2_b,
                       rw_a0[l][:, None, :], a2_b,
                       sg_ln_g[l][None], sg_ln_b[l][None], sg_w[l].astype(BF16),
                       jnp.repeat(sg_b[l].T, HEAD_DIM, axis=1))

        (oc_sg, c_q, c_kv, c_r, c_gd, c_v, c_kk, c_lw0, c_lw1, c_b0, c_b1, c_kd0, c_kd1) = _proj_prep(
            xc, g[0:1], cm[0], cm[1], w_in_b, l, cos_c, sin_c, *prep_params, False, clen)
        c_yf, c_yb, s_ctx = _scan(c_lw0, c_lw1, c_b0, c_b1, c_kd0, c_kd1, c_kk, c_v, c_r, s_zero, nb_scan)

        (o_sg, p_q, p_kv, r, gd, v, kk, lw0, lw1, b0, b1, kd0, kd1) = _proj_prep(
            x, g[0:1], lat[0], lat[1], w_in_b, l, cos, sin, *prep_params, True, tm_proj)
        yf, yb, _ = _scan(lw0, lw1, b0, b1, kd0, kd1, kk, v, r, s_ctx, nb_scan)

        o_att = _attention(p_q, p_kv, c_kv, attn_sink[l], True)

        out_params = (g, rw_lnx_g[l][None], rw_lnx_b[l][None], rw_rk[l].reshape(1, RW_W),
                      rw_g2[l].astype(BF16), w_out_b, w_gu_b, w_down_b, l)
        x_new = _mix_out_ffn(x, o_sg, o_att, yf, yb, r, kd0, kd1, v, gd,
                             lat[2], lat[3], lat[4], lat[5], *out_params, tm_out)
        if not last:
            oc_att = _attention(c_q, c_kv, c_kv, attn_sink[l], False)
            flat = lambda a: a.reshape(1, bsz * clen, a.shape[-1])
            xc = _mix_out_ffn(*[flat(a) for a in (xc, oc_sg, oc_att, c_yf, c_yb, c_r, c_kd0, c_kd1, c_v, c_gd)],
                              cm[2][:1], cm[3][:1], cm[4][:1], cm[5][:1], *out_params, tc_out).reshape(bsz, clen, d)
        x = x_new
    return x
```

```python
import functools
import math

import jax
import jax.numpy as jnp
import numpy as np
from jax import lax
from jax.experimental import pallas as pl
from jax.experimental.pallas import tpu as pltpu

F32 = jnp.float32
BF16 = jnp.bfloat16

D_MODEL = 1024
HEAD_DIM = 64
GRID_W = 64
A_GROUPS = 4
A_W = A_GROUPS * HEAD_DIM
CHUNK = 128
ATT_HEADS = 8
ATT_KV_HEADS = 2
ATT_Q = ATT_HEADS * HEAD_DIM
ATT_KV = ATT_KV_HEADS * HEAD_DIM
WINDOW = 128
QBLOCK = 128
ROPE_THETA = 10000.0
ROPE_FREQS = HEAD_DIM // 4
NEG_INF = -1e30
ATT_QBLOCKS_PER_STEP = 8
ATT_SCALE = HEAD_DIM ** -0.5
RW_HEADS = 4
RW_W = RW_HEADS * HEAD_DIM
W_LORA = 64
A_LORA = 64
G_LORA = 128
N_DIR = 2
GN_EPS = 64e-5
MIX_W = A_W + ATT_Q + RW_W
D_FF = -(-8 * D_MODEL // (3 * 256)) * 256
RMS_EPS = 1e-6
LN_EPS = 1e-5
OFF_Q = 2 * A_W
OFF_R = OFF_Q + ATT_Q
OFF_CTX = OFF_R + RW_W + G_LORA
OFF_RWK = OFF_CTX + 2 * ATT_KV
IN_COLS = OFF_RWK + 2 * RW_W + N_DIR * (W_LORA + A_LORA)
MU_HEAD = RW_W + G_LORA
RWK_W = 2 * RW_W + N_DIR * (W_LORA + A_LORA)

V7X_LANES = 128
V7X_SUBLANES = 8
V7X_VMEM_LIMIT_BYTES = 56 * 1024 * 1024

SCAN_CHUNK = 64
SCAN_CHUNKS_PER_STEP = 4
SCAN_BATCH = 4
LOG_DECAY_SCALE = -math.exp(-0.5)


def _params(n_grid):
    return pltpu.CompilerParams(
        dimension_semantics=("arbitrary",) * n_grid, vmem_limit_bytes=V7X_VMEM_LIMIT_BYTES)


def _dot(a, b):
    return jnp.dot(a.astype(BF16), b.astype(BF16), preferred_element_type=F32)


def _dot_nt(a, b):
    return lax.dot_general(a.astype(BF16), b.astype(BF16), (((1,), (1,)), ((), ())),
                           preferred_element_type=F32)


def _split_bf16(x, n):
    parts = []
    for _ in range(n - 1):
        h = x.astype(BF16)
        parts.append(h)
        x = x - h.astype(F32)
    parts.append(x.astype(BF16))
    return parts


def _dot_exact_rhs(x, m_bf16, n=3):
    acc = None
    for p in _split_bf16(x, n):
        t = jnp.dot(p, m_bf16, preferred_element_type=F32)
        acc = t if acc is None else acc + t
    return acc


def _dot_exact_lhs(m_bf16, x, n=3):
    acc = None
    for p in _split_bf16(x, n):
        t = jnp.dot(m_bf16, p, preferred_element_type=F32)
        acc = t if acc is None else acc + t
    return acc


def _head_sum_matrix(width):
    i = lax.broadcasted_iota(jnp.int32, (width, width), 0) // HEAD_DIM
    j = lax.broadcasted_iota(jnp.int32, (width, width), 1) // HEAD_DIM
    return jnp.where(i == j, 1.0, 0.0).astype(BF16)


def _round_robin(chains):
    while chains:
        chains = [g for g in chains if next(g, True) is None]


def _rms_norm(x, g):
    return x * lax.rsqrt(jnp.mean(x * x, axis=-1, keepdims=True) + RMS_EPS) * g


def _tile_lanes(x, width):
    reps = width // x.shape[-1]
    return x if reps == 1 else jnp.concatenate([x] * reps, axis=-1)


def _rope(x, cos, sin_signed):
    w = x.shape[-1]
    lane = lax.broadcasted_iota(jnp.int32, x.shape, 1)
    lo = (lane % (2 * ROPE_FREQS)) < ROPE_FREQS
    partner = jnp.where(lo, pltpu.roll(x, w - ROPE_FREQS, 1), pltpu.roll(x, ROPE_FREQS, 1))
    return x * _tile_lanes(cos, w) + partner * _tile_lanes(sin_signed, w)


def _mod_kernel(c_ref, w_ref, b_ref, o_ref):
    c = c_ref[...]
    act = c * jax.nn.sigmoid(c)
    o_ref[0] = _dot(act, w_ref[0]) + b_ref[0]


def _modulation(c_rows, w_mod, b_mod):
    depth, d, n = w_mod.shape
    rows = c_rows.shape[0]
    tn = n // 4
    return pl.pallas_call(
        _mod_kernel,
        grid=(depth, n // tn),
        in_specs=[
            pl.BlockSpec((rows, d), lambda l, j: (0, 0)),
            pl.BlockSpec((1, d, tn), lambda l, j: (l, 0, j)),
            pl.BlockSpec((1, 1, tn), lambda l, j: (l, 0, j)),
        ],
        out_specs=pl.BlockSpec((1, rows, tn), lambda l, j: (l, 0, j)),
        out_shape=jax.ShapeDtypeStruct((depth, rows, n), F32),
        compiler_params=_params(2),
        name="modulation",
    )(c_rows, w_mod, b_mod.reshape(depth, 1, n))


def _shift_rows(x, k):
    return pltpu.roll(x, k % x.shape[0], 0)


def _token_shift(cur, prev_rows, next_rows, grid_mode, first, last):
    tm, width = cur.shape
    ch = lax.broadcasted_iota(jnp.int32, (tm, width), 1)
    pos = lax.broadcasted_iota(jnp.int32, (tm, width), 0)
    if grid_mode:
        qw = width // 4
        col = pos % GRID_W
        left = jnp.where(col == 0, 0.0, _shift_rows(cur, 1))
        right = jnp.where(col == GRID_W - 1, 0.0, _shift_rows(cur, -1))
        up = jnp.concatenate([prev_rows, cur[:tm - GRID_W]], axis=0)
        up = jnp.where((pos < GRID_W) & first, 0.0, up)
        down = jnp.concatenate([cur[GRID_W:], next_rows], axis=0)
        down = jnp.where((pos >= tm - GRID_W) & last, 0.0, down)
        return jnp.where(ch < qw, left, jnp.where(ch < 2 * qw, right, jnp.where(ch < 3 * qw, up, down)))
    prev = jnp.where(pos == 0, 0.0, _shift_rows(cur, 1))
    nxt = jnp.where(pos == tm - 1, 0.0, _shift_rows(cur, -1))
    return jnp.where(ch < width // 2, prev, nxt)


def _lane_slab(lo, hi):
    return (lo // V7X_LANES) * V7X_LANES, -(-hi // V7X_LANES) * V7X_LANES


def _halo_columns(hb, w_ref, off, width, quarter):
    qw = width // 4
    lo, hi = _lane_slab(quarter * qw, (quarter + 1) * qw)
    p = jnp.dot(hb, w_ref[:, off + lo:off + hi], preferred_element_type=F32)
    parts = []
    if lo:
        parts.append(jnp.zeros((p.shape[0], lo), F32))
    parts.append(p)
    if hi < width:
        parts.append(jnp.zeros((p.shape[0], width - hi), F32))
    return jnp.concatenate(parts, axis=1)


def _spatial_gating(z, ln_g, ln_b, w_ref, bias):
    z = jax.nn.gelu(z, approximate=True)
    u, v = z[:, :A_W], z[:, A_W:]
    mu = jnp.mean(v, axis=-1, keepdims=True)
    var = jnp.mean(jnp.square(v - mu), axis=-1, keepdims=True)
    vn = ((v - mu) * lax.rsqrt(var + LN_EPS) * ln_g + ln_b).astype(BF16)
    outs = []
    for c in range(z.shape[0] // CHUNK):
        rows = slice(c * CHUNK, (c + 1) * CHUNK)
        mixed = jnp.concatenate(
            [jnp.dot(w_ref[gi], vn[rows, gi * HEAD_DIM:(gi + 1) * HEAD_DIM],
                     preferred_element_type=F32) for gi in range(A_GROUPS)], axis=-1)
        outs.append(u[rows] * (mixed + bias))
    return jnp.concatenate(outs, axis=0)


def _proj_prep_kernel(grid_mode, nblk, *refs):
    if grid_mode:
        x_ref, xp_ref, xn_ref = refs[:3]
        refs = refs[3:]
    else:
        x_ref = refs[0]
        refs = refs[1:]
    (g_ref, sh_ref, sc_ref, w_ref, cos_ref, sin_ref,
     mu_rg_ref, mu_rwk_ref, kkp_ref, ka_ref, w0_ref, w2_ref, a0_ref, a2_ref,
     sg_g_ref, sg_b_ref, sg_w_ref, sg_bias_ref,
     osg_ref, pq_ref, pkv_ref,
     r_ref, gd_ref, v_ref, kk_ref, lw0_ref, lw1_ref, b0_ref, b1_ref, kd0_ref, kd1_ref) = refs
    i = pl.program_id(1)
    first, last = i == 0, i == nblk - 1

    def normed(x):
        return (_rms_norm(x, g_ref[...]) * (1.0 + sc_ref[0]) + sh_ref[0]).astype(BF16)

    def project(hb, start, width):
        return jnp.dot(hb, w_ref[:, start:start + width], preferred_element_type=F32)

    hb = normed(x_ref[0])
    rg = project(hb, OFF_R, MU_HEAD)
    rwk = project(hb, OFF_RWK, RWK_W)
    if grid_mode:
        hp, hn = normed(xp_ref[0]), normed(xn_ref[0])
        halos = [_halo_columns(hp, w_ref, OFF_R, MU_HEAD, 2), _halo_columns(hn, w_ref, OFF_R, MU_HEAD, 3),
                 _halo_columns(hp, w_ref, OFF_RWK, RWK_W, 2), _halo_columns(hn, w_ref, OFF_RWK, RWK_W, 3)]
    q = project(hb, OFF_Q, ATT_Q)
    kv = project(hb, OFF_CTX, 2 * ATT_KV)
    z_a = project(hb, 0, OFF_Q)
    if grid_mode:
        q = _rope(q, cos_ref[...], sin_ref[...])
        kv = jnp.concatenate([_rope(kv[:, :ATT_KV], cos_ref[...], sin_ref[...]), kv[:, ATT_KV:]], axis=-1)
    pq_ref[0] = (q * ATT_SCALE).astype(BF16)
    pkv_ref[0] = kv.astype(BF16)
    if grid_mode:
        rg_s = _token_shift(rg, halos[0], halos[1], True, first, last)
        rwk_s = _token_shift(rwk, halos[2], halos[3], True, first, last)
    else:
        rg_s = _token_shift(rg, None, None, False, first, last)
        rwk_s = _token_shift(rwk, None, None, False, first, last)
    rg = rg + (rg_s - rg) * mu_rg_ref[...]
    rwk = rwk + (rwk_s - rwk) * mu_rwk_ref[...]
    r_ref[0] = rg[:, :RW_W]
    gd_ref[0] = rg[:, RW_W:]
    k = rwk[:, :RW_W]
    v_ref[0] = rwk[:, RW_W:2 * RW_W]
    kk = k * kkp_ref[...]
    norm = jnp.sqrt(_dot_exact_rhs(kk * kk, _head_sum_matrix(RW_W)))
    kk = kk / jnp.maximum(norm, 1e-12)
    kk_ref[0] = kk
    base = 2 * RW_W
    lws = []
    for d, (b_ref, kd_ref) in enumerate(((b0_ref, kd0_ref), (b1_ref, kd1_ref))):
        wd = rwk[:, base + d * W_LORA:base + (d + 1) * W_LORA]
        ad = rwk[:, base + N_DIR * W_LORA + d * A_LORA:base + N_DIR * W_LORA + (d + 1) * A_LORA]
        z = w0_ref[d] + _dot(jnp.tanh(wd), w2_ref[d])
        lws.append(LOG_DECAY_SCALE * jax.nn.sigmoid(z))
        a = jax.nn.sigmoid(a0_ref[d] + _dot(ad, a2_ref[d]))
        b_ref[0] = kk * a
        kd_ref[0] = k * (1.0 + (a - 1.0) * ka_ref[...])
    osg_ref[0] = _spatial_gating(z_a, sg_g_ref[...], sg_b_ref[...], sg_w_ref, sg_bias_ref[...])
    ci = lax.broadcasted_iota(jnp.int32, (SCAN_CHUNK, SCAN_CHUNK), 0)
    cj = lax.broadcasted_iota(jnp.int32, (SCAN_CHUNK, SCAN_CHUNK), 1)
    for d, lw_ref in enumerate((lw0_ref, lw1_ref)):
        cum = jnp.where((ci <= cj) if d else (ci >= cj), 1.0, 0.0).astype(BF16)
        for c in range(lws[d].shape[0] // SCAN_CHUNK):
            rows = slice(c * SCAN_CHUNK, (c + 1) * SCAN_CHUNK)
            lw_ref[0, rows] = _dot_exact_lhs(cum, lws[d][rows])


def _proj_prep(x, g, sh, sc, w_bf16, layer, cos, sin_signed, mu, kkp, ka, w0, w2_bf16, a0, a2_bf16,
               sg_ln_g, sg_ln_b, sg_w_bf16, sg_bias, grid_mode, tm):
    b, t, d = x.shape
    nblk = t // tm
    assert tm % SCAN_CHUNK == 0 and tm % CHUNK == 0 and (grid_mode or nblk == 1)
    row = lambda bi, i: (bi, i, 0)
    per_b = lambda bi, i: (bi, 0, 0)
    const2 = lambda bi, i: (0, 0)
    const3 = lambda bi, i: (0, 0, 0)
    in_specs, args = [pl.BlockSpec((1, tm, d), row)], [x]
    if grid_mode:
        per = tm // GRID_W
        nrow = t // GRID_W
        in_specs += [
            pl.BlockSpec((1, GRID_W, d), lambda bi, i: (bi, jnp.maximum(i * per - 1, 0), 0)),
            pl.BlockSpec((1, GRID_W, d), lambda bi, i: (bi, jnp.minimum((i + 1) * per, nrow - 1), 0)),
        ]
        args += [x, x]
    in_specs += [
        pl.BlockSpec((1, d), const2),
        pl.BlockSpec((1, 1, d), per_b),
        pl.BlockSpec((1, 1, d), per_b),
        pl.BlockSpec((None, d, IN_COLS), lambda bi, i: (layer, 0, 0)),
        pl.BlockSpec((tm, V7X_LANES), lambda bi, i: (i, 0)),
        pl.BlockSpec((tm, V7X_LANES), lambda bi, i: (i, 0)),
        pl.BlockSpec((1, MU_HEAD), const2),
        pl.BlockSpec((1, RWK_W), const2),
        pl.BlockSpec((1, RW_W), const2),
        pl.BlockSpec((1, RW_W), const2),
        pl.BlockSpec((N_DIR, 1, RW_W), const3),
        pl.BlockSpec((N_DIR, W_LORA, RW_W), const3),
        pl.BlockSpec((N_DIR, 1, RW_W), const3),
        pl.BlockSpec((N_DIR, A_LORA, RW_W), const3),
        pl.BlockSpec((1, A_W), const2),
        pl.BlockSpec((1, A_W), const2),
        pl.BlockSpec((A_GROUPS, CHUNK, CHUNK), const3),
        pl.BlockSpec((CHUNK, A_W), const2),
    ]
    args += [g, sh, sc, w_bf16, cos, sin_signed, mu[:, :MU_HEAD], mu[:, MU_HEAD:], kkp, ka,
             w0, w2_bf16, a0, a2_bf16, sg_ln_g, sg_ln_b, sg_w_bf16, sg_bias]
    widths = [A_W, ATT_Q, 2 * ATT_KV, RW_W, G_LORA] + [RW_W] * 8
    return pl.pallas_call(
        functools.partial(_proj_prep_kernel, grid_mode, nblk),
        grid=(b, nblk),
        in_specs=in_specs,
        out_specs=[pl.BlockSpec((1, tm, w), row) for w in widths],
        out_shape=[jax.ShapeDtypeStruct((b, t, w), BF16 if i in (1, 2) else F32) for i, w in enumerate(widths)],
        compiler_params=_params(2),
        name="proj_prep",
    )(*args)


def _attn_kernel(local, nblk, nq, sink_ref, q_ref, *refs):
    if local:
        kvp_ref, kvc_ref, kvn_ref, ctx_ref, o_ref = refs
    else:
        ctx_ref, o_ref = refs
    step = pl.program_id(1)
    group = ATT_HEADS // ATT_KV_HEADS
    ctx = ctx_ref[0]
    rows = group * QBLOCK
    row_head = lax.broadcasted_iota(jnp.int32, (rows, 1), 0) // QBLOCK
    if local:
        kv_all = jnp.concatenate([kvp_ref[0], kvc_ref[0], kvn_ref[0]], axis=0)
        qi = lax.broadcasted_iota(jnp.int32, (rows, 3 * QBLOCK), 0) % QBLOCK
        kj = lax.broadcasted_iota(jnp.int32, (rows, 3 * QBLOCK), 1)
        blk = kj // QBLOCK
        in_band = (jnp.abs(kj - QBLOCK - qi) <= WINDOW)
    sinks = []
    for g in range(ATT_KV_HEADS):
        sink = jnp.zeros((rows, 1), F32)
        for i in range(group):
            sink = jnp.where(row_head == i, sink_ref[g * group + i], sink)
        sinks.append(sink)
    units = [(j, g) for j in range(nq) for g in range(ATT_KV_HEADS)]
    s_ctx, s_loc, kv_loc = {}, {}, {}
    for j, g in units:
        q = q_ref[0, j * QBLOCK:(j + 1) * QBLOCK]
        qs = jnp.concatenate(
            [q[:, (g * group + i) * HEAD_DIM:(g * group + i + 1) * HEAD_DIM] for i in range(group)],
            axis=0)
        s_ctx[j, g] = _dot_nt(qs, ctx[:, g * HEAD_DIM:(g + 1) * HEAD_DIM])
        if local:
            n = step * nq + j
            valid = in_band & ((blk != 0) | (n > 0)) & ((blk != 2) | (n < nblk - 1))
            kv_loc[j] = kv_all[j * QBLOCK:(j + 3) * QBLOCK]
            s_loc[j, g] = jnp.where(valid, _dot_nt(qs, kv_loc[j][:, g * HEAD_DIM:(g + 1) * HEAD_DIM]), NEG_INF)
    p_ctx, p_loc, p_sink = {}, {}, {}
    for u in units:
        g = u[1]
        m = jnp.maximum(jnp.max(s_ctx[u], axis=-1, keepdims=True), sinks[g])
        if local:
            m = jnp.maximum(m, jnp.max(s_loc[u], axis=-1, keepdims=True))
        p_ctx[u] = jnp.exp(s_ctx[u] - m).astype(BF16)
        if local:
            p_loc[u] = jnp.exp(s_loc[u] - m).astype(BF16)
        p_sink[u] = jnp.exp(sinks[g] - m)

    def with_ones(v):
        return jnp.concatenate([v.astype(BF16), jnp.ones(v.shape, BF16)], axis=1)

    for u in units:
        j, g = u
        acc = _dot(p_ctx[u], with_ones(ctx[:, ATT_KV + g * HEAD_DIM:ATT_KV + (g + 1) * HEAD_DIM]))
        if local:
            acc = acc + _dot(p_loc[u], with_ones(kv_loc[j][:, ATT_KV + g * HEAD_DIM:ATT_KV + (g + 1) * HEAD_DIM]))
        out = acc[:, :HEAD_DIM] / (acc[:, HEAD_DIM:] + p_sink[u])
        for i in range(group):
            h = g * group + i
            o_ref[0, j * QBLOCK:(j + 1) * QBLOCK, h * HEAD_DIM:(h + 1) * HEAD_DIM] = out[i * QBLOCK:(i + 1) * QBLOCK]


def _attention(q, kv, kv_ctx, sink, local):
    b, t, _ = q.shape
    c = kv_ctx.shape[1]
    nblk = t // QBLOCK
    nq = ATT_QBLOCKS_PER_STEP if nblk % ATT_QBLOCKS_PER_STEP == 0 else 1
    row = lambda bi, i: (bi, i, 0)
    in_specs = [
        pl.BlockSpec(memory_space=pltpu.SMEM),
        pl.BlockSpec((1, nq * QBLOCK, ATT_Q), row),
    ]
    args = [sink, q]
    if local:
        in_specs += [
            pl.BlockSpec((1, QBLOCK, 2 * ATT_KV), lambda bi, i: (bi, jnp.maximum(i * nq - 1, 0), 0)),
            pl.BlockSpec((1, nq * QBLOCK, 2 * ATT_KV), row),
            pl.BlockSpec((1, QBLOCK, 2 * ATT_KV), lambda bi, i: (bi, jnp.minimum((i + 1) * nq, nblk - 1), 0)),
        ]
        args += [kv, kv, kv]
    in_specs.append(pl.BlockSpec((1, c, 2 * ATT_KV), lambda bi, i: (bi, 0, 0)))
    args.append(kv_ctx)
    return pl.pallas_call(
        functools.partial(_attn_kernel, local, nblk, nq),
        grid=(b, nblk // nq),
        in_specs=in_specs,
        out_specs=pl.BlockSpec((1, nq * QBLOCK, ATT_Q), row),
        out_shape=jax.ShapeDtypeStruct((b, t, ATT_Q), F32),
        compiler_params=_params(2),
        name="attention",
    )(*args)


def _scan_direction(reverse, lw_cum, b, kd, kk, v, r, st_ref, y_ref):
    L = SCAN_CHUNK
    ti = lax.broadcasted_iota(jnp.int32, (L, RW_W), 0)
    if reverse:
        lw_before = jnp.where(ti == L - 1, 0.0, pltpu.roll(lw_cum, L - 1, 0))
    else:
        lw_before = jnp.where(ti == 0, 0.0, pltpu.roll(lw_cum, 1, 0))
    lw_tot = lw_cum[0:1] if reverse else lw_cum[L - 1:L]
    e_neg = jnp.exp(-lw_cum)
    e_end = jnp.exp(lw_tot - lw_cum)

    lane = lax.broadcasted_iota(jnp.int32, (L, RW_W), 1)
    si = lane % HEAD_DIM
    strict = (ti < si) if reverse else (ti > si)
    incl = (ti <= si) if reverse else (ti >= si)
    eye = ti == si
    row = lax.broadcasted_iota(jnp.int32, (RW_W, RW_W), 0)
    col = lax.broadcasted_iota(jnp.int32, (RW_W, RW_W), 1)
    blk = (row // HEAD_DIM) == (col // HEAD_DIM)

    def block_diag(x):
        return jnp.where(blk, jnp.concatenate([x] * RW_HEADS, axis=0), 0.0).astype(BF16)

    def head_transpose(x):
        xt = x.T
        return jnp.concatenate([xt[h * HEAD_DIM:(h + 1) * HEAD_DIM] for h in range(RW_HEADS)], axis=1)

    alpha = kk * jnp.exp(lw_before)
    rho = r * jnp.exp(lw_cum)
    beta_end_t = head_transpose(b * e_end)
    kappa_end_t = head_transpose(kd * e_end)
    v_bd = block_diag(v)
    alpha_bd = block_diag(alpha)
    p1 = _dot_nt(jnp.concatenate([alpha, rho], axis=0),
                 jnp.concatenate([block_diag(b * e_neg), block_diag(kd * e_neg)], axis=0))
    yield
    nn = jnp.where(strict, p1[:L, :RW_W], 0.0)
    ak = jnp.where(strict, p1[:L, RW_W:], 0.0)
    rb = jnp.where(incl, p1[L:, :RW_W], 0.0)
    rk = jnp.where(incl, p1[L:, RW_W:], 0.0)
    tinv = jnp.where(eye, 1.0, 0.0) - nn
    npow = _dot(nn, block_diag(nn))
    kv = _dot(jnp.concatenate([ak, rk, kappa_end_t], axis=0), v_bd)
    akv, rkv, kev = kv[:L], kv[L:2 * L], kv[2 * L:]
    yield
    for _ in range(int(math.log2(L)) - 2):
        both = _dot(jnp.concatenate([npow, tinv], axis=0), block_diag(npow))
        npow, tinv = both[:L], tinv + both[L:]
        yield
    tinv = tinv + _dot(tinv, block_diag(npow))
    yield
    u = _dot(jnp.concatenate([rb, beta_end_t], axis=0), block_diag(tinv))
    yield
    bz = _dot(u, jnp.concatenate([alpha_bd, block_diag(-akv)], axis=1))
    q_mat = rho - bz[:L, :RW_W]
    y0 = bz[:L, RW_W:] + rkv
    m_mat = jnp.where(eye, jnp.exp(lw_tot), 0.0) - bz[L:, :RW_W]
    c_mat = bz[L:, RW_W:] + kev
    yield
    both = _dot(jnp.concatenate([q_mat, m_mat], axis=0), block_diag(st_ref[...]))
    y_ref[...] = both[:L] + y0
    st_ref[...] = both[L:] + c_mat


def _scan_kernel(nchunk, nbatch, nsub, lw0_ref, b0_ref, kd0_ref, kkf_ref, vf_ref, rf_ref,
                 lw1_ref, b1_ref, kd1_ref, kkb_ref, vb_ref, rb_ref, s0_ref,
                 yf_ref, yb_ref, sT_ref, st_ref):
    c = pl.program_id(1)

    @pl.when(c == 0)
    def _():
        st_ref[...] = s0_ref[...]

    def chain(reverse, bi, refs, y_ref):
        order = range(nsub - 1, -1, -1) if reverse else range(nsub)
        for sub in order:
            rows = pl.ds(sub * SCAN_CHUNK, SCAN_CHUNK)
            yield from _scan_direction(reverse, *[ref[bi, rows] for ref in refs],
                                       st_ref.at[bi, int(reverse)], y_ref.at[bi, rows])

    chains = []
    for bi in range(nbatch):
        chains.append(chain(False, bi, (lw0_ref, b0_ref, kd0_ref, kkf_ref, vf_ref, rf_ref), yf_ref))
        chains.append(chain(True, bi, (lw1_ref, b1_ref, kd1_ref, kkb_ref, vb_ref, rb_ref), yb_ref))
    _round_robin(chains)

    @pl.when(c == nchunk - 1)
    def _():
        sT_ref[...] = st_ref[...]


def _scan(lw0, lw1, b0, b1, kd0, kd1, kk, v, r, s0, nbatch):
    assert SCAN_CHUNK == HEAD_DIM
    bsz, t, _ = kk.shape
    nsub = SCAN_CHUNKS_PER_STEP if (t // SCAN_CHUNK) % SCAN_CHUNKS_PER_STEP == 0 else 1
    nchunk = t // (SCAN_CHUNK * nsub)
    fwd = lambda bi, c: (bi, c, 0)
    bwd = lambda bi, c: (bi, nchunk - 1 - c, 0)
    blk = (nbatch, nsub * SCAN_CHUNK, RW_W)
    st_blk = (nbatch, N_DIR, HEAD_DIM, RW_W)
    st_map = lambda bi, c: (bi, 0, 0, 0)
    return pl.pallas_call(
        functools.partial(_scan_kernel, nchunk, nbatch, nsub),
        grid=(bsz // nbatch, nchunk),
        in_specs=[pl.BlockSpec(blk, fwd)] * 6 + [pl.BlockSpec(blk, bwd)] * 6
        + [pl.BlockSpec(st_blk, st_map)],
        out_specs=[pl.BlockSpec(blk, fwd), pl.BlockSpec(blk, bwd), pl.BlockSpec(st_blk, st_map)],
        out_shape=[jax.ShapeDtypeStruct((bsz, t, RW_W), F32),
                   jax.ShapeDtypeStruct((bsz, t, RW_W), F32),
                   jax.ShapeDtypeStruct(s0.shape, F32)],
        scratch_shapes=[pltpu.VMEM(st_blk, F32)],
        compiler_params=_params(2),
        name="rwkv_scan",
    )(lw0, b0, kd0, kk, v, r, lw1, b1, kd1, kk, v, r, s0)


FFN_SPLIT = 1
OUT_SUBTILES = 2


def _out_rows(rows, x_ref, sg_ref, att_ref, yf_ref, yb_ref, r_ref, kd0_ref, kd1_ref, v_ref, gd_ref,
              gt1_ref, sh2_ref, sc2_ref, gt2_ref, ng_ref, lng_ref, lnb_ref, rk_ref, g2_ref,
              wo_ref, wgu_ref, wd_ref, o_ref):
    hsum = _head_sum_matrix(RW_W)
    y = yf_ref[0, rows] + yb_ref[0, rows]
    mu = _dot_exact_rhs(y, hsum) * (1.0 / HEAD_DIM)
    bonus = _dot_exact_rhs(r_ref[0, rows] * (kd0_ref[0, rows] + kd1_ref[0, rows]) * rk_ref[...], hsum, 2) * v_ref[0, rows]
    gate = _dot(jax.nn.sigmoid(gd_ref[0, rows]), g2_ref[...])
    y_mix = _dot(sg_ref[0, rows], wo_ref[:A_W]) + _dot(att_ref[0, rows], wo_ref[A_W:A_W + ATT_Q])
    yield
    yc = y - mu
    var = _dot_exact_rhs(yc * yc, hsum, 2) * (1.0 / HEAD_DIM)
    yield
    yn = yc * lax.rsqrt(var + GN_EPS) * lng_ref[...] + lnb_ref[...]
    o_rw = (yn + bonus) * gate
    y_mix = y_mix + _dot(o_rw, wo_ref[A_W + ATT_Q:])
    yield
    x1 = x_ref[0, rows] + gt1_ref[0] * _rms_norm(y_mix, ng_ref[1:2])
    h2 = (_rms_norm(x1, ng_ref[2:3]) * (1.0 + sc2_ref[0]) + sh2_ref[0]).astype(BF16)
    fc = D_FF // FFN_SPLIT
    acc = None
    for j in range(FFN_SPLIT):
        gg = jnp.dot(h2, wgu_ref[:, j * fc:(j + 1) * fc], preferred_element_type=F32)
        uu = jnp.dot(h2, wgu_ref[:, D_FF + j * fc:D_FF + (j + 1) * fc], preferred_element_type=F32)
        yield
        part = _dot(gg * jax.nn.sigmoid(gg) * uu, wd_ref[j * fc:(j + 1) * fc])
        acc = part if acc is None else acc + part
        yield
    o_ref[0, rows] = x1 + gt2_ref[0] * _rms_norm(acc, ng_ref[3:4])


def _out_kernel(nsub, *refs):
    tm = refs[0].shape[1]
    sub = tm // nsub
    _round_robin([_out_rows(pl.ds(i * sub, sub), *refs) for i in range(nsub)])


def _mix_out_ffn(x, o_sg, o_att, yf, yb, r, kd0, kd1, v, gd, gt1, sh2, sc2, gt2, norm_g,
                 lnx_g, lnx_b, rk, g2_bf16, wo_bf16, wgu_bf16, wd_bf16, layer, tm):
    b, t, d = x.shape
    row = lambda bi, i: (bi, i, 0)
    per_b = lambda bi, i: (bi, 0, 0)
    const2 = lambda bi, i: (0, 0)
    single = pl.Buffered(1)
    stacked = lambda bi, i: (layer, 0, 0)
    in_specs = [
        pl.BlockSpec((1, tm, d), row),
        pl.BlockSpec((1, tm, A_W), row),
        pl.BlockSpec((1, tm, ATT_Q), row),
    ] + [pl.BlockSpec((1, tm, RW_W), row)] * 6 + [
        pl.BlockSpec((1, tm, G_LORA), row),
    ] + [pl.BlockSpec((1, 1, d), per_b)] * 4 + [
        pl.BlockSpec((4, d), const2),
        pl.BlockSpec((1, RW_W), const2),
        pl.BlockSpec((1, RW_W), const2),
        pl.BlockSpec((1, RW_W), const2),
        pl.BlockSpec((G_LORA, RW_W), const2),
        pl.BlockSpec((None, MIX_W, d), stacked, pipeline_mode=single),
        pl.BlockSpec((None, d, 2 * D_FF), stacked, pipeline_mode=single),
        pl.BlockSpec((None, D_FF, d), stacked, pipeline_mode=single),
    ]
    return pl.pallas_call(
        functools.partial(_out_kernel, OUT_SUBTILES if tm % (OUT_SUBTILES * V7X_SUBLANES) == 0 else 1),
        grid=(b, t // tm),
        in_specs=in_specs,
        out_specs=pl.BlockSpec((1, tm, d), row),
        out_shape=jax.ShapeDtypeStruct((b, t, d), F32),
        compiler_params=_params(2),
        name="mix_out_ffn",
    )(x, o_sg, o_att, yf, yb, r, kd0, kd1, v, gd, gt1, sh2, sc2, gt2, norm_g,
      lnx_g, lnx_b, rk, g2_bf16, wo_bf16, wgu_bf16, wd_bf16)


def _rope_tables(t):
    pos = np.arange(t)
    inv = np.float32(ROPE_THETA) ** (-np.arange(ROPE_FREQS, dtype=np.float32) / np.float32(ROPE_FREQS))
    ang_r = (pos // GRID_W).astype(np.float32)[:, None] * inv
    ang_c = (pos % GRID_W).astype(np.float32)[:, None] * inv
    cos = np.concatenate([np.cos(ang_r)] * 2 + [np.cos(ang_c)] * 2, axis=-1)
    sin = np.concatenate([-np.sin(ang_r), np.sin(ang_r), -np.sin(ang_c), np.sin(ang_c)], axis=-1)
    return (jnp.asarray(np.concatenate([cos, cos], axis=-1), F32),
            jnp.asarray(np.concatenate([sin, sin], axis=-1), F32))


def _row_tile(t, target):
    tm = min(t, target)
    while t % tm:
        tm //= 2
    return tm


def kernel(x, c, ctx, c_ctx, w_mod, b_mod, norm_g, w_in, w_out, sg_ln_g, sg_ln_b, sg_w, sg_b,
           attn_sink, rw_mu, rw_w0, rw_w2, rw_a0, rw_a2, rw_kk, rw_ka, rw_rk, rw_g2,
           rw_lnx_g, rw_lnx_b, w_gu, w_down):
    bsz, t, d = x.shape
    clen = ctx.shape[1]
    depth = w_mod.shape[0]

    rows = -(-(bsz + 1) // V7X_SUBLANES) * V7X_SUBLANES
    c_rows = jnp.zeros((rows, d), F32).at[:bsz].set(c).at[bsz].set(c_ctx)
    mods = _modulation(c_rows, w_mod, b_mod)

    cos, sin = _rope_tables(t)
    cos_c, sin_c = cos[:clen], sin[:clen]
    s_zero = jnp.zeros((bsz, N_DIR, HEAD_DIM, RW_W), F32)

    tm_proj = _row_tile(t, 1024)
    tm_out = _row_tile(t, 512)
    tc_out = _row_tile(bsz * clen, 512)
    nb_scan = SCAN_BATCH if bsz % SCAN_BATCH == 0 else 1

    w_in_b, w_out_b, w_gu_b, w_down_b = (w.astype(BF16) for w in (w_in, w_out, w_gu, w_down))

    xc = ctx
    for l in range(depth):
        last = l == depth - 1
        lat = [mods[l, :bsz, i * d:(i + 1) * d][:, None, :] for i in range(6)]
        cm = [jnp.broadcast_to(mods[l, bsz, i * d:(i + 1) * d][None, None, :], (bsz, 1, d))
              for i in range(6)]
        g = norm_g[l]
        w2_b, a2_b = rw_w2[l].astype(BF16), rw_a2[l].astype(BF16)
        prep_params = (rw_mu[l][None], rw_kk[l][None], rw_ka[l][None], rw_w0[l][:, None, :], w2_b,
                       rw_a0[l][:, None, :], a2_b,
                       sg_ln_g[l][None], sg_ln_b[l][None], sg_w[l].astype(BF16),
                       jnp.repeat(sg_b[l].T, HEAD_DIM, axis=1))

        (oc_sg, c_q, c_kv, c_r, c_gd, c_v, c_kk, c_lw0, c_lw1, c_b0, c_b1, c_kd0, c_kd1) = _proj_prep(
            xc, g[0:1], cm[0], cm[1], w_in_b, l, cos_c, sin_c, *prep_params, False, clen)
        c_yf, c_yb, s_ctx = _scan(c_lw0, c_lw1, c_b0, c_b1, c_kd0, c_kd1, c_kk, c_v, c_r, s_zero, nb_scan)

        (o_sg, p_q, p_kv, r, gd, v, kk, lw0, lw1, b0, b1, kd0, kd1) = _proj_prep(
            x, g[0:1], lat[0], lat[1], w_in_b, l, cos, sin, *prep_params, True, tm_proj)
        yf, yb, _ = _scan(lw0, lw1, b0, b1, kd0, kd1, kk, v, r, s_ctx, nb_scan)

        o_att = _attention(p_q, p_kv, c_kv, attn_sink[l], True)

        out_params = (g, rw_lnx_g[l][None], rw_lnx_b[l][None], rw_rk[l].reshape(1, RW_W),
                      rw_g2[l].astype(BF16), w_out_b, w_gu_b, w_down_b, l)
        x_new = _mix_out_ffn(x, o_sg, o_att, yf, yb, r, kd0, kd1, v, gd,
                             lat[2], lat[3], lat[4], lat[5], *out_params, tm_out)
        if not last:
            oc_att = _attention(c_q, c_kv, c_kv, attn_sink[l], False)
            flat = lambda a: a.reshape(1, bsz * clen, a.shape[-1])
            xc = _mix_out_ffn(*[flat(a) for a in (xc, oc_sg, oc_att, c_yf, c_yb, c_r, c_kd0, c_kd1, c_v, c_gd)],
                              cm[2][:1], cm[3][:1], cm[4][:1], cm[5][:1], *out_params, tc_out).reshape(bsz, clen, d)
        x = x_new
    return x
```
